```python
import math
import jax, jax.numpy as jnp
from jax import lax
import numpy as np

D_MODEL = 2048
BATCH = 2
SEQ = 4096
DEPTH = 4

N_MIXERS = 3
EPS = 1e-6
NEG_INF = -1e30
RET_HEADS = 8
RET_DK = D_MODEL // RET_HEADS
RET_DV = 2 * RET_DK
RET_CHUNK = 128
ROPE_BASE = 10000.0
RET_BWD_DECAY_OFFSET = 0.5
DIL_PATTERNS = ((128, 1), (512, 4), (2048, 16))
DIL_HEADS = 16
DIL_DH = D_MODEL // DIL_HEADS
REL_BUCKETS = 32
REL_MAX_DIST = 1024
DN_K_HEADS = 16
DN_V_HEADS = 32
DN_DK = 128
DN_DV = 128
DN_CONV = 5
DN_CHUNK = 64
N_EXPERTS = 16
EXPERT_FF = D_MODEL // 2
CAPACITY_FACTOR = 2

kernel_name = "hybrid_retention_dilated_deltanet_ecmoe_encoder"


def rmsnorm(x, w):
    xf = x.astype(jnp.float32)
    y = xf * lax.rsqrt(jnp.mean(xf * xf, axis=-1, keepdims=True) + EPS)
    return (y * w.astype(jnp.float32)).astype(x.dtype)


def head_rmsnorm(x):
    xf = x.astype(jnp.float32)
    return xf * lax.rsqrt(jnp.mean(xf * xf, axis=-1, keepdims=True) + EPS)


def l2norm(x):
    return x * lax.rsqrt(jnp.sum(x * x, axis=-1, keepdims=True) + EPS)


def rotary(x, pos):
    half = x.shape[-1] // 2
    inv = ROPE_BASE ** (-jnp.arange(half, dtype=jnp.float32) / half)
    ang = pos[:, None] * inv[None, :]
    cos, sin = jnp.cos(ang)[:, None, :], jnp.sin(ang)[:, None, :]
    x1, x2 = x[..., :half], x[..., half:]
    return jnp.concatenate([x1 * cos - x2 * sin, x2 * cos + x1 * sin], axis=-1)


def retention_log_decay(offset):
    return jnp.log1p(-jnp.exp2(-(5.0 + offset) - jnp.arange(RET_HEADS, dtype=jnp.float32)))


def retention_chunkwise(q, k, v, lg, strict):
    B, H, S, dk = q.shape
    dv = v.shape[-1]
    Q = RET_CHUNK
    nc = S // Q
    t = jnp.arange(Q, dtype=jnp.float32)
    diff = t[:, None] - t[None, :]
    mask = (diff > 0) if strict else (diff >= 0)
    dmat = jnp.exp(jnp.where(mask[None], diff[None] * lg[:, None, None], -jnp.inf))
    inner = jnp.exp((t[None, :] + 1.0) * lg[:, None])
    kdec = jnp.exp((Q - 1.0 - t)[None, :] * lg[:, None])
    cdec = jnp.exp(Q * lg)

    def to_chunks(a):
        return jnp.moveaxis(a.reshape(B, H, nc, Q, a.shape[-1]), 2, 0)

    def step(state, xs):
        qc, kc, vc = xs
        intra = jnp.einsum('bhqk,bhke->bhqe', jnp.einsum('bhqd,bhkd->bhqk', qc, kc) * dmat[None], vc)
        inter = jnp.einsum('bhqd,bhde->bhqe', qc, state) * inner[None, :, :, None]
        state = state * cdec[None, :, None, None] + jnp.einsum('bhkd,bhke->bhde', kc * kdec[None, :, :, None], vc)
        return state, intra + inter

    s0 = jnp.zeros((B, H, dk, dv), jnp.float32)
    _, ys = lax.scan(step, s0, (to_chunks(q), to_chunks(k), to_chunks(v)))
    return jnp.moveaxis(ys, 0, 2).reshape(B, H, S, dv)


def retention_mixer(h, w_in, w_out):
    B, S, _ = h.shape
    qk_w = RET_HEADS * RET_DK
    v_w = RET_HEADS * RET_DV
    proj = h @ w_in
    q = proj[..., :qk_w].reshape(B, S, RET_HEADS, RET_DK).astype(jnp.float32)
    k = proj[..., qk_w:2 * qk_w].reshape(B, S, RET_HEADS, RET_DK).astype(jnp.float32)
    v = proj[..., 2 * qk_w:2 * qk_w + v_w].reshape(B, S, RET_HEADS, RET_DV).astype(jnp.float32)
    gate = proj[..., 2 * qk_w + v_w:].astype(jnp.float32)
    pos = jnp.arange(S, dtype=jnp.float32)
    q = rotary(q, pos)
    k = rotary(k, pos) * RET_DK ** -0.5
    q, k, v = (jnp.swapaxes(a, 1, 2) for a in (q, k, v))
    flip = lambda a: jnp.flip(a, axis=2)
    fwd = retention_chunkwise(q, k, v, retention_log_decay(0.0), False)
    bwd = flip(retention_chunkwise(flip(q), flip(k), flip(v), retention_log_decay(RET_BWD_DECAY_OFFSET), True))
    o = head_rmsnorm(jnp.swapaxes(fwd + bwd, 1, 2))
    o = o.reshape(B, S, v_w) * jax.nn.silu(gate)
    return o.astype(h.dtype) @ w_out


def t5_bucket(rel):
    nb = REL_BUCKETS // 2
    max_exact = nb // 2
    ret = jnp.where(rel > 0, nb, 0)
    n = jnp.abs(rel)
    nf = jnp.maximum(n, 1).astype(jnp.float32)
    large = max_exact + (jnp.log(nf / max_exact) / math.log(REL_MAX_DIST / max_exact)
                         * (nb - max_exact)).astype(jnp.int32)
    large = jnp.minimum(large, nb - 1)
    return ret + jnp.where(n < max_exact, n, large)


def dilated_group_attention(q, k, v, table, dilation, window):
    B, S, H, dh = q.shape
    n_half = window // (2 * dilation)
    Qb = n_half
    L = S // dilation
    nb = -(-L // Qb)
    Lp = nb * Qb

    def residues(a):
        return a.reshape(B, L, dilation, H, dh)

    qb = jnp.pad(residues(q), ((0, 0), (0, Lp - L), (0, 0), (0, 0), (0, 0))).reshape(B, nb, Qb, dilation, H, dh)

    def key_windows(a):
        ap = jnp.pad(residues(a), ((0, 0), (Qb, Lp - L + Qb), (0, 0), (0, 0), (0, 0)))
        ap = ap.reshape(B, nb + 2, Qb, dilation, H, dh)
        return jnp.concatenate([ap[:, :-2], ap[:, 1:-1], ap[:, 2:]], axis=2)

    kw = key_windows(k)
    vw = key_windows(v)
    s_idx = jnp.arange(Qb)[:, None]
    t_idx = jnp.arange(3 * Qb)[None, :]
    rel_steps = t_idx - s_idx - Qb
    bias = jnp.transpose(table[t5_bucket(rel_steps * dilation)], (2, 0, 1)).astype(jnp.float32)
    jq = jnp.arange(nb)[:, None, None] * Qb + s_idx[None]
    jk = jq + rel_steps[None]
    valid = (jnp.abs(rel_steps)[None] <= n_half) & (jk >= 0) & (jk < L)
    scores = jnp.einsum('bcqrhe,bckrhe->bcrhqk', qb, kw).astype(jnp.float32) * (dh ** -0.5) + bias
    scores = jnp.where(valid[None, :, None, None], scores, NEG_INF)
    m = jnp.max(scores, axis=-1, keepdims=True)
    p = jnp.exp(scores - m)
    den = jnp.sum(p, axis=-1, keepdims=True)
    o = jnp.einsum('bcrhqk,bckrhe->bcqrhe', p / den, vw.astype(jnp.float32))
    lse = jnp.transpose((m + jnp.log(den))[..., 0], (0, 1, 4, 2, 3))
    o = o.reshape(B, Lp, dilation, H, dh)[:, :L].reshape(B, S, H, dh)
    lse = lse.reshape(B, Lp, dilation, H)[:, :L].reshape(B, S, H)
    return o, lse


def dilated_mixer(h, w_in, w_out, rel_table):
    B, S, _ = h.shape
    proj = (h @ w_in).reshape(B, S, len(DIL_PATTERNS), 3, DIL_HEADS, DIL_DH)
    outs, lses = [], []
    for gi, (window, dil) in enumerate(DIL_PATTERNS):
        o, l = dilated_group_attention(proj[:, :, gi, 0], proj[:, :, gi, 1], proj[:, :, gi, 2],
                                       rel_table[:, gi * DIL_HEADS:(gi + 1) * DIL_HEADS], dil, window)
        outs.append(o)
        lses.append(l)
    wts = jax.nn.softmax(jnp.stack(lses), axis=0)
    o = jnp.sum(wts[..., None] * jnp.stack(outs), axis=0)
    return o.reshape(B, S, DIL_HEADS * DIL_DH).astype(h.dtype) @ w_out


def centred_depthwise_conv(x, w):
    pad = w.shape[0] // 2
    return lax.conv_general_dilated(x, w[:, None, :].astype(x.dtype), window_strides=(1,),
                                    padding=[(pad, pad)], dimension_numbers=('NWC', 'WIO', 'NWC'),
                                    feature_group_count=x.shape[-1])


def gated_delta_chunkwise(q, k, v, beta, g):
    B, H, S, dk = q.shape
    dv = v.shape[-1]
    C = DN_CHUNK
    nc = S // C
    q = q.reshape(B, H, nc, C, dk)
    k = k.reshape(B, H, nc, C, dk)
    v = v.reshape(B, H, nc, C, dv)
    beta = beta.reshape(B, H, nc, C)
    gc = jnp.cumsum(g.reshape(B, H, nc, C), axis=-1)
    tri = jnp.tril(jnp.ones((C, C), bool))
    strict = jnp.tril(jnp.ones((C, C), bool), -1)
    gam = jnp.exp(jnp.where(tri, gc[..., :, None] - gc[..., None, :], -jnp.inf))
    kb = k * beta[..., None]
    a = jnp.where(strict, jnp.einsum('bhnid,bhnjd->bhnij', kb, k) * gam, 0.0) + jnp.eye(C, dtype=jnp.float32)
    solve = lambda rhs: lax.linalg.triangular_solve(a, rhs, left_side=True, lower=True, unit_diagonal=True)
    u = solve(v * beta[..., None])
    w = solve(kb * jnp.exp(gc)[..., None])
    qk = jnp.einsum('bhnid,bhnjd->bhnij', q, k) * gam
    qg = q * jnp.exp(gc)[..., None]
    kd = k * jnp.exp(gc[..., -1:] - gc)[..., None]
    dl = jnp.exp(gc[..., -1])

    def step(state, xs):
        u_c, w_c, qk_c, qg_c, kd_c, dl_c = xs
        v_new = u_c - jnp.einsum('bhcd,bhde->bhce', w_c, state)
        o = jnp.einsum('bhcd,bhde->bhce', qg_c, state) + jnp.einsum('bhij,bhje->bhie', qk_c, v_new)
        state = state * dl_c[..., None, None] + jnp.einsum('bhcd,bhce->bhde', kd_c, v_new)
        return state, o

    xs = tuple(jnp.moveaxis(a_, 2, 0) for a_ in (u, w, qk, qg, kd, dl))
    s0 = jnp.zeros((B, H, dk, dv), jnp.float32)
    _, ys = lax.scan(step, s0, xs)
    return jnp.moveaxis(ys, 0, 2).reshape(B, H, S, dv)


def deltanet_mixer(h, w_in, conv_w, a_log, dt_bias, norm_w, w_out):
    B, S, _ = h.shape
    k_w = DN_K_HEADS * DN_DK
    v_w = DN_V_HEADS * DN_DV
    conv_dim = 2 * k_w + v_w
    proj = h @ w_in
    qkv = jax.nn.silu(centred_depthwise_conv(proj[..., :conv_dim], conv_w))
    z = proj[..., conv_dim:conv_dim + v_w].astype(jnp.float32).reshape(B, S, DN_V_HEADS, DN_DV)
    ba = proj[..., conv_dim + v_w:].astype(jnp.float32).reshape(B, S, 2, 2, DN_V_HEADS)
    beta = jax.nn.sigmoid(ba[:, :, 0])
    g = -jnp.exp(a_log.astype(jnp.float32)) * jax.nn.softplus(ba[:, :, 1] + dt_bias.astype(jnp.float32))
    rep = DN_V_HEADS // DN_K_HEADS
    q = jnp.repeat(qkv[..., :k_w].reshape(B, S, DN_K_HEADS, DN_DK), rep, axis=2).astype(jnp.float32)
    k = jnp.repeat(qkv[..., k_w:2 * k_w].reshape(B, S, DN_K_HEADS, DN_DK), rep, axis=2).astype(jnp.float32)
    v = qkv[..., 2 * k_w:].reshape(B, S, DN_V_HEADS, DN_DV).astype(jnp.float32)
    q = l2norm(q) * DN_DK ** -0.5
    k = l2norm(k)
    bhs = lambda a_: jnp.moveaxis(a_, 1, 2)
    q, k, v = bhs(q), bhs(k), bhs(v)
    flip = lambda a_: jnp.flip(a_, axis=2)
    fwd = gated_delta_chunkwise(q, k, v, bhs(beta[:, :, 0]), bhs(g[:, :, 0]))
    bwd = flip(gated_delta_chunkwise(flip(q), flip(k), flip(v), flip(bhs(beta[:, :, 1])), flip(bhs(g[:, :, 1]))))
    o = jnp.moveaxis(fwd + bwd, 1, 2)
    o = head_rmsnorm(o) * norm_w.astype(jnp.float32) * jax.nn.silu(z)
    return o.reshape(B, S, v_w).astype(h.dtype) @ w_out


def expert_choice_ffn(h, w_router, w_gate, w_up, w_down):
    B, S, D = h.shape
    cap = CAPACITY_FACTOR * S // N_EXPERTS
    aff = jax.nn.softmax(jnp.einsum('bsd,de->bse', h, w_router).astype(jnp.float32), axis=-1)
    gates, idx = lax.top_k(jnp.swapaxes(aff, 1, 2), cap)
    xin = jax.vmap(lambda hb, ib: hb[ib])(h, idx)
    hid = jax.nn.silu(jnp.einsum('becd,edf->becf', xin, w_gate)) * jnp.einsum('becd,edf->becf', xin, w_up)
    y = jnp.einsum('becf,efd->becd', hid, w_down) * gates[..., None].astype(h.dtype)
    return jax.vmap(lambda ib, yb: jnp.zeros((S, D), yb.dtype).at[ib.reshape(-1)].add(yb.reshape(-1, D)))(idx, y)


def setup_inputs(seed: int = 0) -> dict:
    key = jax.random.key(seed)
    ks = jax.random.split(key, 20)
    f32 = jnp.float32
    n_a = len(range(0, DEPTH, N_MIXERS))
    n_b = len(range(1, DEPTH, N_MIXERS))
    n_c = len(range(2, DEPTH, N_MIXERS))

    def dense(k, shape, fan_in):
        return jax.random.normal(k, shape, f32) * (fan_in ** -0.5)

    ret_in = 2 * RET_HEADS * RET_DK + 2 * RET_HEADS * RET_DV
    dil_in = len(DIL_PATTERNS) * 3 * DIL_HEADS * DIL_DH
    dn_conv_dim = 2 * DN_K_HEADS * DN_DK + DN_V_HEADS * DN_DV
    dn_in = dn_conv_dim + DN_V_HEADS * DN_DV + 4 * DN_V_HEADS
    dt = jnp.exp(jax.random.uniform(ks[11], (n_c, 2, DN_V_HEADS), f32, math.log(1e-3), math.log(1e-1)))
    return {
        "x": jax.random.normal(ks[0], (BATCH, SEQ, D_MODEL), f32),
        "norm_mix_w": 1.0 + 0.02 * jax.random.normal(ks[1], (DEPTH, D_MODEL), f32),
        "norm_ffn_w": 1.0 + 0.02 * jax.random.normal(ks[2], (DEPTH, D_MODEL), f32),
        "final_norm_w": 1.0 + 0.02 * jax.random.normal(ks[3], (D_MODEL,), f32),
        "rel_bias_table": 0.2 * jax.random.normal(ks[4], (REL_BUCKETS, len(DIL_PATTERNS) * DIL_HEADS), f32),
        "ret_w_in": dense(ks[5], (n_a, D_MODEL, ret_in), D_MODEL),
        "ret_w_out": dense(ks[6], (n_a, RET_HEADS * RET_DV, D_MODEL), RET_HEADS * RET_DV),
        "dil_w_in": dense(ks[7], (n_b, D_MODEL, dil_in), D_MODEL),
        "dil_w_out": dense(ks[8], (n_b, DIL_HEADS * DIL_DH, D_MODEL), DIL_HEADS * DIL_DH),
        "dn_w_in": dense(ks[9], (n_c, D_MODEL, dn_in), D_MODEL),
        "dn_conv_w": dense(ks[10], (n_c, DN_CONV, dn_conv_dim), DN_CONV),
        "dn_a_log": jnp.log(jax.random.uniform(ks[12], (n_c, 2, DN_V_HEADS), f32, 1.0, 16.0)),
        "dn_dt_bias": dt + jnp.log(-jnp.expm1(-dt)),
        "dn_norm_w": 1.0 + 0.02 * jax.random.normal(ks[13], (n_c, DN_DV), f32),
        "dn_w_out": dense(ks[14], (n_c, DN_V_HEADS * DN_DV, D_MODEL), DN_V_HEADS * DN_DV),
        "moe_w_router": dense(ks[15], (DEPTH, D_MODEL, N_EXPERTS), D_MODEL),
        "moe_w_gate": dense(ks[16], (DEPTH, N_EXPERTS, D_MODEL, EXPERT_FF), D_MODEL),
        "moe_w_up": dense(ks[17], (DEPTH, N_EXPERTS, D_MODEL, EXPERT_FF), D_MODEL),
        "moe_w_down": dense(ks[18], (DEPTH, N_EXPERTS, EXPERT_FF, D_MODEL), EXPERT_FF),
    }


def reference(x, norm_mix_w, norm_ffn_w, final_norm_w, rel_bias_table, ret_w_in, ret_w_out,
              dil_w_in, dil_w_out, dn_w_in, dn_conv_w, dn_a_log, dn_dt_bias, dn_norm_w, dn_w_out,
              moe_w_router, moe_w_gate, moe_w_up, moe_w_down):
    for i in range(DEPTH):
        j = i // N_MIXERS
        kind = i % N_MIXERS
        h = rmsnorm(x, norm_mix_w[i])
        if kind == 0:
            mix = retention_mixer(h, ret_w_in[j], ret_w_out[j])
        elif kind == 1:
            mix = dilated_mixer(h, dil_w_in[j], dil_w_out[j], rel_bias_table)
        else:
            mix = deltanet_mixer(h, dn_w_in[j], dn_conv_w[j], dn_a_log[j], dn_dt_bias[j], dn_norm_w[j], dn_w_out[j])
        x = x + mix
        h = rmsnorm(x, norm_ffn_w[i])
        x = x + expert_choice_ffn(h, moe_w_router[i], moe_w_gate[i], moe_w_up[i], moe_w_down[i])
    return rmsnorm(x, final_norm_w)
```

```python
import functools
import math

import jax
import jax.numpy as jnp
from jax import lax
from jax.experimental import pallas as pl
from jax.experimental.pallas import tpu as pltpu

F32 = jnp.float32
BF16 = jnp.bfloat16

D_MODEL = 2048
EPS = 1e-6
NEG_INF = -1e30
RET_HEADS = 8
RET_DK = D_MODEL // RET_HEADS
RET_DV = 2 * RET_DK
ROPE_BASE = 10000.0
RET_BWD_DECAY_OFFSET = 0.5
DIL_PATTERNS = ((128, 1), (512, 4), (2048, 16))
DIL_HEADS = 16
DIL_DH = D_MODEL // DIL_HEADS
REL_BUCKETS = 32
REL_MAX_DIST = 1024
DN_K_HEADS = 16
DN_V_HEADS = 32
DN_DK = 128
DN_DV = 128
DN_CONV = 5
N_EXPERTS = 16
EXPERT_FF = D_MODEL // 2
CAPACITY_FACTOR = 2
N_MIXERS = 3

LANES = 128
VMEM_LIMIT_BYTES = 56 * 1024 * 1024
ROW_TILE = 1024
RET_CHUNK = 256
MOE_FF_TILE = 512


def _cparams(*sem):
    return pltpu.CompilerParams(dimension_semantics=sem, vmem_limit_bytes=VMEM_LIMIT_BYTES)


def _dot(a, b):
    return jnp.dot(a, b, preferred_element_type=F32)


def _dot_nt(a, b):
    return lax.dot_general(a, b, (((1,), (1,)), ((), ())), preferred_element_type=F32)


def _dot_tn(a, b):
    return lax.dot_general(a, b, (((0,), (0,)), ((), ())), preferred_element_type=F32)


def _silu(x):
    return x / (1.0 + jnp.exp(-x))


def _rms_rows(x):
    return x * lax.rsqrt(jnp.mean(x * x, axis=-1, keepdims=True) + EPS)


def _norm_proj_kernel(x_ref, nw_ref, w_ref, o_ref, hn_ref):
    @pl.when(pl.program_id(1) == 0)
    def _():
        hn_ref[...] = (_rms_rows(x_ref[...]) * nw_ref[...]).astype(BF16)

    o_ref[...] = _dot(hn_ref[...], w_ref[...].astype(BF16)).astype(o_ref.dtype)


def _norm_proj_rope_kernel(x_ref, nw_ref, w_ref, cos_ref, sin_ref, o_ref, hn_ref, *, n_q_tiles, n_rope_tiles, k_scale):
    j = pl.program_id(1)

    @pl.when(j == 0)
    def _():
        hn_ref[...] = (_rms_rows(x_ref[...]) * nw_ref[...]).astype(BF16)

    acc = _dot(hn_ref[...], w_ref[...].astype(BF16))

    @pl.when(j < n_rope_tiles)
    def _():
        c = cos_ref[...]
        s = sin_ref[...]
        half = c.shape[1]
        scale = jnp.where(j >= n_q_tiles, k_scale, 1.0).astype(F32)
        for hh in range(acc.shape[1] // (2 * half)):
            x1 = acc[:, 2 * hh * half:(2 * hh + 1) * half]
            x2 = acc[:, (2 * hh + 1) * half:(2 * hh + 2) * half]
            o_ref[:, 2 * hh * half:(2 * hh + 1) * half] = ((x1 * c - x2 * s) * scale).astype(o_ref.dtype)
            o_ref[:, (2 * hh + 1) * half:(2 * hh + 2) * half] = ((x2 * c + x1 * s) * scale).astype(o_ref.dtype)

    @pl.when(j >= n_rope_tiles)
    def _():
        o_ref[...] = acc.astype(o_ref.dtype)


def _norm_proj(x2, nw, w, *, n_out, tn, out_dtype, col_off=0, rope=None):
    m, k = x2.shape
    w_stack, layer = w
    tm = min(ROW_TILE, m)
    grid = (m // tm, n_out // tn)
    off = col_off // tn
    in_specs = [pl.BlockSpec((tm, k), lambda i, j: (i, 0)),
                pl.BlockSpec((1, k), lambda i, j: (0, 0)),
                pl.BlockSpec((None, k, tn), lambda i, j: (layer, 0, j + off))]
    args = [x2, nw.reshape(1, k), w_stack]
    if rope is None:
        body = _norm_proj_kernel
    else:
        cos, sin, seq, n_q_cols, n_rope_cols, k_scale = rope
        nsb = seq // tm
        in_specs += [pl.BlockSpec((tm, cos.shape[1]), lambda i, j: (i % nsb, 0)),
                     pl.BlockSpec((tm, cos.shape[1]), lambda i, j: (i % nsb, 0))]
        args += [cos, sin]
        body = functools.partial(_norm_proj_rope_kernel, n_q_tiles=n_q_cols // tn, n_rope_tiles=n_rope_cols // tn,
                                 k_scale=k_scale)
    return pl.pallas_call(
        body, grid=grid, in_specs=in_specs,
        out_specs=pl.BlockSpec((tm, tn), lambda i, j: (i, j)),
        out_shape=jax.ShapeDtypeStruct((m, n_out), out_dtype),
        scratch_shapes=[pltpu.VMEM((tm, k), BF16)],
        compiler_params=_cparams("parallel", "arbitrary"),
    )(*args)


def _out_proj_kernel(a_ref, w_ref, res_ref, o_ref):
    o_ref[...] = res_ref[...] + _dot(a_ref[...], w_ref[...].astype(BF16))


def _out_proj(a, w, res, *, tn=512):
    m, k = a.shape
    w_stack, layer = w
    n = w_stack.shape[2]
    tm = min(ROW_TILE, m)
    return pl.pallas_call(
        _out_proj_kernel, grid=(m // tm, n // tn),
        in_specs=[pl.BlockSpec((tm, k), lambda i, j: (i, 0)),
                  pl.BlockSpec((None, k, tn), lambda i, j: (layer, 0, j)),
                  pl.BlockSpec((tm, tn), lambda i, j: (i, j))],
        out_specs=pl.BlockSpec((tm, tn), lambda i, j: (i, j)),
        out_shape=jax.ShapeDtypeStruct((m, n), F32),
        compiler_params=_cparams("parallel", "arbitrary"),
    )(a, w_stack, res)


def _final_norm_kernel(x_ref, nw_ref, o_ref):
    o_ref[...] = _rms_rows(x_ref[...]) * nw_ref[...]


def _final_norm(x2, nw):
    m, k = x2.shape
    tm = min(ROW_TILE, m)
    return pl.pallas_call(
        _final_norm_kernel, grid=(m // tm,),
        in_specs=[pl.BlockSpec((tm, k), lambda i: (i, 0)), pl.BlockSpec((1, k), lambda i: (0, 0))],
        out_specs=pl.BlockSpec((tm, k), lambda i: (i, 0)),
        out_shape=jax.ShapeDtypeStruct((m, k), F32),
        compiler_params=_cparams("parallel"),
    )(x2, nw.reshape(1, k))


def _retention_kernel(q_ref, k_ref, v_ref, g_ref, dmat_ref, vec_ref, o_ref, state_ref, oacc_ref):
    sweep = pl.program_id(2)
    c = pl.program_id(3)
    nc = pl.num_programs(3)

    @pl.when(c == 0)
    def _():
        state_ref[...] = jnp.zeros_like(state_ref)

    q = q_ref[...]
    k = k_ref[...]
    v = v_ref[...]
    vec = vec_ref[0]

    @pl.when(sweep == 0)
    def _():
        p = (_dot_nt(q, k) * dmat_ref[0]).astype(BF16)
        inter = _dot(q, state_ref[...].astype(BF16)) * vec[:, 0:1]
        oacc_ref[c] = _dot(p, v) + inter
        kd = (k.astype(F32) * vec[:, 2:3]).astype(BF16)
        state_ref[...] = state_ref[...] * vec[0:1, 4:5] + _dot_tn(kd, v)

    @pl.when(sweep == 1)
    def _():
        inter = _dot(q, state_ref[...].astype(BF16)) * vec[:, 1:2]
        o = _rms_rows(oacc_ref[nc - 1 - c] + inter)
        o_ref[...] = (o * _silu(g_ref[...].astype(F32))).astype(o_ref.dtype)
        kd = (k.astype(F32) * vec[:, 3:4]).astype(BF16)
        state_ref[...] = state_ref[...] * vec[0:1, 5:6] + _dot_tn(kd, v)


def _retention_tables(chunk):
    hh = jnp.arange(RET_HEADS, dtype=F32)
    lg_f = jnp.log1p(-jnp.exp2(-5.0 - hh))
    lg_b = jnp.log1p(-jnp.exp2(-(5.0 + RET_BWD_DECAY_OFFSET) - hh))
    t = jnp.arange(chunk, dtype=F32)
    diff = t[:, None] - t[None, :]
    dmat = jnp.where(diff[None] >= 0,
                     jnp.exp(jnp.maximum(diff, 0.0)[None] * lg_f[:, None, None]),
                     jnp.exp(jnp.maximum(-diff, 0.0)[None] * lg_b[:, None, None]))
    cols = [jnp.exp((t[None, :] + 1.0) * lg_f[:, None]),
            jnp.exp((chunk - t)[None, :] * lg_b[:, None]),
            jnp.exp((chunk - 1.0 - t)[None, :] * lg_f[:, None]),
            jnp.exp(t[None, :] * lg_b[:, None]),
            jnp.broadcast_to(jnp.exp(chunk * lg_f)[:, None], (RET_HEADS, chunk)),
            jnp.broadcast_to(jnp.exp(chunk * lg_b)[:, None], (RET_HEADS, chunk))]
    cols += [jnp.zeros((RET_HEADS, chunk), F32)] * 2
    return dmat, jnp.stack(cols, axis=-1)


def _retention_core(proj, batch, seq):
    cq = min(RET_CHUNK, seq)
    nc = seq // cq
    dmat, vec = _retention_tables(cq)
    kq = RET_HEADS
    kv = 2 * RET_HEADS * RET_DK // RET_DV
    kg = kv + RET_HEADS

    def row(b, s, c):
        return b * nc + c + s * (nc - 1 - 2 * c)

    def row_out(b, s, c):
        return b * nc + nc - 1 - c * s

    return pl.pallas_call(
        _retention_kernel, grid=(batch, RET_HEADS, 2, nc),
        in_specs=[pl.BlockSpec((cq, RET_DK), lambda b, h, s, c: (row(b, s, c), h)),
                  pl.BlockSpec((cq, RET_DK), lambda b, h, s, c: (row(b, s, c), kq + h)),
                  pl.BlockSpec((cq, RET_DV), lambda b, h, s, c: (row(b, s, c), kv + h)),
                  pl.BlockSpec((cq, RET_DV), lambda b, h, s, c: (row_out(b, s, c), kg + h)),
                  pl.BlockSpec((1, cq, cq), lambda b, h, s, c: (h, 0, 0)),
                  pl.BlockSpec((1, cq, 8), lambda b, h, s, c: (h, 0, 0))],
        out_specs=pl.BlockSpec((cq, RET_DV), lambda b, h, s, c: (row_out(b, s, c), h)),
        out_shape=jax.ShapeDtypeStruct((batch * seq, RET_HEADS * RET_DV), BF16),
        scratch_shapes=[pltpu.VMEM((RET_DK, RET_DV), F32), pltpu.VMEM((nc, cq, RET_DV), F32)],
        compiler_params=_cparams("parallel", "parallel", "arbitrary", "arbitrary"),
    )(proj, proj, proj, proj, dmat, vec)


def _rope_tables(seq, half):
    inv = ROPE_BASE ** (-jnp.arange(half, dtype=F32) / half)
    ang = jnp.arange(seq, dtype=F32)[:, None] * inv[None, :]
    return jnp.cos(ang), jnp.sin(ang)


def _retention_mixer(x2, nw, w_in, w_out, batch, seq):
    cos, sin = _rope_tables(seq, RET_DK // 2)
    qk_cols = 2 * RET_HEADS * RET_DK
    proj = _norm_proj(x2, nw, w_in, n_out=w_in[0].shape[2], tn=512, out_dtype=BF16,
                      rope=(cos, sin, seq, qk_cols // 2, qk_cols, RET_DK ** -0.5))
    o = _retention_core(proj, batch, seq)
    return _out_proj(o, w_out, x2)


def _split_bf16(x):
    hi = x.astype(BF16)
    return hi, (x - hi.astype(F32)).astype(BF16)


def _router_kernel(x_ref, nw_ref, wr_ref, wrt_ref, hx_ref, afft_ref):
    d = x_ref.shape[1]
    h = _rms_rows(x_ref[...]) * nw_ref[...]
    hx_ref[:, :d] = h
    h_hi, h_lo = _split_bf16(h)
    w_hi, w_lo = _split_bf16(wr_ref[...])
    lg = _dot(h_hi, w_hi) + (_dot(h_lo, w_hi) + _dot(h_hi, w_lo))
    lane = lax.broadcasted_iota(jnp.int32, lg.shape, 1)
    lg = jnp.where(lane < N_EXPERTS, lg, NEG_INF)
    e = jnp.exp(lg - jnp.max(lg, axis=1, keepdims=True))
    hx_ref[:, d:] = e / jnp.sum(e, axis=1, keepdims=True)
    wt_hi, wt_lo = _split_bf16(wrt_ref[...])
    lgt = _dot_nt(wt_hi, h_hi) + (_dot_nt(wt_hi, h_lo) + _dot_nt(wt_lo, h_hi))
    et = jnp.exp(lgt - jnp.max(lgt, axis=0, keepdims=True))
    afft_ref[0] = et / jnp.sum(et, axis=0, keepdims=True)


def _router(x2, nw, w_router, batch, seq):
    m, d = x2.shape
    tm = min(512, seq)
    nsb = seq // tm
    wr = jnp.pad(w_router, ((0, 0), (0, LANES - N_EXPERTS)))
    return pl.pallas_call(
        _router_kernel, grid=(m // tm,),
        in_specs=[pl.BlockSpec((tm, d), lambda i: (i, 0)),
                  pl.BlockSpec((1, d), lambda i: (0, 0)),
                  pl.BlockSpec((d, LANES), lambda i: (0, 0)),
                  pl.BlockSpec((N_EXPERTS, d), lambda i: (0, 0))],
        out_specs=[pl.BlockSpec((tm, d + LANES), lambda i: (i, 0)),
                   pl.BlockSpec((1, N_EXPERTS, tm), lambda i: (i // nsb, 0, i % nsb))],
        out_shape=[jax.ShapeDtypeStruct((m, d + LANES), F32),
                   jax.ShapeDtypeStruct((batch, N_EXPERTS, seq), F32)],
        compiler_params=_cparams("parallel"),
    )(x2, nw.reshape(1, d), wr, w_router.T)


def _topk_kernel(aff_ref, idx_ref, loc_ref, off_ref, end_ref, *, cap, n_groups, n_blk):
    n = n_groups * n_blk
    bits = lax.bitcast_convert_type(aff_ref[...], jnp.int32)
    ri = lax.broadcasted_iota(jnp.int32, (n, n), 0)
    ci = lax.broadcasted_iota(jnp.int32, (n, n), 1)
    same = (ri // n_blk) == (ci // n_blk)
    grp_ones = same.astype(BF16)
    grp_before = (same & (ci < ri)).astype(BF16)
    li = lax.broadcasted_iota(jnp.int32, (LANES, LANES), 0)
    lj = lax.broadcasted_iota(jnp.int32, (LANES, LANES), 1)
    incl = (li <= lj).astype(BF16)
    ones = jnp.ones((LANES, LANES), BF16)

    def row_total(mask):
        return _dot(mask.astype(BF16), ones)

    def group_count(mask):
        return _dot(grp_ones, row_total(mask).astype(BF16))

    def search(i, tau):
        cand = tau | jnp.left_shift(jnp.int32(1), 30 - i)
        return jnp.where(group_count(bits >= cand) >= cap, cand, tau)

    tau = lax.fori_loop(0, 31, search, jnp.zeros((n, LANES), jnp.int32))

    def cumsum(mask):
        mb = mask.astype(BF16)
        tot = _dot(mb, ones)
        return _dot(mb, incl), _dot(grp_before, tot.astype(BF16)), tot

    gt = bits > tau
    eq = bits == tau
    need = cap - group_count(gt)
    eq_loc, eq_off, _ = cumsum(eq)
    sel = gt | (eq & (eq_loc + eq_off <= need))
    loc, off, tot = cumsum(sel)
    loc_ref[...] = loc
    off_ref[...] = off
    end_ref[...] = off + tot

    slot = lax.broadcasted_iota(jnp.int32, (cap, LANES), 0).astype(F32)
    lane = lax.broadcasted_iota(jnp.int32, (cap, LANES), 1)
    eye = lax.broadcasted_iota(jnp.int32, (n_blk, LANES), 0) == lax.broadcasted_iota(jnp.int32, (n_blk, LANES), 1)
    pad = jnp.zeros((LANES - n_blk, LANES), BF16)

    def compact(g, acc):
        rows = pl.ds(pl.multiple_of(g * n_blk, n_blk), n_blk)
        end_row = jnp.sum(jnp.where(eye, end_ref[rows, :], 0.0), axis=0, keepdims=True)
        off_row = jnp.sum(jnp.where(eye, off_ref[rows, :], 0.0), axis=0, keepdims=True)
        blk = jnp.sum(((end_row <= slot) & (lane < n_blk)).astype(F32), axis=1, keepdims=True)
        onehot = lane == blk.astype(jnp.int32)
        loc_pad = jnp.concatenate([loc_ref[rows, :].astype(BF16), pad], axis=0)
        in_blk = _dot(onehot.astype(BF16), loc_pad)
        rank = slot - jnp.sum(jnp.where(onehot, off_row, 0.0), axis=1, keepdims=True)
        pos = blk * LANES + jnp.sum((in_blk <= rank).astype(F32), axis=1, keepdims=True)
        return jnp.where(lane == g, pos.astype(jnp.int32), acc)

    idx_ref[...] = lax.fori_loop(0, n_groups, compact, jnp.zeros((cap, LANES), jnp.int32))


def _topk(afft, cap):
    batch, n_e, seq = afft.shape
    n_groups = batch * n_e
    n_blk = seq // LANES
    n = n_groups * n_blk
    assert n_groups <= LANES and n_blk <= LANES
    idx = pl.pallas_call(
        functools.partial(_topk_kernel, cap=cap, n_groups=n_groups, n_blk=n_blk),
        out_shape=jax.ShapeDtypeStruct((cap, LANES), jnp.int32),
        scratch_shapes=[pltpu.VMEM((n, LANES), F32)] * 3,
        compiler_params=pltpu.CompilerParams(vmem_limit_bytes=VMEM_LIMIT_BYTES),
    )(afft.reshape(n, LANES))
    return idx[:, :n_groups].T.reshape(batch, n_e, cap)


def _moe_ffn_kernel(idx_ref, hx_hbm, wg_ref, wu_ref, wd_ref, xres_hbm, out_hbm, xbuf, xbf, acc, gate, sem):
    del xres_hbm
    e = pl.program_id(0)
    f = pl.program_id(1)
    nf = pl.num_programs(1)
    rows = xbuf.shape[0]
    d = xbf.shape[1]
    base = e * rows

    def gather(src, width):
        def start(r, carry):
            pltpu.make_async_copy(src.at[pl.ds(idx_ref[base + r], 1), pl.ds(0, width)],
                                  xbuf.at[pl.ds(r, 1), pl.ds(0, width)], sem.at[0]).start()
            return carry

        def wait(r, carry):
            pltpu.make_async_copy(src.at[pl.ds(0, 1), pl.ds(0, width)],
                                  xbuf.at[pl.ds(r, 1), pl.ds(0, width)], sem.at[0]).wait()
            return carry

        lax.fori_loop(0, rows, start, 0)
        lax.fori_loop(0, rows, wait, 0)

    @pl.when(f == 0)
    def _():
        gather(hx_hbm, d + LANES)
        xbf[...] = xbuf[:, :d].astype(BF16)
        lane = lax.broadcasted_iota(jnp.int32, (rows, LANES), 1)
        gate[...] = jnp.sum(jnp.where(lane == e, xbuf[:, d:], 0.0), axis=1, keepdims=True)

    x = xbf[...]
    hid = (_silu(_dot(x, wg_ref[...].astype(BF16))) * _dot(x, wu_ref[...].astype(BF16))).astype(BF16)
    part = _dot(hid, wd_ref[...].astype(BF16))

    @pl.when(f == 0)
    def _():
        acc[...] = part

    @pl.when(f > 0)
    def _():
        acc[...] += part

    @pl.when(f == nf - 1)
    def _():
        gather(out_hbm, d)
        xbuf[:, :d] = xbuf[:, :d] + acc[...] * gate[...]

        def start(r, carry):
            pltpu.make_async_copy(xbuf.at[pl.ds(r, 1), pl.ds(0, d)],
                                  out_hbm.at[pl.ds(idx_ref[base + r], 1), pl.ds(0, d)], sem.at[0]).start()
            return carry

        def wait(r, carry):
            pltpu.make_async_copy(xbuf.at[pl.ds(r, 1), pl.ds(0, d)],
                                  out_hbm.at[pl.ds(0, 1), pl.ds(0, d)], sem.at[0]).wait()
            return carry

        lax.fori_loop(0, rows, start, 0)
        lax.fori_loop(0, rows, wait, 0)


def _moe_ffn(rows_idx, hx, x2, w_gate, w_up, w_down, layer):
    m, d = x2.shape
    _, n_e, _, ff = w_gate.shape
    rows = rows_idx.shape[0] // n_e
    tf = min(MOE_FF_TILE, ff)
    grid_spec = pltpu.PrefetchScalarGridSpec(
        num_scalar_prefetch=1, grid=(n_e, ff // tf),
        in_specs=[pl.BlockSpec(memory_space=pl.ANY),
                  pl.BlockSpec((None, None, d, tf), lambda e, f, idx: (layer, e, 0, f)),
                  pl.BlockSpec((None, None, d, tf), lambda e, f, idx: (layer, e, 0, f)),
                  pl.BlockSpec((None, None, tf, d), lambda e, f, idx: (layer, e, f, 0)),
                  pl.BlockSpec(memory_space=pl.ANY)],
        out_specs=pl.BlockSpec(memory_space=pl.ANY),
        scratch_shapes=[pltpu.VMEM((rows, d + LANES), F32), pltpu.VMEM((rows, d), BF16),
                        pltpu.VMEM((rows, d), F32), pltpu.VMEM((rows, 1), F32),
                        pltpu.SemaphoreType.DMA((1,))])
    return pl.pallas_call(
        _moe_ffn_kernel, grid_spec=grid_spec,
        out_shape=jax.ShapeDtypeStruct((m, d), F32),
        input_output_aliases={5: 0},
        compiler_params=_cparams("arbitrary", "arbitrary"),
    )(rows_idx, hx, w_gate, w_up, w_down, x2)


def _moe(x2, nw, w_router, w_gate, w_up, w_down, layer, batch, seq):
    cap = CAPACITY_FACTOR * seq // N_EXPERTS
    hx, afft = _router(x2, nw, w_router, batch, seq)
    idx = _topk(afft, cap)
    rows_idx = idx + (jnp.arange(batch, dtype=jnp.int32) * seq)[:, None, None]
    rows_idx = jnp.transpose(rows_idx, (1, 0, 2)).reshape(-1)
    return _moe_ffn(rows_idx, hx, x2, w_gate, w_up, w_down, layer)


DIL_BLOCK = 64


def _t5_bucket(rel):
    nb = REL_BUCKETS // 2
    max_exact = nb // 2
    ret = jnp.where(rel > 0, nb, 0)
    n = jnp.abs(rel)
    nf = jnp.maximum(n, 1).astype(F32)
    large = max_exact + (jnp.log(nf / max_exact) / math.log(REL_MAX_DIST / max_exact)
                         * (nb - max_exact)).astype(jnp.int32)
    large = jnp.minimum(large, nb - 1)
    return ret + jnp.where(n < max_exact, n, large)


def _dil_bias_kernel(table_ref, bucket_ref, o_ref):
    col = pl.program_id(0)
    bkt = bucket_ref[0]
    acc = jnp.zeros(bkt.shape, F32)
    for b in range(REL_BUCKETS):
        acc = jnp.where(bkt == b, table_ref[b, col], acc)
    s = lax.broadcasted_iota(jnp.int32, bkt.shape, 0)
    t = lax.broadcasted_iota(jnp.int32, bkt.shape, 1)
    o_ref[0] = jnp.where((t >= s) & (t <= s + 2 * DIL_BLOCK), acc, NEG_INF)


def _dil_bias(rel_table):
    qb = DIL_BLOCK
    rel_steps = jnp.arange(3 * qb)[None, :] - jnp.arange(qb)[:, None] - qb
    buckets = jnp.stack([_t5_bucket(rel_steps * dil) for _, dil in DIL_PATTERNS]).astype(jnp.int32)
    n_col = rel_table.shape[1]
    return pl.pallas_call(
        _dil_bias_kernel, grid=(n_col,),
        in_specs=[pl.BlockSpec(memory_space=pltpu.SMEM),
                  pl.BlockSpec((1, qb, 3 * qb), lambda c: (c // DIL_HEADS, 0, 0))],
        out_specs=pl.BlockSpec((1, qb, 3 * qb), lambda c: (c, 0, 0)),
        out_shape=jax.ShapeDtypeStruct((n_col, qb, 3 * qb), F32),
        compiler_params=_cparams("parallel"),
    )(rel_table, buckets)


def _dil_attn_kernel(q_ref, kp_ref, kc_ref, kn_ref, vp_ref, vc_ref, vn_ref, bias_ref, o_ref, lse_ref):
    jb = pl.program_id(2)
    nb = pl.num_programs(2)
    qb = q_ref.shape[0]
    t = lax.broadcasted_iota(jnp.int32, (qb, 3 * qb), 1)
    in_seq = ((jb > 0) | (t >= qb)) & ((jb < nb - 1) | (t < 2 * qb))
    kw = jnp.concatenate([kp_ref[...], kc_ref[...], kn_ref[...]], axis=0)
    vw = jnp.concatenate([vp_ref[...], vc_ref[...], vn_ref[...]], axis=0)
    lane = lax.broadcasted_iota(jnp.int32, (qb, LANES), 1)
    lse = jnp.zeros((qb, LANES), F32)
    for h in range(DIL_HEADS):
        hs = slice(h * DIL_DH, (h + 1) * DIL_DH)
        s = _dot_nt(q_ref[:, hs], kw[:, hs]) * (DIL_DH ** -0.5) + bias_ref[h]
        s = jnp.where(in_seq, s, NEG_INF)
        m = jnp.max(s, axis=1, keepdims=True)
        p = jnp.exp(s - m)
        den = jnp.sum(p, axis=1, keepdims=True)
        o_ref[:, hs] = (_dot(p.astype(BF16), vw[:, hs]) / den).astype(o_ref.dtype)
        lse = jnp.where(lane == h, m + jnp.log(den), lse)
    lse_ref[0, 0, 0] = lse[:, :DIL_HEADS]


def _dil_group(proj, bias, gi, dil, batch, seq):
    qb = DIL_BLOCK
    l = seq // dil
    nb = l // qb
    hw = DIL_HEADS * DIL_DH
    n_grp = proj.shape[1] // hw
    p2 = proj.reshape(batch * l, dil * proj.shape[1])
    cq = 3 * gi

    def spec(which, shift):
        def imap(b, r, j):
            jj = jnp.clip(j + shift, 0, nb - 1)
            return (b * nb + jj, r * n_grp + cq + which)
        return pl.BlockSpec((qb, hw), imap)

    o, lse = pl.pallas_call(
        _dil_attn_kernel, grid=(batch, dil, nb),
        in_specs=[spec(0, 0), spec(1, -1), spec(1, 0), spec(1, 1), spec(2, -1), spec(2, 0), spec(2, 1),
                  pl.BlockSpec((DIL_HEADS, qb, 3 * qb), lambda b, r, j: (gi, 0, 0))],
        out_specs=[pl.BlockSpec((qb, hw), lambda b, r, j: (b * nb + j, r)),
                   pl.BlockSpec((1, 1, 1, qb, DIL_HEADS), lambda b, r, j: (b, j, r, 0, 0))],
        out_shape=[jax.ShapeDtypeStruct((batch * l, dil * hw), BF16),
                   jax.ShapeDtypeStruct((batch, nb, dil, qb, DIL_HEADS), F32)],
        compiler_params=_cparams("parallel", "parallel", "arbitrary"),
    )(p2, p2, p2, p2, p2, p2, p2, bias)
    lse = jnp.transpose(lse, (0, 1, 3, 2, 4)).reshape(batch * seq, DIL_HEADS)
    return o.reshape(batch * seq, hw), lse


def _dil_out_kernel(o0_ref, o1_ref, o2_ref, l0_ref, l1_ref, l2_ref, w_ref, res_ref, out_ref, comb_ref):
    @pl.when(pl.program_id(1) == 0)
    def _():
        l0, l1, l2 = l0_ref[...], l1_ref[...], l2_ref[...]
        m = jnp.maximum(jnp.maximum(l0, l1), l2)
        e0, e1, e2 = jnp.exp(l0 - m), jnp.exp(l1 - m), jnp.exp(l2 - m)
        den = e0 + e1 + e2
        w0, w1, w2 = e0 / den, e1 / den, e2 / den
        for h in range(DIL_HEADS):
            hs = slice(h * DIL_DH, (h + 1) * DIL_DH)
            comb_ref[:, hs] = (w0[:, h:h + 1] * o0_ref[:, hs].astype(F32) + w1[:, h:h + 1] * o1_ref[:, hs].astype(F32)
                               + w2[:, h:h + 1] * o2_ref[:, hs].astype(F32)).astype(BF16)

    out_ref[...] = res_ref[...] + _dot(comb_ref[...], w_ref[...].astype(BF16))


def _dil_out(os_, lses, w, res, *, tn=512):
    m, k = os_[0].shape
    w_stack, layer = w
    n = w_stack.shape[2]
    tm = min(ROW_TILE, m)
    ospec = pl.BlockSpec((tm, k), lambda i, j: (i, 0))
    lspec = pl.BlockSpec((tm, DIL_HEADS), lambda i, j: (i, 0))
    return pl.pallas_call(
        _dil_out_kernel, grid=(m // tm, n // tn),
        in_specs=[ospec, ospec, ospec, lspec, lspec, lspec,
                  pl.BlockSpec((None, k, tn), lambda i, j: (layer, 0, j)),
                  pl.BlockSpec((tm, tn), lambda i, j: (i, j))],
        out_specs=pl.BlockSpec((tm, tn), lambda i, j: (i, j)),
        out_shape=jax.ShapeDtypeStruct((m, n), F32),
        scratch_shapes=[pltpu.VMEM((tm, k), BF16)],
        compiler_params=_cparams("parallel", "arbitrary"),
    )(*os_, *lses, w_stack, res)


def _dilated_mixer(x2, nw, w_in, w_out, rel_table, batch, seq):
    proj = _norm_proj(x2, nw, w_in, n_out=w_in[0].shape[2], tn=512, out_dtype=BF16)
    bias = _dil_bias(rel_table)
    outs = [_dil_group(proj, bias, gi, dil, batch, seq) for gi, (_, dil) in enumerate(DIL_PATTERNS)]
    return _dil_out([o for o, _ in outs], [l for _, l in outs], w_out, x2)


DN_CHUNK = 64
DN_REP = DN_V_HEADS // DN_K_HEADS
DN_INST = 2 * DN_REP
DN_ROWS = DN_INST * DN_CHUNK
CONV_TILE = 256
CONV_HALO = 8


def _split3_bf16(x):
    hi = x.astype(BF16)
    r = x - hi.astype(F32)
    mid = r.astype(BF16)
    return hi, mid, (r - mid.astype(F32)).astype(BF16)


def _softplus(x):
    return jnp.maximum(x, 0.0) + jnp.log1p(jnp.exp(-jnp.abs(x)))


def _deltanet_kernel(q_ref, k_ref, v_ref, z_ref, tail_ref, cwq_ref, cwk_ref, cwv_ref, alog_ref, dtb_ref, nw_ref,
                     o_ref, pad_ref, qn_ref, kn_ref, vc_ref, mt_ref, bb_ref, qp_ref, op_ref, state_ref, oacc_ref):
    seq = q_ref.shape[0]
    nc = seq // DN_CHUNK
    c64 = DN_CHUNK
    dk = DN_DK

    pad_ref[0:CONV_HALO, :] = jnp.zeros((CONV_HALO, dk), F32)
    pad_ref[CONV_HALO + seq:, :] = jnp.zeros((CONV_HALO, dk), F32)

    def conv_into(src_ref, col, w_ref, dst_ref, l2_scale):
        pad_ref[CONV_HALO:CONV_HALO + seq, :] = src_ref[:, col:col + dk].astype(F32)
        w = w_ref[:, col:col + dk]

        def tile(r, carry):
            start = pl.multiple_of(r * CONV_TILE, CONV_TILE)
            win = pad_ref[pl.ds(start, CONV_TILE + 2 * CONV_HALO), :]
            y = jnp.zeros((CONV_TILE, dk), F32)
            for j in range(DN_CONV):
                lo = CONV_HALO + j - DN_CONV // 2
                y = y + win[lo:lo + CONV_TILE, :] * w[j:j + 1, :]
            y = _silu(y)
            if l2_scale is not None:
                y = y * (lax.rsqrt(jnp.sum(y * y, axis=1, keepdims=True) + EPS) * l2_scale)
            dst_ref[pl.ds(start, CONV_TILE), col:col + dk] = y.astype(dst_ref.dtype)
            return carry

        lax.fori_loop(0, seq // CONV_TILE, tile, 0)

    conv_into(q_ref, 0, cwq_ref, qn_ref, DN_DK ** -0.5)
    conv_into(k_ref, 0, cwk_ref, kn_ref, 1.0)
    for vl in range(DN_REP):
        conv_into(v_ref, vl * DN_DV, cwv_ref, vc_ref, None)

    rr = lax.broadcasted_iota(jnp.int32, (DN_ROWS, DN_ROWS), 0)
    cc = lax.broadcasted_iota(jnp.int32, (DN_ROWS, DN_ROWS), 1)
    same = (rr // c64) == (cc // c64)
    fwd_rows = rr < DN_REP * c64
    strict = same & ((fwd_rows & (rr > cc)) | (jnp.logical_not(fwd_rows) & (rr < cc)))
    eye = (rr == cc)
    eye_f = eye.astype(F32)
    r64 = lax.broadcasted_iota(jnp.int32, (c64, DN_ROWS), 0)
    c64i = lax.broadcasted_iota(jnp.int32, (c64, DN_ROWS), 1)
    eye_row = (r64 == (c64i % c64)).astype(F32)
    li = lax.broadcasted_iota(jnp.int32, (c64, c64), 0)
    lj = lax.broadcasted_iota(jnp.int32, (c64, c64), 1)
    tril = (lj <= li).astype(BF16)
    triu = (lj >= li).astype(BF16)
    lane8 = lax.broadcasted_iota(jnp.int32, (c64, 2 * DN_INST), 1)
    rb = lax.broadcasted_iota(jnp.int32, (DN_ROWS, dk), 0) // c64
    e_r = lax.broadcasted_iota(jnp.int32, (DN_INST * dk, dk), 0)
    e_c = lax.broadcasted_iota(jnp.int32, (DN_INST * dk, dk), 1)
    eye_tall = ((e_r % dk) == e_c).astype(F32)
    neg_a = -jnp.exp(alog_ref[0])
    dtb = dtb_ref[0]

    def stack(cols):
        return jnp.concatenate(cols, axis=0)

    def blockdiag(rows_):
        return jnp.where(same, jnp.concatenate([rows_] * DN_INST, axis=0), 0.0)

    def prep(c, carry):
        rows = pl.ds(pl.multiple_of(c * c64, c64), c64)
        tl = tail_ref[0, 0, rows, :]
        beta8 = 1.0 / (1.0 + jnp.exp(-tl))
        g8 = neg_a * _softplus(tl + dtb)
        parts = _split3_bf16(g8)
        gcf = _dot(tril, parts[0]) + (_dot(tril, parts[1]) + _dot(tril, parts[2]))
        gcb = _dot(triu, parts[0]) + (_dot(triu, parts[1]) + _dot(triu, parts[2]))
        gc8 = jnp.where(lane8 < DN_INST + DN_REP, gcf, gcb)
        gtot8 = jnp.sum(g8, axis=0, keepdims=True)
        bcol = stack([beta8[:, i:i + 1] for i in range(DN_INST)])
        gcol = stack([gc8[:, DN_INST + i:DN_INST + i + 1] for i in range(DN_INST)])
        gtc = stack([jnp.broadcast_to(gtot8[:, DN_INST + i:DN_INST + i + 1], (c64, 1)) for i in range(DN_INST)])

        kb = kn_ref[rows, :]
        qb = qn_ref[rows, :]
        vb = vc_ref[rows, :]
        k4b = stack([kb] * DN_INST)
        q4b = stack([qb] * DN_INST)
        k4 = k4b.astype(F32)
        v4 = stack([vb[:, (i % DN_REP) * DN_DV:(i % DN_REP + 1) * DN_DV] for i in range(DN_INST)]).astype(F32)

        gmat = jnp.broadcast_to(gcol, (DN_ROWS, DN_ROWS))
        grow = jnp.sum(jnp.where(eye, gmat, 0.0), axis=0, keepdims=True)
        decay = jnp.exp(jnp.where(strict, gmat - grow, NEG_INF))
        n_bd = bcol * _dot_nt(k4b, k4b) * decay
        qk_bd = _dot_nt(q4b, k4b) * (decay + eye_f)

        r_pow = -(n_bd[0:c64] + n_bd[c64:2 * c64] + n_bd[2 * c64:3 * c64] + n_bd[3 * c64:4 * c64])
        t_row = eye_row + r_pow
        r_pow = _dot(r_pow.astype(BF16), (-n_bd).astype(BF16))
        for _ in range(4):
            p_bd = blockdiag(r_pow).astype(BF16)
            rt = _dot(jnp.concatenate([r_pow, t_row], axis=0).astype(BF16), p_bd)
            r_pow = rt[0:c64]
            t_row = t_row + rt[c64:]
        t_row = t_row + _dot(t_row.astype(BF16), blockdiag(r_pow).astype(BF16))

        egc = jnp.exp(gcol)
        rhs = jnp.concatenate([k4 * (bcol * egc), v4 * bcol], axis=1).astype(BF16)
        wu = _dot(blockdiag(t_row).astype(BF16), rhs).astype(BF16)
        qk_wu = _dot(qk_bd.astype(BF16), wu)
        qp_ref[c] = (q4b.astype(F32) * egc - qk_wu[:, :dk]).astype(BF16)
        op_ref[c] = qk_wu[:, dk:].astype(BF16)
        kd = k4 * jnp.exp(gtc - gcol)
        kd_wide = jnp.concatenate([jnp.where(rb == i, kd, 0.0) for i in range(DN_INST)], axis=1).astype(BF16)
        kw = _dot_tn(kd_wide, wu)
        dl = stack([jnp.broadcast_to(jnp.exp(gtot8[:, DN_INST + i:DN_INST + i + 1]), (dk, 1)) for i in range(DN_INST)])
        mt_ref[c] = (dl * eye_tall - kw[:, :dk]).astype(BF16)
        bb_ref[c] = kw[:, dk:].astype(BF16)
        return carry

    lax.fori_loop(0, nc, prep, 0)

    state_ref[...] = jnp.zeros_like(state_ref)
    oacc_ref[...] = jnp.zeros_like(oacc_ref)

    def sweep(j, carry):
        for i in range(DN_INST):
            c = j if i < DN_REP else nc - 1 - j
            s_bf = state_ref[i].astype(BF16)
            new_s = _dot(mt_ref[c, i * dk:(i + 1) * dk, :], s_bf) + bb_ref[c, i * dk:(i + 1) * dk, :].astype(F32)
            o = _dot(qp_ref[c, i * c64:(i + 1) * c64, :], s_bf) + op_ref[c, i * c64:(i + 1) * c64, :].astype(F32)
            state_ref[i] = new_s
            rows = pl.ds(pl.multiple_of(c * c64, c64), c64)
            vcols = slice((i % DN_REP) * DN_DV, (i % DN_REP + 1) * DN_DV)
            oacc_ref[rows, vcols] = oacc_ref[rows, vcols] + o
        return carry

    lax.fori_loop(0, nc, sweep, 0)

    def finish(r, carry):
        rows = pl.ds(pl.multiple_of(r * CONV_TILE, CONV_TILE), CONV_TILE)
        for vl in range(DN_REP):
            vcols = slice(vl * DN_DV, (vl + 1) * DN_DV)
            o = _rms_rows(oacc_ref[rows, vcols]) * nw_ref[...]
            o_ref[rows, vcols] = (o * _silu(z_ref[rows, vcols].astype(F32))).astype(o_ref.dtype)
        return carry

    lax.fori_loop(0, seq // CONV_TILE, finish, 0)


def _deltanet_core(proj, tail8, conv_w, alog8, dtb8, norm_w, batch, seq):
    nc = seq // DN_CHUNK
    kq = DN_K_HEADS
    vw = DN_REP * DN_DV
    kv = 2 * DN_K_HEADS * DN_DK // vw
    kz = kv + DN_K_HEADS
    once = pl.Buffered(1)
    return pl.pallas_call(
        _deltanet_kernel, grid=(batch, DN_K_HEADS),
        in_specs=[pl.BlockSpec((seq, DN_DK), lambda b, h: (b, h), pipeline_mode=once),
                  pl.BlockSpec((seq, DN_DK), lambda b, h: (b, kq + h), pipeline_mode=once),
                  pl.BlockSpec((seq, vw), lambda b, h: (b, kv + h), pipeline_mode=once),
                  pl.BlockSpec((seq, vw), lambda b, h: (b, kz + h), pipeline_mode=once),
                  pl.BlockSpec((1, 1, seq, 2 * DN_INST), lambda b, h: (b, h, 0, 0), pipeline_mode=once),
                  pl.BlockSpec((DN_CONV, DN_DK), lambda b, h: (0, h)),
                  pl.BlockSpec((DN_CONV, DN_DK), lambda b, h: (0, kq + h)),
                  pl.BlockSpec((DN_CONV, vw), lambda b, h: (0, kv + h)),
                  pl.BlockSpec((1, 1, 2 * DN_INST), lambda b, h: (h, 0, 0)),
                  pl.BlockSpec((1, 1, 2 * DN_INST), lambda b, h: (h, 0, 0)),
                  pl.BlockSpec((1, DN_DV), lambda b, h: (0, 0))],
        out_specs=pl.BlockSpec((seq, vw), lambda b, h: (b, h)),
        out_shape=jax.ShapeDtypeStruct((batch * seq, DN_V_HEADS * DN_DV), BF16),
        scratch_shapes=[pltpu.VMEM((seq + 2 * CONV_HALO, DN_DK), F32),
                        pltpu.VMEM((seq, DN_DK), BF16), pltpu.VMEM((seq, DN_DK), BF16), pltpu.VMEM((seq, vw), BF16),
                        pltpu.VMEM((nc, DN_INST * DN_DK, DN_DK), BF16), pltpu.VMEM((nc, DN_INST * DN_DK, DN_DV), BF16),
                        pltpu.VMEM((nc, DN_ROWS, DN_DK), BF16), pltpu.VMEM((nc, DN_ROWS, DN_DV), BF16),
                        pltpu.VMEM((DN_INST, DN_DK, DN_DV), F32), pltpu.VMEM((seq, vw), F32)],
        compiler_params=_cparams("parallel", "parallel"),
    )(proj, proj, proj, proj, tail8, conv_w, conv_w, conv_w, alog8, dtb8, norm_w.reshape(1, DN_DV))


def _per_key_head(a):
    lead = a.shape[:-2]
    a = a.reshape(lead + (2, DN_K_HEADS, DN_REP))
    a = jnp.moveaxis(a, -2, 0)
    return a.reshape((DN_K_HEADS,) + lead + (DN_INST,))


def _deltanet_mixer(x2, nw, w_in, conv_w, a_log, dt_bias, norm_w, w_out, batch, seq):
    conv_dim = 2 * DN_K_HEADS * DN_DK + DN_V_HEADS * DN_DV
    main = conv_dim + DN_V_HEADS * DN_DV
    proj = _norm_proj(x2, nw, w_in, n_out=main, tn=512, out_dtype=BF16)
    tail = _norm_proj(x2, nw, w_in, n_out=LANES, tn=LANES, out_dtype=F32, col_off=main)
    tail = tail.reshape(batch, seq, 2, 2, DN_V_HEADS)
    tail8 = jnp.concatenate([_per_key_head(tail[:, :, 0]), _per_key_head(tail[:, :, 1])], axis=-1)
    tail8 = jnp.moveaxis(tail8, 0, 1)
    zeros = jnp.zeros((DN_K_HEADS, DN_INST), F32)
    alog8 = jnp.concatenate([zeros, _per_key_head(a_log)], axis=-1).reshape(DN_K_HEADS, 1, 2 * DN_INST)
    dtb8 = jnp.concatenate([zeros, _per_key_head(dt_bias)], axis=-1).reshape(DN_K_HEADS, 1, 2 * DN_INST)
    o = _deltanet_core(proj, tail8, conv_w, alog8, dtb8, norm_w, batch, seq)
    return _out_proj(o, w_out, x2)


def kernel(x, norm_mix_w, norm_ffn_w, final_norm_w, rel_bias_table, ret_w_in, ret_w_out, dil_w_in, dil_w_out,
           dn_w_in, dn_conv_w, dn_a_log, dn_dt_bias, dn_norm_w, dn_w_out, moe_w_router, moe_w_gate, moe_w_up,
           moe_w_down):
    batch, seq, d = x.shape
    x2 = x.reshape(batch * seq, d)
    for i in range(norm_mix_w.shape[0]):
        j = i // N_MIXERS
        kind = i % N_MIXERS
        if kind == 0:
            x2 = _retention_mixer(x2, norm_mix_w[i], (ret_w_in, j), (ret_w_out, j), batch, seq)
        elif kind == 1:
            x2 = _dilated_mixer(x2, norm_mix_w[i], (dil_w_in, j), (dil_w_out, j), rel_bias_table, batch, seq)
        else:
            x2 = _deltanet_mixer(x2, norm_mix_w[i], (dn_w_in, j), dn_conv_w[j], dn_a_log[j], dn_dt_bias[j],
                                 dn_norm_w[j], (dn_w_out, j), batch, seq)
        x2 = _moe(x2, norm_ffn_w[i], moe_w_router[i], moe_w_gate, moe_w_up, moe_w_down, i, batch, seq)
    return _final_norm(x2, final_norm_w).reshape(batch, seq, d)
```

```python
import functools
import math

import jax
import jax.numpy as jnp
from jax import lax
from jax.experimental import pallas as pl
from jax.experimental.pallas import tpu as pltpu

F32 = jnp.float32
BF16 = jnp.bfloat16

D_MODEL = 2048
EPS = 1e-6
NEG_INF = -1e30
RET_HEADS = 8
RET_DK = D_MODEL // RET_HEADS
RET_DV = 2 * RET_DK
ROPE_BASE = 10000.0
RET_BWD_DECAY_OFFSET = 0.5
DIL_PATTERNS = ((128, 1), (512, 4), (2048, 16))
DIL_HEADS = 16
DIL_DH = D_MODEL // DIL_HEADS
REL_BUCKETS = 32
REL_MAX_DIST = 1024
DN_K_HEADS = 16
DN_V_HEADS = 32
DN_DK = 128
DN_DV = 128
DN_CONV = 5
N_EXPERTS = 16
EXPERT_FF = D_MODEL // 2
CAPACITY_FACTOR = 2
N_MIXERS = 3

LANES = 128
VMEM_LIMIT_BYTES = 56 * 1024 * 1024
ROW_TILE = 1024
RET_CHUNK = 256
MOE_FF_TILE = 256
MOE_DMA_UNROLL = 8


def _cparams(*sem):
    return pltpu.CompilerParams(dimension_semantics=sem, vmem_limit_bytes=VMEM_LIMIT_BYTES)


def _dot(a, b):
    return jnp.dot(a, b, preferred_element_type=F32)


def _dot_nt(a, b):
    return lax.dot_general(a, b, (((1,), (1,)), ((), ())), preferred_element_type=F32)


def _dot_tn(a, b):
    return lax.dot_general(a, b, (((0,), (0,)), ((), ())), preferred_element_type=F32)


def _silu(x):
    return x / (1.0 + jnp.exp(-x))


def _rms_rows(x):
    return x * lax.rsqrt(jnp.mean(x * x, axis=-1, keepdims=True) + EPS)


def _norm_proj_kernel(x_ref, nw_ref, w_ref, o_ref, hn_ref):
    @pl.when(pl.program_id(1) == 0)
    def _():
        hn_ref[...] = (_rms_rows(x_ref[...]) * nw_ref[...]).astype(BF16)

    o_ref[...] = _dot(hn_ref[...], w_ref[...].astype(BF16)).astype(o_ref.dtype)


def _norm_proj_rope_kernel(x_ref, nw_ref, w_ref, cos_ref, sin_ref, o_ref, hn_ref, *, n_q_tiles, n_rope_tiles, k_scale):
    j = pl.program_id(1)

    @pl.when(j == 0)
    def _():
        hn_ref[...] = (_rms_rows(x_ref[...]) * nw_ref[...]).astype(BF16)

    acc = _dot(hn_ref[...], w_ref[...].astype(BF16))

    @pl.when(j < n_rope_tiles)
    def _():
        c = cos_ref[...]
        s = sin_ref[...]
        half = c.shape[1]
        scale = jnp.where(j >= n_q_tiles, k_scale, 1.0).astype(F32)
        for hh in range(acc.shape[1] // (2 * half)):
            x1 = acc[:, 2 * hh * half:(2 * hh + 1) * half]
            x2 = acc[:, (2 * hh + 1) * half:(2 * hh + 2) * half]
            o_ref[:, 2 * hh * half:(2 * hh + 1) * half] = ((x1 * c - x2 * s) * scale).astype(o_ref.dtype)
            o_ref[:, (2 * hh + 1) * half:(2 * hh + 2) * half] = ((x2 * c + x1 * s) * scale).astype(o_ref.dtype)

    @pl.when(j >= n_rope_tiles)
    def _():
        o_ref[...] = acc.astype(o_ref.dtype)


def _norm_proj(x2, nw, w, *, n_out, tn, out_dtype, col_off=0, rope=None):
    m, k = x2.shape
    w_stack, layer = w
    tm = min(ROW_TILE, m)
    grid = (m // tm, n_out // tn)
    off = col_off // tn
    in_specs = [pl.BlockSpec((tm, k), lambda i, j: (i, 0)),
                pl.BlockSpec((1, k), lambda i, j: (0, 0)),
                pl.BlockSpec((None, k, tn), lambda i, j: (layer, 0, j + off))]
    args = [x2, nw.reshape(1, k), w_stack]
    if rope is None:
        body = _norm_proj_kernel
    else:
        cos, sin, seq, n_q_cols, n_rope_cols, k_scale = rope
        nsb = seq // tm
        in_specs += [pl.BlockSpec((tm, cos.shape[1]), lambda i, j: (i % nsb, 0)),
                     pl.BlockSpec((tm, cos.shape[1]), lambda i, j: (i % nsb, 0))]
        args += [cos, sin]
        body = functools.partial(_norm_proj_rope_kernel, n_q_tiles=n_q_cols // tn, n_rope_tiles=n_rope_cols // tn,
                                 k_scale=k_scale)
    return pl.pallas_call(
        body, name="norm_proj", grid=grid, in_specs=in_specs,
        out_specs=pl.BlockSpec((tm, tn), lambda i, j: (i, j)),
        out_shape=jax.ShapeDtypeStruct((m, n_out), out_dtype),
        scratch_shapes=[pltpu.VMEM((tm, k), BF16)],
        compiler_params=_cparams("parallel", "arbitrary"),
    )(*args)


def _out_proj_kernel(a_ref, w_ref, res_ref, o_ref):
    o_ref[...] = res_ref[...] + _dot(a_ref[...], w_ref[...].astype(BF16))


def _out_proj(a, w, res, *, tn=512):
    m, k = a.shape
    w_stack, layer = w
    n = w_stack.shape[2]
    tm = min(ROW_TILE, m)
    return pl.pallas_call(
        _out_proj_kernel, name="out_proj", grid=(m // tm, n // tn),
        in_specs=[pl.BlockSpec((tm, k), lambda i, j: (i, 0)),
                  pl.BlockSpec((None, k, tn), lambda i, j: (layer, 0, j)),
                  pl.BlockSpec((tm, tn), lambda i, j: (i, j))],
        out_specs=pl.BlockSpec((tm, tn), lambda i, j: (i, j)),
        out_shape=jax.ShapeDtypeStruct((m, n), F32),
        compiler_params=_cparams("parallel", "arbitrary"),
    )(a, w_stack, res)


def _final_norm_kernel(x_ref, nw_ref, o_ref):
    o_ref[...] = _rms_rows(x_ref[...]) * nw_ref[...]


def _final_norm(x2, nw):
    m, k = x2.shape
    tm = min(ROW_TILE, m)
    return pl.pallas_call(
        _final_norm_kernel, name="final_norm", grid=(m // tm,),
        in_specs=[pl.BlockSpec((tm, k), lambda i: (i, 0)), pl.BlockSpec((1, k), lambda i: (0, 0))],
        out_specs=pl.BlockSpec((tm, k), lambda i: (i, 0)),
        out_shape=jax.ShapeDtypeStruct((m, k), F32),
        compiler_params=_cparams("parallel"),
    )(x2, nw.reshape(1, k))


def _retention_kernel(q_ref, k_ref, v_ref, g_ref, dmat_ref, vec_ref, o_ref, state_ref, oacc_ref):
    sweep = pl.program_id(2)
    c = pl.program_id(3)
    nc = pl.num_programs(3)

    @pl.when(c == 0)
    def _():
        state_ref[...] = jnp.zeros_like(state_ref)

    q = q_ref[...]
    k = k_ref[...]
    v = v_ref[...]
    vec = vec_ref[0]

    @pl.when(sweep == 0)
    def _():
        p = (_dot_nt(q, k) * dmat_ref[0]).astype(BF16)
        inter = _dot(q, state_ref[...].astype(BF16)) * vec[:, 0:1]
        oacc_ref[c] = _dot(p, v) + inter
        kd = (k.astype(F32) * vec[:, 2:3]).astype(BF16)
        state_ref[...] = state_ref[...] * vec[0:1, 4:5] + _dot_tn(kd, v)

    @pl.when(sweep == 1)
    def _():
        inter = _dot(q, state_ref[...].astype(BF16)) * vec[:, 1:2]
        o = _rms_rows(oacc_ref[nc - 1 - c] + inter)
        o_ref[...] = (o * _silu(g_ref[...].astype(F32))).astype(o_ref.dtype)
        kd = (k.astype(F32) * vec[:, 3:4]).astype(BF16)
        state_ref[...] = state_ref[...] * vec[0:1, 5:6] + _dot_tn(kd, v)


def _retention_tables(chunk):
    hh = jnp.arange(RET_HEADS, dtype=F32)
    lg_f = jnp.log1p(-jnp.exp2(-5.0 - hh))
    lg_b = jnp.log1p(-jnp.exp2(-(5.0 + RET_BWD_DECAY_OFFSET) - hh))
    t = jnp.arange(chunk, dtype=F32)
    diff = t[:, None] - t[None, :]
    dmat = jnp.where(diff[None] >= 0,
                     jnp.exp(jnp.maximum(diff, 0.0)[None] * lg_f[:, None, None]),
                     jnp.exp(jnp.maximum(-diff, 0.0)[None] * lg_b[:, None, None]))
    cols = [jnp.exp((t[None, :] + 1.0) * lg_f[:, None]),
            jnp.exp((chunk - t)[None, :] * lg_b[:, None]),
            jnp.exp((chunk - 1.0 - t)[None, :] * lg_f[:, None]),
            jnp.exp(t[None, :] * lg_b[:, None]),
            jnp.broadcast_to(jnp.exp(chunk * lg_f)[:, None], (RET_HEADS, chunk)),
            jnp.broadcast_to(jnp.exp(chunk * lg_b)[:, None], (RET_HEADS, chunk))]
    cols += [jnp.zeros((RET_HEADS, chunk), F32)] * 2
    return dmat, jnp.stack(cols, axis=-1)


def _retention_core(proj, batch, seq):
    cq = min(RET_CHUNK, seq)
    nc = seq // cq
    dmat, vec = _retention_tables(cq)
    kq = RET_HEADS
    kv = 2 * RET_HEADS * RET_DK // RET_DV
    kg = kv + RET_HEADS

    def row(b, s, c):
        return b * nc + c + s * (nc - 1 - 2 * c)

    def row_out(b, s, c):
        return b * nc + nc - 1 - c * s

    return pl.pallas_call(
        _retention_kernel, name="retention_core", grid=(batch, RET_HEADS, 2, nc),
        in_specs=[pl.BlockSpec((cq, RET_DK), lambda b, h, s, c: (row(b, s, c), h)),
                  pl.BlockSpec((cq, RET_DK), lambda b, h, s, c: (row(b, s, c), kq + h)),
                  pl.BlockSpec((cq, RET_DV), lambda b, h, s, c: (row(b, s, c), kv + h)),
                  pl.BlockSpec((cq, RET_DV), lambda b, h, s, c: (row_out(b, s, c), kg + h)),
                  pl.BlockSpec((1, cq, cq), lambda b, h, s, c: (h, 0, 0)),
                  pl.BlockSpec((1, cq, 8), lambda b, h, s, c: (h, 0, 0))],
        out_specs=pl.BlockSpec((cq, RET_DV), lambda b, h, s, c: (row_out(b, s, c), h)),
        out_shape=jax.ShapeDtypeStruct((batch * seq, RET_HEADS * RET_DV), BF16),
        scratch_shapes=[pltpu.VMEM((RET_DK, RET_DV), F32), pltpu.VMEM((nc, cq, RET_DV), F32)],
        compiler_params=_cparams("parallel", "parallel", "arbitrary", "arbitrary"),
    )(proj, proj, proj, proj, dmat, vec)


def _rope_tables(seq, half):
    inv = ROPE_BASE ** (-jnp.arange(half, dtype=F32) / half)
    ang = jnp.arange(seq, dtype=F32)[:, None] * inv[None, :]
    return jnp.cos(ang), jnp.sin(ang)


def _retention_mixer(x2, nw, w_in, w_out, batch, seq):
    cos, sin = _rope_tables(seq, RET_DK // 2)
    qk_cols = 2 * RET_HEADS * RET_DK
    proj = _norm_proj(x2, nw, w_in, n_out=w_in[0].shape[2], tn=512, out_dtype=BF16,
                      rope=(cos, sin, seq, qk_cols // 2, qk_cols, RET_DK ** -0.5))
    o = _retention_core(proj, batch, seq)
    return _out_proj(o, w_out, x2)


def _split_bf16(x):
    hi = x.astype(BF16)
    return hi, (x - hi.astype(F32)).astype(BF16)


def _router_kernel(x_ref, nw_ref, wr_ref, wrt_ref, hx_ref, afft_ref):
    half = x_ref.shape[1] // 2
    h = _rms_rows(x_ref[...]) * nw_ref[...]
    h_hi, h_lo = _split_bf16(h)
    bits = lax.bitcast_convert_type(h_hi.astype(F32), jnp.uint32)
    hx_ref[:, :half] = (bits[:, half:] & jnp.uint32(0xFFFF0000)) | (bits[:, :half] >> 16)
    w_hi, w_lo = _split_bf16(wr_ref[...])
    lg = _dot(h_hi, w_hi) + (_dot(h_lo, w_hi) + _dot(h_hi, w_lo))
    lane = lax.broadcasted_iota(jnp.int32, lg.shape, 1)
    lg = jnp.where(lane < N_EXPERTS, lg, NEG_INF)
    e = jnp.exp(lg - jnp.max(lg, axis=1, keepdims=True))
    hx_ref[:, half:] = lax.bitcast_convert_type(e / jnp.sum(e, axis=1, keepdims=True), jnp.uint32)
    wt_hi, wt_lo = _split_bf16(wrt_ref[...])
    lgt = _dot_nt(wt_hi, h_hi) + (_dot_nt(wt_hi, h_lo) + _dot_nt(wt_lo, h_hi))
    et = jnp.exp(lgt - jnp.max(lgt, axis=0, keepdims=True))
    afft_ref[0] = et / jnp.sum(et, axis=0, keepdims=True)


def _router(x2, nw, w_router, batch, seq):
    m, d = x2.shape
    tm = min(512, seq)
    nsb = seq // tm
    hw = d // 2 + LANES
    wr = jnp.pad(w_router, ((0, 0), (0, LANES - N_EXPERTS)))
    return pl.pallas_call(
        _router_kernel, name="moe_router", grid=(m // tm,),
        in_specs=[pl.BlockSpec((tm, d), lambda i: (i, 0)),
                  pl.BlockSpec((1, d), lambda i: (0, 0)),
                  pl.BlockSpec((d, LANES), lambda i: (0, 0)),
                  pl.BlockSpec((N_EXPERTS, d), lambda i: (0, 0))],
        out_specs=[pl.BlockSpec((tm, hw), lambda i: (i, 0)),
                   pl.BlockSpec((1, N_EXPERTS, tm), lambda i: (i // nsb, 0, i % nsb))],
        out_shape=[jax.ShapeDtypeStruct((m, hw), jnp.uint32),
                   jax.ShapeDtypeStruct((batch, N_EXPERTS, seq), F32)],
        compiler_params=_cparams("parallel"),
    )(x2, nw.reshape(1, d), wr, w_router.T)


def _topk_kernel(aff_ref, idx_ref, loc_ref, off_ref, end_ref, *, cap, n_groups, n_blk):
    n = n_groups * n_blk
    bits = lax.bitcast_convert_type(aff_ref[...], jnp.int32)
    ri = lax.broadcasted_iota(jnp.int32, (n, n), 0)
    ci = lax.broadcasted_iota(jnp.int32, (n, n), 1)
    same = (ri // n_blk) == (ci // n_blk)
    grp_ones = same.astype(BF16)
    grp_before = (same & (ci < ri)).astype(BF16)
    li = lax.broadcasted_iota(jnp.int32, (LANES, LANES), 0)
    lj = lax.broadcasted_iota(jnp.int32, (LANES, LANES), 1)
    incl = (li <= lj).astype(BF16)
    ones = jnp.ones((LANES, LANES), BF16)

    def row_total(mask):
        return _dot(mask.astype(BF16), ones)

    def group_count(mask):
        return _dot(grp_ones, row_total(mask).astype(BF16))

    def search(i, tau):
        cand = tau | jnp.left_shift(jnp.int32(1), 30 - i)
        return jnp.where(group_count(bits >= cand) >= cap, cand, tau)

    tau = lax.fori_loop(0, 31, search, jnp.zeros((n, LANES), jnp.int32))

    def cumsum(mask):
        mb = mask.astype(BF16)
        tot = _dot(mb, ones)
        return _dot(mb, incl), _dot(grp_before, tot.astype(BF16)), tot

    gt = bits > tau
    eq = bits == tau
    need = cap - group_count(gt)
    eq_loc, eq_off, _ = cumsum(eq)
    sel = gt | (eq & (eq_loc + eq_off <= need))
    loc, off, tot = cumsum(sel)
    loc_ref[...] = loc
    off_ref[...] = off
    end_ref[...] = off + tot

    slot = lax.broadcasted_iota(jnp.int32, (cap, LANES), 0).astype(F32)
    lane = lax.broadcasted_iota(jnp.int32, (cap, LANES), 1)
    eye = lax.broadcasted_iota(jnp.int32, (n_blk, LANES), 0) == lax.broadcasted_iota(jnp.int32, (n_blk, LANES), 1)
    pad = jnp.zeros((LANES - n_blk, LANES), BF16)

    def compact(g, acc):
        rows = pl.ds(pl.multiple_of(g * n_blk, n_blk), n_blk)
        end_row = jnp.sum(jnp.where(eye, end_ref[rows, :], 0.0), axis=0, keepdims=True)
        off_row = jnp.sum(jnp.where(eye, off_ref[rows, :], 0.0), axis=0, keepdims=True)
        blk = jnp.sum(((end_row <= slot) & (lane < n_blk)).astype(F32), axis=1, keepdims=True)
        onehot = lane == blk.astype(jnp.int32)
        loc_pad = jnp.concatenate([loc_ref[rows, :].astype(BF16), pad], axis=0)
        in_blk = _dot(onehot.astype(BF16), loc_pad)
        rank = slot - jnp.sum(jnp.where(onehot, off_row, 0.0), axis=1, keepdims=True)
        pos = blk * LANES + jnp.sum((in_blk <= rank).astype(F32), axis=1, keepdims=True)
        return jnp.where(lane == g, pos.astype(jnp.int32), acc)

    idx_ref[...] = lax.fori_loop(0, n_groups, compact, jnp.zeros((cap, LANES), jnp.int32))


def _topk(afft, cap):
    batch, n_e, seq = afft.shape
    n_groups = batch * n_e
    n_blk = seq // LANES
    n = n_groups * n_blk
    assert n_groups <= LANES and n_blk <= LANES
    idx = pl.pallas_call(
        functools.partial(_topk_kernel, cap=cap, n_groups=n_groups, n_blk=n_blk), name="moe_topk",
        out_shape=jax.ShapeDtypeStruct((cap, LANES), jnp.int32),
        scratch_shapes=[pltpu.VMEM((n, LANES), F32)] * 3,
        compiler_params=pltpu.CompilerParams(vmem_limit_bytes=VMEM_LIMIT_BYTES),
    )(afft.reshape(n, LANES))
    return idx[:, :n_groups].T.reshape(batch, n_e, cap)


def _moe_ffn_kernel(idx_ref, hx_hbm, wg_ref, wu_ref, wd_ref, xres_hbm, out_hbm,
                    hbuf, xbf, acc, rbuf, gate, sem_h, sem_r, sem_s):
    del xres_hbm
    e = pl.program_id(0)
    f = pl.program_id(1)
    n_e = pl.num_programs(0)
    nf = pl.num_programs(1)
    rows = xbf.shape[0]
    d = xbf.shape[1]
    per_step = rows // nf
    slot = e % 2
    base = e * rows
    nxt = ((e + 1) % n_e) * rows

    def h_copy(row_id, r, s):
        return pltpu.make_async_copy(hx_hbm.at[pl.ds(row_id, 1), :], hbuf.at[s, pl.ds(r, 1), :], sem_h.at[s])

    def h_wait(s):
        pltpu.make_async_copy(hx_hbm.at[pl.ds(0, rows), :], hbuf.at[s], sem_h.at[s]).wait()

    @pl.when((e == 0) & (f == 0))
    def _():
        def start(r, carry):
            h_copy(idx_ref[r], r, 0).start()
            return carry

        lax.fori_loop(0, rows, start, 0, unroll=MOE_DMA_UNROLL)

    @pl.when(f == 0)
    def _():
        h_wait(slot)
        half = d // 2
        words = hbuf[slot, :, :half]
        xbf[:, :half] = lax.bitcast_convert_type(words << 16, F32).astype(BF16)
        xbf[:, half:] = lax.bitcast_convert_type(words & jnp.uint32(0xFFFF0000), F32).astype(BF16)
        lane = lax.broadcasted_iota(jnp.int32, (rows, LANES), 1)
        aff = lax.bitcast_convert_type(hbuf[slot, :, half:], F32)
        gate[...] = jnp.sum(jnp.where(lane == e, aff, 0.0), axis=1, keepdims=True)

    for u in range(per_step):
        r = f * per_step + u
        h_copy(idx_ref[nxt + r], r, 1 - slot).start()
        pltpu.make_async_copy(out_hbm.at[pl.ds(idx_ref[base + r], 1), :], rbuf.at[pl.ds(r, 1), :], sem_r.at[0]).start()

    x = xbf[...]
    hid = (_silu(_dot(x, wg_ref[...].astype(BF16))) * _dot(x, wu_ref[...].astype(BF16))).astype(BF16)
    part = _dot(hid, wd_ref[...].astype(BF16))

    @pl.when(f == 0)
    def _():
        acc[...] = part

    @pl.when(f > 0)
    def _():
        acc[...] += part

    @pl.when(f == nf - 1)
    def _():
        pltpu.make_async_copy(out_hbm.at[pl.ds(0, rows), :], rbuf, sem_r.at[0]).wait()
        rbuf[...] = rbuf[...] + acc[...] * gate[...]

        def start(r, carry):
            pltpu.make_async_copy(rbuf.at[pl.ds(r, 1), :], out_hbm.at[pl.ds(idx_ref[base + r], 1), :],
                                  sem_s.at[0]).start()
            return carry

        lax.fori_loop(0, rows, start, 0, unroll=MOE_DMA_UNROLL)
        pltpu.make_async_copy(rbuf, out_hbm.at[pl.ds(0, rows), :], sem_s.at[0]).wait()

    @pl.when((e == n_e - 1) & (f == nf - 1))
    def _():
        h_wait(1 - slot)


def _moe_ffn(rows_idx, hx, x2, w_gate, w_up, w_down, layer):
    m, d = x2.shape
    _, n_e, _, ff = w_gate.shape
    rows = rows_idx.shape[0] // n_e
    tf = min(MOE_FF_TILE, ff)
    grid_spec = pltpu.PrefetchScalarGridSpec(
        num_scalar_prefetch=1, grid=(n_e, ff // tf),
        in_specs=[pl.BlockSpec(memory_space=pl.ANY),
                  pl.BlockSpec((None, None, d, tf), lambda e, f, idx: (layer, e, 0, f)),
                  pl.BlockSpec((None, None, d, tf), lambda e, f, idx: (layer, e, 0, f)),
                  pl.BlockSpec((None, None, tf, d), lambda e, f, idx: (layer, e, f, 0)),
                  pl.BlockSpec(memory_space=pl.ANY)],
        out_specs=pl.BlockSpec(memory_space=pl.ANY),
        scratch_shapes=[pltpu.VMEM((2, rows, hx.shape[1]), jnp.uint32), pltpu.VMEM((rows, d), BF16),
                        pltpu.VMEM((rows, d), F32), pltpu.VMEM((rows, d), F32), pltpu.VMEM((rows, 1), F32),
                        pltpu.SemaphoreType.DMA((2,)), pltpu.SemaphoreType.DMA((1,)),
                        pltpu.SemaphoreType.DMA((1,))])
    return pl.pallas_call(
        _moe_ffn_kernel, name="moe_ffn", grid_spec=grid_spec,
        out_shape=jax.ShapeDtypeStruct((m, d), F32),
        input_output_aliases={5: 0},
        compiler_params=_cparams("arbitrary", "arbitrary"),
    )(rows_idx, hx, w_gate, w_up, w_down, x2)


def _moe(x2, nw, w_router, w_gate, w_up, w_down, layer, batch, seq):
    cap = CAPACITY_FACTOR * seq // N_EXPERTS
    hx, afft = _router(x2, nw, w_router, batch, seq)
    idx = _topk(afft, cap)
    rows_idx = idx + (jnp.arange(batch, dtype=jnp.int32) * seq)[:, None, None]
    rows_idx = jnp.transpose(rows_idx, (1, 0, 2)).reshape(-1)
    return _moe_ffn(rows_idx, hx, x2, w_gate, w_up, w_down, layer)


DIL_BLOCK = 64


def _t5_bucket(rel):
    nb = REL_BUCKETS // 2
    max_exact = nb // 2
    ret = jnp.where(rel > 0, nb, 0)
    n = jnp.abs(rel)
    nf = jnp.maximum(n, 1).astype(F32)
    large = max_exact + (jnp.log(nf / max_exact) / math.log(REL_MAX_DIST / max_exact)
                         * (nb - max_exact)).astype(jnp.int32)
    large = jnp.minimum(large, nb - 1)
    return ret + jnp.where(n < max_exact, n, large)


def _dil_bias_kernel(table_ref, bucket_ref, o_ref):
    col = pl.program_id(0)
    bkt = bucket_ref[0]
    acc = jnp.zeros(bkt.shape, F32)
    for b in range(REL_BUCKETS):
        acc = jnp.where(bkt == b, table_ref[b, col], acc)
    s = lax.broadcasted_iota(jnp.int32, bkt.shape, 0)
    t = lax.broadcasted_iota(jnp.int32, bkt.shape, 1)
    o_ref[0] = jnp.where((t >= s) & (t <= s + 2 * DIL_BLOCK), acc, NEG_INF)


def _dil_bias(rel_table):
    qb = DIL_BLOCK
    rel_steps = jnp.arange(3 * qb)[None, :] - jnp.arange(qb)[:, None] - qb
    buckets = jnp.stack([_t5_bucket(rel_steps * dil) for _, dil in DIL_PATTERNS]).astype(jnp.int32)
    n_col = rel_table.shape[1]
    return pl.pallas_call(
        _dil_bias_kernel, name="dil_bias", grid=(n_col,),
        in_specs=[pl.BlockSpec(memory_space=pltpu.SMEM),
                  pl.BlockSpec((1, qb, 3 * qb), lambda c: (c // DIL_HEADS, 0, 0))],
        out_specs=pl.BlockSpec((1, qb, 3 * qb), lambda c: (c, 0, 0)),
        out_shape=jax.ShapeDtypeStruct((n_col, qb, 3 * qb), F32),
        compiler_params=_cparams("parallel"),
    )(rel_table, buckets)


def _dil_attn_kernel(q_ref, kp_ref, kc_ref, kn_ref, vp_ref, vc_ref, vn_ref, bias_ref, o_ref, lse_ref):
    jb = pl.program_id(2)
    nb = pl.num_programs(2)
    qb = q_ref.shape[0]
    t = lax.broadcasted_iota(jnp.int32, (qb, 3 * qb), 1)
    in_seq = ((jb > 0) | (t >= qb)) & ((jb < nb - 1) | (t < 2 * qb))
    kw = jnp.concatenate([kp_ref[...], kc_ref[...], kn_ref[...]], axis=0)
    vw = jnp.concatenate([vp_ref[...], vc_ref[...], vn_ref[...]], axis=0)
    lane = lax.broadcasted_iota(jnp.int32, (qb, LANES), 1)
    lse = jnp.zeros((qb, LANES), F32)
    heads = [slice(h * DIL_DH, (h + 1) * DIL_DH) for h in range(DIL_HEADS)]
    scores = [_dot_nt(q_ref[:, hs], kw[:, hs]) for hs in heads]
    probs, dens = [], []
    for h, s in enumerate(scores):
        s = jnp.where(in_seq, s * (DIL_DH ** -0.5) + bias_ref[h], NEG_INF)
        m = jnp.max(s, axis=1, keepdims=True)
        p = jnp.exp(s - m)
        den = jnp.sum(p, axis=1, keepdims=True)
        probs.append(p.astype(BF16))
        dens.append(den)
        lse = jnp.where(lane == h, m + jnp.log(den), lse)
    outs = [_dot(p, vw[:, hs]) for p, hs in zip(probs, heads)]
    for hs, o, den in zip(heads, outs, dens):
        o_ref[:, hs] = (o / den).astype(o_ref.dtype)
    lse_ref[0, 0, 0] = lse[:, :DIL_HEADS]


def _dil_group(proj, bias, gi, dil, batch, seq):
    qb = DIL_BLOCK
    l = seq // dil
    nb = l // qb
    hw = DIL_HEADS * DIL_DH
    n_grp = proj.shape[1] // hw
    p2 = proj.reshape(batch * l, dil * proj.shape[1])
    cq = 0

    def spec(which, shift):
        def imap(b, r, j):
            jj = jnp.clip(j + shift, 0, nb - 1)
            return (b * nb + jj, r * n_grp + cq + which)
        return pl.BlockSpec((qb, hw), imap)

    o, lse = pl.pallas_call(
        _dil_attn_kernel, name=f"dil_attn_d{dil}", grid=(batch, dil, nb),
        in_specs=[spec(0, 0), spec(1, -1), spec(1, 0), spec(1, 1), spec(2, -1), spec(2, 0), spec(2, 1),
                  pl.BlockSpec((DIL_HEADS, qb, 3 * qb), lambda b, r, j: (gi, 0, 0))],
        out_specs=[pl.BlockSpec((qb, hw), lambda b, r, j: (b * nb + j, r)),
                   pl.BlockSpec((1, 1, 1, qb, DIL_HEADS), lambda b, r, j: (b, j, r, 0, 0))],
        out_shape=[jax.ShapeDtypeStruct((batch * l, dil * hw), BF16),
                   jax.ShapeDtypeStruct((batch, nb, dil, qb, DIL_HEADS), F32)],
        compiler_params=_cparams("parallel", "parallel", "arbitrary"),
    )(p2, p2, p2, p2, p2, p2, p2, bias)
    lse = jnp.transpose(lse, (0, 1, 3, 2, 4)).reshape(batch * seq, DIL_HEADS)
    return o.reshape(batch * seq, hw), lse


def _dil_out_kernel(o0_ref, o1_ref, o2_ref, l0_ref, l1_ref, l2_ref, w_ref, res_ref, out_ref, comb_ref):
    @pl.when(pl.program_id(1) == 0)
    def _():
        l0, l1, l2 = l0_ref[...], l1_ref[...], l2_ref[...]
        m = jnp.maximum(jnp.maximum(l0, l1), l2)
        e0, e1, e2 = jnp.exp(l0 - m), jnp.exp(l1 - m), jnp.exp(l2 - m)
        den = e0 + e1 + e2
        w0, w1, w2 = e0 / den, e1 / den, e2 / den
        for h in range(DIL_HEADS):
            hs = slice(h * DIL_DH, (h + 1) * DIL_DH)
            comb_ref[:, hs] = (w0[:, h:h + 1] * o0_ref[:, hs].astype(F32) + w1[:, h:h + 1] * o1_ref[:, hs].astype(F32)
                               + w2[:, h:h + 1] * o2_ref[:, hs].astype(F32)).astype(BF16)

    out_ref[...] = res_ref[...] + _dot(comb_ref[...], w_ref[...].astype(BF16))


def _dil_out(os_, lses, w, res, *, tn=512):
    m, k = os_[0].shape
    w_stack, layer = w
    n = w_stack.shape[2]
    tm = min(ROW_TILE, m)
    ospec = pl.BlockSpec((tm, k), lambda i, j: (i, 0))
    lspec = pl.BlockSpec((tm, DIL_HEADS), lambda i, j: (i, 0))
    return pl.pallas_call(
        _dil_out_kernel, name="dil_out", grid=(m // tm, n // tn),
        in_specs=[ospec, ospec, ospec, lspec, lspec, lspec,
                  pl.BlockSpec((None, k, tn), lambda i, j: (layer, 0, j)),
                  pl.BlockSpec((tm, tn), lambda i, j: (i, j))],
        out_specs=pl.BlockSpec((tm, tn), lambda i, j: (i, j)),
        out_shape=jax.ShapeDtypeStruct((m, n), F32),
        scratch_shapes=[pltpu.VMEM((tm, k), BF16)],
        compiler_params=_cparams("parallel", "arbitrary"),
    )(*os_, *lses, w_stack, res)


def _dilated_mixer(x2, nw, w_in, w_out, rel_table, batch, seq):
    gw = w_in[0].shape[2] // len(DIL_PATTERNS)
    bias = _dil_bias(rel_table)
    outs = []
    for gi, (_, dil) in enumerate(DIL_PATTERNS):
        proj = _norm_proj(x2, nw, w_in, n_out=gw, tn=512, out_dtype=BF16, col_off=gi * gw)
        outs.append(_dil_group(proj, bias, gi, dil, batch, seq))
    return _dil_out([o for o, _ in outs], [l for _, l in outs], w_out, x2)


DN_CHUNK = 64
DN_REP = DN_V_HEADS // DN_K_HEADS
DN_INST = 2 * DN_REP
DN_ROWS = DN_INST * DN_CHUNK
DN_PREP_GROUP = 4
CONV_TILE = 256
CONV_HALO = 8


def _split3_bf16(x):
    hi = x.astype(BF16)
    r = x - hi.astype(F32)
    mid = r.astype(BF16)
    return hi, mid, (r - mid.astype(F32)).astype(BF16)


def _softplus(x):
    return jnp.maximum(x, 0.0) + jnp.log1p(jnp.exp(-jnp.abs(x)))


def _deltanet_kernel(q_ref, k_ref, v_ref, z_ref, tail_ref, cwq_ref, cwk_ref, cwv_ref, alog_ref, dtb_ref, nw_ref,
                     o_ref, pad_ref, qn_ref, kn_ref, vc_ref, mt_ref, bb_ref, qp_ref, op_ref, state_ref, oacc_ref):
    seq = q_ref.shape[0]
    nc = seq // DN_CHUNK
    c64 = DN_CHUNK
    dk = DN_DK

    pad_ref[0:CONV_HALO, :] = jnp.zeros((CONV_HALO, dk), F32)
    pad_ref[CONV_HALO + seq:, :] = jnp.zeros((CONV_HALO, dk), F32)

    def conv_into(src_ref, col, w_ref, dst_ref, l2_scale):
        pad_ref[CONV_HALO:CONV_HALO + seq, :] = src_ref[:, col:col + dk].astype(F32)
        w = w_ref[:, col:col + dk]

        def tile(r, carry):
            start = pl.multiple_of(r * CONV_TILE, CONV_TILE)
            win = pad_ref[pl.ds(start, CONV_TILE + 2 * CONV_HALO), :]
            y = jnp.zeros((CONV_TILE, dk), F32)
            for j in range(DN_CONV):
                lo = CONV_HALO + j - DN_CONV // 2
                y = y + win[lo:lo + CONV_TILE, :] * w[j:j + 1, :]
            y = _silu(y)
            if l2_scale is not None:
                y = y * (lax.rsqrt(jnp.sum(y * y, axis=1, keepdims=True) + EPS) * l2_scale)
            dst_ref[pl.ds(start, CONV_TILE), col:col + dk] = y.astype(dst_ref.dtype)
            return carry

        lax.fori_loop(0, seq // CONV_TILE, tile, 0)

    conv_into(q_ref, 0, cwq_ref, qn_ref, DN_DK ** -0.5)
    conv_into(k_ref, 0, cwk_ref, kn_ref, 1.0)
    for vl in range(DN_REP):
        conv_into(v_ref, vl * DN_DV, cwv_ref, vc_ref, None)

    rr = lax.broadcasted_iota(jnp.int32, (DN_ROWS, DN_ROWS), 0)
    cc = lax.broadcasted_iota(jnp.int32, (DN_ROWS, DN_ROWS), 1)
    same = (rr // c64) == (cc // c64)
    fwd_rows = rr < DN_REP * c64
    strict = same & ((fwd_rows & (rr > cc)) | (jnp.logical_not(fwd_rows) & (rr < cc)))
    eye = (rr == cc)
    eye_f = eye.astype(F32)
    r64 = lax.broadcasted_iota(jnp.int32, (c64, DN_ROWS), 0)
    c64i = lax.broadcasted_iota(jnp.int32, (c64, DN_ROWS), 1)
    eye_row = (r64 == (c64i % c64)).astype(F32)
    li = lax.broadcasted_iota(jnp.int32, (c64, c64), 0)
    lj = lax.broadcasted_iota(jnp.int32, (c64, c64), 1)
    tril = (lj <= li).astype(BF16)
    triu = (lj >= li).astype(BF16)
    lane8 = lax.broadcasted_iota(jnp.int32, (c64, 2 * DN_INST), 1)
    rb = lax.broadcasted_iota(jnp.int32, (DN_ROWS, dk), 0) // c64
    e_r = lax.broadcasted_iota(jnp.int32, (DN_INST * dk, dk), 0)
    e_c = lax.broadcasted_iota(jnp.int32, (DN_INST * dk, dk), 1)
    eye_tall = ((e_r % dk) == e_c).astype(F32)
    neg_a = -jnp.exp(alog_ref[0])
    dtb = dtb_ref[0]

    def stack(cols):
        return jnp.concatenate(cols, axis=0)

    def blockdiag(rows_):
        return jnp.where(same, jnp.concatenate([rows_] * DN_INST, axis=0), 0.0)

    def chunk_maps(c):
        rows = pl.ds(pl.multiple_of(c * c64, c64), c64)
        tl = tail_ref[0, 0, rows, :]
        beta8 = 1.0 / (1.0 + jnp.exp(-tl))
        g8 = neg_a * _softplus(tl + dtb)
        parts = _split3_bf16(g8)
        gcf = _dot(tril, parts[0]) + (_dot(tril, parts[1]) + _dot(tril, parts[2]))
        gcb = _dot(triu, parts[0]) + (_dot(triu, parts[1]) + _dot(triu, parts[2]))
        yield
        gc8 = jnp.where(lane8 < DN_INST + DN_REP, gcf, gcb)
        gtot8 = jnp.sum(g8, axis=0, keepdims=True)
        bcol = stack([beta8[:, i:i + 1] for i in range(DN_INST)])
        gcol = stack([gc8[:, DN_INST + i:DN_INST + i + 1] for i in range(DN_INST)])
        gtc = stack([jnp.broadcast_to(gtot8[:, DN_INST + i:DN_INST + i + 1], (c64, 1)) for i in range(DN_INST)])

        kb = kn_ref[rows, :]
        qb = qn_ref[rows, :]
        vb = vc_ref[rows, :]
        k4b = stack([kb] * DN_INST)
        q4b = stack([qb] * DN_INST)
        k4 = k4b.astype(F32)
        v4 = stack([vb[:, (i % DN_REP) * DN_DV:(i % DN_REP + 1) * DN_DV] for i in range(DN_INST)]).astype(F32)

        gmat = jnp.broadcast_to(gcol, (DN_ROWS, DN_ROWS))
        grow = jnp.sum(jnp.where(eye, gmat, 0.0), axis=0, keepdims=True)
        decay = jnp.exp(jnp.where(strict, gmat - grow, NEG_INF))
        kk_row = _dot_nt(kb, k4b)
        qk_row = _dot_nt(qb, k4b)
        yield
        n_bd = bcol * stack([kk_row] * DN_INST) * decay
        qk_bd = stack([qk_row] * DN_INST) * (decay + eye_f)

        r_pow = -(n_bd[0:c64] + n_bd[c64:2 * c64] + n_bd[2 * c64:3 * c64] + n_bd[3 * c64:4 * c64])
        t_row = eye_row + r_pow
        r_pow = _dot(r_pow.astype(BF16), (-n_bd).astype(BF16))
        yield
        for _ in range(4):
            p_bd = blockdiag(r_pow).astype(BF16)
            rt = _dot(jnp.concatenate([r_pow, t_row], axis=0).astype(BF16), p_bd)
            yield
            r_pow = rt[0:c64]
            t_row = t_row + rt[c64:]
        t_row = t_row + _dot(t_row.astype(BF16), blockdiag(r_pow).astype(BF16))
        yield

        egc = jnp.exp(gcol)
        rhs = jnp.concatenate([k4 * (bcol * egc), v4 * bcol], axis=1).astype(BF16)
        wu = _dot(blockdiag(t_row).astype(BF16), rhs).astype(BF16)
        yield
        qk_wu = _dot(qk_bd.astype(BF16), wu)
        kd = k4 * jnp.exp(gtc - gcol)
        kd_wide = jnp.concatenate([jnp.where(rb == i, kd, 0.0) for i in range(DN_INST)], axis=1).astype(BF16)
        kw = _dot_tn(kd_wide, wu)
        yield
        qp = (q4b.astype(F32) * egc - qk_wu[:, :dk]).astype(BF16)
        op = qk_wu[:, dk:].astype(BF16)
        dl = stack([jnp.broadcast_to(jnp.exp(gtot8[:, DN_INST + i:DN_INST + i + 1]), (dk, 1)) for i in range(DN_INST)])
        return qp, op, (dl * eye_tall - kw[:, :dk]).astype(BF16), kw[:, dk:].astype(BF16)

    def run_interleaved(gens):
        results = [None] * len(gens)
        while any(r is None for r in results):
            for u, gen in enumerate(gens):
                if results[u] is None:
                    try:
                        next(gen)
                    except StopIteration as done:
                        results[u] = done.value
        return results

    def prep(g, carry):
        cs = [g * DN_PREP_GROUP + u for u in range(DN_PREP_GROUP)]
        maps = run_interleaved([chunk_maps(c) for c in cs])
        for c, (qp, op, mt, bb) in zip(cs, maps):
            qp_ref[c] = qp
            op_ref[c] = op
            mt_ref[c] = mt
            bb_ref[c] = bb
        return carry

    lax.fori_loop(0, nc // DN_PREP_GROUP, prep, 0)

    state_ref[...] = jnp.zeros_like(state_ref)
    oacc_ref[...] = jnp.zeros_like(oacc_ref)

    def sweep(j, carry):
        for i in range(DN_INST):
            c = j if i < DN_REP else nc - 1 - j
            s_bf = state_ref[i].astype(BF16)
            new_s = _dot(mt_ref[c, i * dk:(i + 1) * dk, :], s_bf) + bb_ref[c, i * dk:(i + 1) * dk, :].astype(F32)
            o = _dot(qp_ref[c, i * c64:(i + 1) * c64, :], s_bf) + op_ref[c, i * c64:(i + 1) * c64, :].astype(F32)
            state_ref[i] = new_s
            rows = pl.ds(pl.multiple_of(c * c64, c64), c64)
            vcols = slice((i % DN_REP) * DN_DV, (i % DN_REP + 1) * DN_DV)
            oacc_ref[rows, vcols] = oacc_ref[rows, vcols] + o
        return carry

    lax.fori_loop(0, nc, sweep, 0)

    def finish(r, carry):
        rows = pl.ds(pl.multiple_of(r * CONV_TILE, CONV_TILE), CONV_TILE)
        for vl in range(DN_REP):
            vcols = slice(vl * DN_DV, (vl + 1) * DN_DV)
            o = _rms_rows(oacc_ref[rows, vcols]) * nw_ref[...]
            o_ref[rows, vcols] = (o * _silu(z_ref[rows, vcols].astype(F32))).astype(o_ref.dtype)
        return carry

    lax.fori_loop(0, seq // CONV_TILE, finish, 0)


def _deltanet_core(proj, tail8, conv_w, alog8, dtb8, norm_w, batch, seq):
    nc = seq // DN_CHUNK
    kq = DN_K_HEADS
    vw = DN_REP * DN_DV
    kv = 2 * DN_K_HEADS * DN_DK // vw
    kz = kv + DN_K_HEADS
    once = pl.Buffered(1)
    return pl.pallas_call(
        _deltanet_kernel, name="deltanet_core", grid=(batch, DN_K_HEADS),
        in_specs=[pl.BlockSpec((seq, DN_DK), lambda b, h: (b, h), pipeline_mode=once),
                  pl.BlockSpec((seq, DN_DK), lambda b, h: (b, kq + h), pipeline_mode=once),
                  pl.BlockSpec((seq, vw), lambda b, h: (b, kv + h), pipeline_mode=once),
                  pl.BlockSpec((seq, vw), lambda b, h: (b, kz + h), pipeline_mode=once),
                  pl.BlockSpec((1, 1, seq, 2 * DN_INST), lambda b, h: (b, h, 0, 0), pipeline_mode=once),
                  pl.BlockSpec((DN_CONV, DN_DK), lambda b, h: (0, h)),
                  pl.BlockSpec((DN_CONV, DN_DK), lambda b, h: (0, kq + h)),
                  pl.BlockSpec((DN_CONV, vw), lambda b, h: (0, kv + h)),
                  pl.BlockSpec((1, 1, 2 * DN_INST), lambda b, h: (h, 0, 0)),
                  pl.BlockSpec((1, 1, 2 * DN_INST), lambda b, h: (h, 0, 0)),
                  pl.BlockSpec((1, DN_DV), lambda b, h: (0, 0))],
        out_specs=pl.BlockSpec((seq, vw), lambda b, h: (b, h)),
        out_shape=jax.ShapeDtypeStruct((batch * seq, DN_V_HEADS * DN_DV), BF16),
        scratch_shapes=[pltpu.VMEM((seq + 2 * CONV_HALO, DN_DK), F32),
                        pltpu.VMEM((seq, DN_DK), BF16), pltpu.VMEM((seq, DN_DK), BF16), pltpu.VMEM((seq, vw), BF16),
                        pltpu.VMEM((nc, DN_INST * DN_DK, DN_DK), BF16), pltpu.VMEM((nc, DN_INST * DN_DK, DN_DV), BF16),
                        pltpu.VMEM((nc, DN_ROWS, DN_DK), BF16), pltpu.VMEM((nc, DN_ROWS, DN_DV), BF16),
                        pltpu.VMEM((DN_INST, DN_DK, DN_DV), F32), pltpu.VMEM((seq, vw), F32)],
        compiler_params=_cparams("parallel", "parallel"),
    )(proj, proj, proj, proj, tail8, conv_w, conv_w, conv_w, alog8, dtb8, norm_w.reshape(1, DN_DV))


def _per_key_head(a):
    lead = a.shape[:-2]
    a = a.reshape(lead + (2, DN_K_HEADS, DN_REP))
    a = jnp.moveaxis(a, -2, 0)
    return a.reshape((DN_K_HEADS,) + lead + (DN_INST,))


def _deltanet_mixer(x2, nw, w_in, conv_w, a_log, dt_bias, norm_w, w_out, batch, seq):
    conv_dim = 2 * DN_K_HEADS * DN_DK + DN_V_HEADS * DN_DV
    main = conv_dim + DN_V_HEADS * DN_DV
    proj = _norm_proj(x2, nw, w_in, n_out=main, tn=512, out_dtype=BF16)
    tail = _norm_proj(x2, nw, w_in, n_out=LANES, tn=LANES, out_dtype=F32, col_off=main)
    tail = tail.reshape(batch, seq, 2, 2, DN_V_HEADS)
    tail8 = jnp.concatenate([_per_key_head(tail[:, :, 0]), _per_key_head(tail[:, :, 1])], axis=-1)
    tail8 = jnp.moveaxis(tail8, 0, 1)
    zeros = jnp.zeros((DN_K_HEADS, DN_INST), F32)
    alog8 = jnp.concatenate([zeros, _per_key_head(a_log)], axis=-1).reshape(DN_K_HEADS, 1, 2 * DN_INST)
    dtb8 = jnp.concatenate([zeros, _per_key_head(dt_bias)], axis=-1).reshape(DN_K_HEADS, 1, 2 * DN_INST)
    o = _deltanet_core(proj, tail8, conv_w, alog8, dtb8, norm_w, batch, seq)
    return _out_proj(o, w_out, x2)


def kernel(x, norm_mix_w, norm_ffn_w, final_norm_w, rel_bias_table, ret_w_in, ret_w_out, dil_w_in, dil_w_out,
           dn_w_in, dn_conv_w, dn_a_log, dn_dt_bias, dn_norm_w, dn_w_out, moe_w_router, moe_w_gate, moe_w_up,
           moe_w_down):
    batch, seq, d = x.shape
    x2 = x.reshape(batch * seq, d)
    for i in range(norm_mix_w.shape[0]):
        j = i // N_MIXERS
        kind = i % N_MIXERS
        if kind == 0:
            x2 = _retention_mixer(x2, norm_mix_w[i], (ret_w_in, j), (ret_w_out, j), batch, seq)
        elif kind == 1:
            x2 = _dilated_mixer(x2, norm_mix_w[i], (dil_w_in, j), (dil_w_out, j), rel_bias_table, batch, seq)
        else:
            x2 = _deltanet_mixer(x2, norm_mix_w[i], (dn_w_in, j), dn_conv_w[j], dn_a_log[j], dn_dt_bias[j],
                                 dn_norm_w[j], (dn_w_out, j), batch, seq)
        x2 = _moe(x2, norm_ffn_w[i], moe_w_router[i], moe_w_gate, moe_w_up, moe_w_down, i, batch, seq)
    return _final_norm(x2, final_norm_w).reshape(batch, seq, d)
```

```python
import functools
import math

import jax
import jax.numpy as jnp
from jax import lax
from jax.experimental import pallas as pl
from jax.experimental.pallas import tpu as pltpu

F32 = jnp.float32
BF16 = jnp.bfloat16

D_MODEL = 2048
EPS = 1e-6
NEG_INF = -1e30
RET_HEADS = 8
RET_DK = D_MODEL // RET_HEADS
RET_DV = 2 * RET_DK
ROPE_BASE = 10000.0
RET_BWD_DECAY_OFFSET = 0.5
DIL_PATTERNS = ((128, 1), (512, 4), (2048, 16))
DIL_HEADS = 16
DIL_DH = D_MODEL // DIL_HEADS
REL_BUCKETS = 32
REL_MAX_DIST = 1024
DN_K_HEADS = 16
DN_V_HEADS = 32
DN_DK = 128
DN_DV = 128
DN_CONV = 5
N_EXPERTS = 16
EXPERT_FF = D_MODEL // 2
CAPACITY_FACTOR = 2
N_MIXERS = 3

LANES = 128
VMEM_LIMIT_BYTES = 56 * 1024 * 1024
ROW_TILE = 1024
PROJ_COL_TILE = 1024
RET_CHUNK = 256
RET_HEAD_GROUP = 2
MOE_FF_TILE = 256
MOE_DMA_UNROLL = 8


def _cparams(*sem):
    return pltpu.CompilerParams(dimension_semantics=sem, vmem_limit_bytes=VMEM_LIMIT_BYTES)


def _dot(a, b):
    return jnp.dot(a, b, preferred_element_type=F32)


def _dot_nt(a, b):
    return lax.dot_general(a, b, (((1,), (1,)), ((), ())), preferred_element_type=F32)


def _dot_tn(a, b):
    return lax.dot_general(a, b, (((0,), (0,)), ((), ())), preferred_element_type=F32)


def _silu(x):
    return x / (1.0 + jnp.exp(-x))


def _rms_rows(x):
    return x * lax.rsqrt(jnp.mean(x * x, axis=-1, keepdims=True) + EPS)


def _norm_proj_kernel(x_ref, nw_ref, w_ref, o_ref, hn_ref):
    @pl.when(pl.program_id(1) == 0)
    def _():
        hn_ref[...] = (_rms_rows(x_ref[...]) * nw_ref[...]).astype(BF16)

    o_ref[...] = _dot(hn_ref[...], w_ref[...].astype(BF16)).astype(o_ref.dtype)


def _norm_proj_rope_kernel(x_ref, nw_ref, w_ref, cos_ref, sin_ref, o_ref, hn_ref, *, n_q_tiles, n_rope_tiles, k_scale):
    j = pl.program_id(1)

    @pl.when(j == 0)
    def _():
        hn_ref[...] = (_rms_rows(x_ref[...]) * nw_ref[...]).astype(BF16)

    acc = _dot(hn_ref[...], w_ref[...].astype(BF16))

    @pl.when(j < n_rope_tiles)
    def _():
        c = cos_ref[...]
        s = sin_ref[...]
        half = c.shape[1]
        scale = jnp.where(j >= n_q_tiles, k_scale, 1.0).astype(F32)
        for hh in range(acc.shape[1] // (2 * half)):
            x1 = acc[:, 2 * hh * half:(2 * hh + 1) * half]
            x2 = acc[:, (2 * hh + 1) * half:(2 * hh + 2) * half]
            o_ref[:, 2 * hh * half:(2 * hh + 1) * half] = ((x1 * c - x2 * s) * scale).astype(o_ref.dtype)
            o_ref[:, (2 * hh + 1) * half:(2 * hh + 2) * half] = ((x2 * c + x1 * s) * scale).astype(o_ref.dtype)

    @pl.when(j >= n_rope_tiles)
    def _():
        o_ref[...] = acc.astype(o_ref.dtype)


def _norm_proj(x2, nw, w, *, n_out, tn, out_dtype, col_off=0, rope=None):
    m, k = x2.shape
    w_stack, layer = w
    tm = min(ROW_TILE, m)
    grid = (m // tm, n_out // tn)
    off = col_off // tn
    in_specs = [pl.BlockSpec((tm, k), lambda i, j: (i, 0)),
                pl.BlockSpec((1, k), lambda i, j: (0, 0)),
                pl.BlockSpec((None, k, tn), lambda i, j: (layer, 0, j + off))]
    args = [x2, nw.reshape(1, k), w_stack]
    if rope is None:
        body = _norm_proj_kernel
    else:
        cos, sin, seq, n_q_cols, n_rope_cols, k_scale = rope
        nsb = seq // tm
        in_specs += [pl.BlockSpec((tm, cos.shape[1]), lambda i, j: (i % nsb, 0)),
                     pl.BlockSpec((tm, cos.shape[1]), lambda i, j: (i % nsb, 0))]
        args += [cos, sin]
        body = functools.partial(_norm_proj_rope_kernel, n_q_tiles=n_q_cols // tn, n_rope_tiles=n_rope_cols // tn,
                                 k_scale=k_scale)
    return pl.pallas_call(
        body, name="norm_proj", grid=grid, in_specs=in_specs,
        out_specs=pl.BlockSpec((tm, tn), lambda i, j: (i, j)),
        out_shape=jax.ShapeDtypeStruct((m, n_out), out_dtype),
        scratch_shapes=[pltpu.VMEM((tm, k), BF16)],
        compiler_params=_cparams("parallel", "arbitrary"),
    )(*args)


def _out_proj_kernel(a_ref, w_ref, res_ref, o_ref):
    o_ref[...] = res_ref[...] + _dot(a_ref[...], w_ref[...].astype(BF16))


def _out_proj(a, w, res, *, tn=512):
    m, k = a.shape
    w_stack, layer = w
    n = w_stack.shape[2]
    tm = min(ROW_TILE, m)
    return pl.pallas_call(
        _out_proj_kernel, name="out_proj", grid=(m // tm, n // tn),
        in_specs=[pl.BlockSpec((tm, k), lambda i, j: (i, 0)),
                  pl.BlockSpec((None, k, tn), lambda i, j: (layer, 0, j)),
                  pl.BlockSpec((tm, tn), lambda i, j: (i, j))],
        out_specs=pl.BlockSpec((tm, tn), lambda i, j: (i, j)),
        out_shape=jax.ShapeDtypeStruct((m, n), F32),
        compiler_params=_cparams("parallel", "arbitrary"),
    )(a, w_stack, res)


def _final_norm_kernel(x_ref, nw_ref, o_ref):
    o_ref[...] = _rms_rows(x_ref[...]) * nw_ref[...]


def _final_norm(x2, nw):
    m, k = x2.shape
    tm = min(ROW_TILE, m)
    return pl.pallas_call(
        _final_norm_kernel, name="final_norm", grid=(m // tm,),
        in_specs=[pl.BlockSpec((tm, k), lambda i: (i, 0)), pl.BlockSpec((1, k), lambda i: (0, 0))],
        out_specs=pl.BlockSpec((tm, k), lambda i: (i, 0)),
        out_shape=jax.ShapeDtypeStruct((m, k), F32),
        compiler_params=_cparams("parallel"),
    )(x2, nw.reshape(1, k))


def _retention_kernel(q_ref, k_ref, v_ref, g_ref, dmat_ref, vec_ref, o_ref, state_ref, oacc_ref):
    sweep = pl.program_id(2)
    c = pl.program_id(3)
    nc = pl.num_programs(3)
    hg = range(RET_HEAD_GROUP)
    qs = [slice(h * RET_DK, (h + 1) * RET_DK) for h in hg]
    vs = [slice(h * RET_DV, (h + 1) * RET_DV) for h in hg]

    @pl.when(c == 0)
    def _():
        state_ref[...] = jnp.zeros_like(state_ref)

    @pl.when(sweep == 0)
    def _():
        scores = [_dot_nt(q_ref[:, qs[h]], k_ref[:, qs[h]]) for h in hg]
        inters = [_dot(q_ref[:, qs[h]], state_ref[h].astype(BF16)) for h in hg]
        kds = [(k_ref[:, qs[h]].astype(F32) * vec_ref[h][:, 2:3]).astype(BF16) for h in hg]
        upds = [_dot_tn(kds[h], v_ref[:, vs[h]]) for h in hg]
        probs = [(scores[h] * dmat_ref[h]).astype(BF16) for h in hg]
        intras = [_dot(probs[h], v_ref[:, vs[h]]) for h in hg]
        for h in hg:
            vec = vec_ref[h]
            oacc_ref[c, :, vs[h]] = intras[h] + inters[h] * vec[:, 0:1]
            state_ref[h] = state_ref[h] * vec[0:1, 4:5] + upds[h]

    @pl.when(sweep == 1)
    def _():
        inters = [_dot(q_ref[:, qs[h]], state_ref[h].astype(BF16)) for h in hg]
        kds = [(k_ref[:, qs[h]].astype(F32) * vec_ref[h][:, 3:4]).astype(BF16) for h in hg]
        upds = [_dot_tn(kds[h], v_ref[:, vs[h]]) for h in hg]
        for h in hg:
            vec = vec_ref[h]
            o = _rms_rows(oacc_ref[nc - 1 - c, :, vs[h]] + inters[h] * vec[:, 1:2])
            o_ref[:, vs[h]] = (o * _silu(g_ref[:, vs[h]].astype(F32))).astype(o_ref.dtype)
            state_ref[h] = state_ref[h] * vec[0:1, 5:6] + upds[h]


def _retention_tables(chunk):
    hh = jnp.arange(RET_HEADS, dtype=F32)
    lg_f = jnp.log1p(-jnp.exp2(-5.0 - hh))
    lg_b = jnp.log1p(-jnp.exp2(-(5.0 + RET_BWD_DECAY_OFFSET) - hh))
    t = jnp.arange(chunk, dtype=F32)
    diff = t[:, None] - t[None, :]
    dmat = jnp.where(diff[None] >= 0,
                     jnp.exp(jnp.maximum(diff, 0.0)[None] * lg_f[:, None, None]),
                     jnp.exp(jnp.maximum(-diff, 0.0)[None] * lg_b[:, None, None]))
    cols = [jnp.exp((t[None, :] + 1.0) * lg_f[:, None]),
            jnp.exp((chunk - t)[None, :] * lg_b[:, None]),
            jnp.exp((chunk - 1.0 - t)[None, :] * lg_f[:, None]),
            jnp.exp(t[None, :] * lg_b[:, None]),
            jnp.broadcast_to(jnp.exp(chunk * lg_f)[:, None], (RET_HEADS, chunk)),
            jnp.broadcast_to(jnp.exp(chunk * lg_b)[:, None], (RET_HEADS, chunk))]
    cols += [jnp.zeros((RET_HEADS, chunk), F32)] * 2
    return dmat, jnp.stack(cols, axis=-1)


def _retention_core(proj, batch, seq):
    cq = min(RET_CHUNK, seq)
    nc = seq // cq
    dmat, vec = _retention_tables(cq)
    g = RET_HEAD_GROUP
    n_hg = RET_HEADS // g
    kq = n_hg
    kv = 2 * RET_HEADS * RET_DK // (g * RET_DV)
    kg = kv + n_hg

    def row(b, s, c):
        return b * nc + c + s * (nc - 1 - 2 * c)

    def row_out(b, s, c):
        return b * nc + nc - 1 - c * s

    return pl.pallas_call(
        _retention_kernel, name="retention_core", grid=(batch, n_hg, 2, nc),
        in_specs=[pl.BlockSpec((cq, g * RET_DK), lambda b, h, s, c: (row(b, s, c), h)),
                  pl.BlockSpec((cq, g * RET_DK), lambda b, h, s, c: (row(b, s, c), kq + h)),
                  pl.BlockSpec((cq, g * RET_DV), lambda b, h, s, c: (row(b, s, c), kv + h)),
                  pl.BlockSpec((cq, g * RET_DV), lambda b, h, s, c: (row_out(b, s, c), kg + h)),
                  pl.BlockSpec((g, cq, cq), lambda b, h, s, c: (h, 0, 0)),
                  pl.BlockSpec((g, cq, 8), lambda b, h, s, c: (h, 0, 0))],
        out_specs=pl.BlockSpec((cq, g * RET_DV), lambda b, h, s, c: (row_out(b, s, c), h)),
        out_shape=jax.ShapeDtypeStruct((batch * seq, RET_HEADS * RET_DV), BF16),
        scratch_shapes=[pltpu.VMEM((g, RET_DK, RET_DV), F32), pltpu.VMEM((nc, cq, g * RET_DV), F32)],
        compiler_params=_cparams("parallel", "parallel", "arbitrary", "arbitrary"),
    )(proj, proj, proj, proj, dmat, vec)


def _rope_tables(seq, half):
    inv = ROPE_BASE ** (-jnp.arange(half, dtype=F32) / half)
    ang = jnp.arange(seq, dtype=F32)[:, None] * inv[None, :]
    return jnp.cos(ang), jnp.sin(ang)


def _retention_mixer(x2, nw, w_in, w_out, batch, seq):
    cos, sin = _rope_tables(seq, RET_DK // 2)
    qk_cols = 2 * RET_HEADS * RET_DK
    proj = _norm_proj(x2, nw, w_in, n_out=w_in[0].shape[2], tn=PROJ_COL_TILE, out_dtype=BF16,
                      rope=(cos, sin, seq, qk_cols // 2, qk_cols, RET_DK ** -0.5))
    o = _retention_core(proj, batch, seq)
    return _out_proj(o, w_out, x2)


def _split_bf16(x):
    hi = x.astype(BF16)
    return hi, (x - hi.astype(F32)).astype(BF16)


def _router_kernel(x_ref, nw_ref, wr_ref, wrt_ref, hx_ref, afft_ref):
    half = x_ref.shape[1] // 2
    h = _rms_rows(x_ref[...]) * nw_ref[...]
    h_hi, h_lo = _split_bf16(h)
    bits = lax.bitcast_convert_type(h_hi.astype(F32), jnp.uint32)
    hx_ref[:, :half] = (bits[:, half:] & jnp.uint32(0xFFFF0000)) | (bits[:, :half] >> 16)
    w_hi, w_lo = _split_bf16(wr_ref[...])
    lg = _dot(h_hi, w_hi) + (_dot(h_lo, w_hi) + _dot(h_hi, w_lo))
    lane = lax.broadcasted_iota(jnp.int32, lg.shape, 1)
    lg = jnp.where(lane < N_EXPERTS, lg, NEG_INF)
    e = jnp.exp(lg - jnp.max(lg, axis=1, keepdims=True))
    hx_ref[:, half:] = lax.bitcast_convert_type(e / jnp.sum(e, axis=1, keepdims=True), jnp.uint32)
    wt_hi, wt_lo = _split_bf16(wrt_ref[...])
    lgt = _dot_nt(wt_hi, h_hi) + (_dot_nt(wt_hi, h_lo) + _dot_nt(wt_lo, h_hi))
    et = jnp.exp(lgt - jnp.max(lgt, axis=0, keepdims=True))
    afft_ref[0] = et / jnp.sum(et, axis=0, keepdims=True)


def _router(x2, nw, w_router, batch, seq):
    m, d = x2.shape
    tm = min(512, seq)
    nsb = seq // tm
    hw = d // 2 + LANES
    wr = jnp.pad(w_router, ((0, 0), (0, LANES - N_EXPERTS)))
    return pl.pallas_call(
        _router_kernel, name="moe_router", grid=(m // tm,),
        in_specs=[pl.BlockSpec((tm, d), lambda i: (i, 0)),
                  pl.BlockSpec((1, d), lambda i: (0, 0)),
                  pl.BlockSpec((d, LANES), lambda i: (0, 0)),
                  pl.BlockSpec((N_EXPERTS, d), lambda i: (0, 0))],
        out_specs=[pl.BlockSpec((tm, hw), lambda i: (i, 0)),
                   pl.BlockSpec((1, N_EXPERTS, tm), lambda i: (i // nsb, 0, i % nsb))],
        out_shape=[jax.ShapeDtypeStruct((m, hw), jnp.uint32),
                   jax.ShapeDtypeStruct((batch, N_EXPERTS, seq), F32)],
        compiler_params=_cparams("parallel"),
    )(x2, nw.reshape(1, d), wr, w_router.T)


def _topk_kernel(aff_ref, idx_ref, loc_ref, off_ref, end_ref, *, cap, n_groups, n_blk):
    n = n_groups * n_blk
    bits = lax.bitcast_convert_type(aff_ref[...], jnp.int32)
    ri = lax.broadcasted_iota(jnp.int32, (n, n), 0)
    ci = lax.broadcasted_iota(jnp.int32, (n, n), 1)
    same = (ri // n_blk) == (ci // n_blk)
    grp_ones = same.astype(BF16)
    grp_before = (same & (ci < ri)).astype(BF16)
    li = lax.broadcasted_iota(jnp.int32, (LANES, LANES), 0)
    lj = lax.broadcasted_iota(jnp.int32, (LANES, LANES), 1)
    incl = (li <= lj).astype(BF16)
    ones = jnp.ones((LANES, LANES), BF16)

    def row_total(mask):
        return _dot(mask.astype(BF16), ones)

    def group_count(mask):
        return _dot(grp_ones, row_total(mask).astype(BF16))

    def search(i, tau):
        cand = tau | jnp.left_shift(jnp.int32(1), 30 - i)
        return jnp.where(group_count(bits >= cand) >= cap, cand, tau)

    tau = lax.fori_loop(0, 31, search, jnp.zeros((n, LANES), jnp.int32))

    def cumsum(mask):
        mb = mask.astype(BF16)
        tot = _dot(mb, ones)
        return _dot(mb, incl), _dot(grp_before, tot.astype(BF16)), tot

    gt = bits > tau
    eq = bits == tau
    need = cap - group_count(gt)
    eq_loc, eq_off, _ = cumsum(eq)
    sel = gt | (eq & (eq_loc + eq_off <= need))
    loc, off, tot = cumsum(sel)
    loc_ref[...] = loc
    off_ref[...] = off
    end_ref[...] = off + tot

    slot = lax.broadcasted_iota(jnp.int32, (cap, LANES), 0).astype(F32)
    lane = lax.broadcasted_iota(jnp.int32, (cap, LANES), 1)
    eye = lax.broadcasted_iota(jnp.int32, (n_blk, LANES), 0) == lax.broadcasted_iota(jnp.int32, (n_blk, LANES), 1)
    pad = jnp.zeros((LANES - n_blk, LANES), BF16)

    def compact(g, acc):
        rows = pl.ds(pl.multiple_of(g * n_blk, n_blk), n_blk)
        end_row = jnp.sum(jnp.where(eye, end_ref[rows, :], 0.0), axis=0, keepdims=True)
        off_row = jnp.sum(jnp.where(eye, off_ref[rows, :], 0.0), axis=0, keepdims=True)
        blk = jnp.sum(((end_row <= slot) & (lane < n_blk)).astype(F32), axis=1, keepdims=True)
        onehot = lane == blk.astype(jnp.int32)
        loc_pad = jnp.concatenate([loc_ref[rows, :].astype(BF16), pad], axis=0)
        in_blk = _dot(onehot.astype(BF16), loc_pad)
        rank = slot - jnp.sum(jnp.where(onehot, off_row, 0.0), axis=1, keepdims=True)
        pos = blk * LANES + jnp.sum((in_blk <= rank).astype(F32), axis=1, keepdims=True)
        return jnp.where(lane == g, pos.astype(jnp.int32), acc)

    idx_ref[...] = lax.fori_loop(0, n_groups, compact, jnp.zeros((cap, LANES), jnp.int32))


def _topk(afft, cap):
    batch, n_e, seq = afft.shape
    n_groups = batch * n_e
    n_blk = seq // LANES
    n = n_groups * n_blk
    assert n_groups <= LANES and n_blk <= LANES
    idx = pl.pallas_call(
        functools.partial(_topk_kernel, cap=cap, n_groups=n_groups, n_blk=n_blk), name="moe_topk",
        out_shape=jax.ShapeDtypeStruct((cap, LANES), jnp.int32),
        scratch_shapes=[pltpu.VMEM((n, LANES), F32)] * 3,
        compiler_params=pltpu.CompilerParams(vmem_limit_bytes=VMEM_LIMIT_BYTES),
    )(afft.reshape(n, LANES))
    return idx[:, :n_groups].T.reshape(batch, n_e, cap)


def _moe_ffn_kernel(idx_ref, hx_hbm, wg_ref, wu_ref, wd_ref, xres_hbm, out_hbm,
                    hbuf, xbf, acc, rbuf, gate, sem_h, sem_r, sem_s):
    del xres_hbm
    e = pl.program_id(0)
    f = pl.program_id(1)
    n_e = pl.num_programs(0)
    nf = pl.num_programs(1)
    rows = xbf.shape[0]
    d = xbf.shape[1]
    per_step = rows // nf
    slot = e % 2
    base = e * rows
    nxt = ((e + 1) % n_e) * rows

    def h_copy(row_id, fq, u, s):
        return pltpu.make_async_copy(hx_hbm.at[pl.ds(row_id, 1), :], hbuf.at[s, fq, pl.ds(u, 1), :], sem_h.at[s])

    def wait_all(buf, sem):
        pltpu.make_async_copy(buf, buf, sem).wait()

    @pl.when((e == 0) & (f == 0))
    def _():
        def start(r, carry):
            h_copy(idx_ref[r], r // per_step, r % per_step, 0).start()
            return carry

        lax.fori_loop(0, rows, start, 0, unroll=MOE_DMA_UNROLL)

    @pl.when(f == 0)
    def _():
        wait_all(hbuf.at[slot], sem_h.at[slot])
        half = d // 2
        packed = hbuf[slot].reshape(rows, half + LANES)
        words = packed[:, :half]
        xbf[:, :half] = lax.bitcast_convert_type(words << 16, F32).astype(BF16)
        xbf[:, half:] = lax.bitcast_convert_type(words & jnp.uint32(0xFFFF0000), F32).astype(BF16)
        lane = lax.broadcasted_iota(jnp.int32, (rows, LANES), 1)
        aff = lax.bitcast_convert_type(packed[:, half:], F32)
        gate[...] = jnp.sum(jnp.where(lane == e, aff, 0.0), axis=1, keepdims=True)

    for u in range(per_step):
        r = f * per_step + u
        h_copy(idx_ref[nxt + r], f, u, 1 - slot).start()
        pltpu.make_async_copy(out_hbm.at[pl.ds(idx_ref[base + r], 1), :], rbuf.at[f, pl.ds(u, 1), :],
                              sem_r.at[0]).start()

    x = xbf[...]
    hid = (_silu(_dot(x, wg_ref[...].astype(BF16))) * _dot(x, wu_ref[...].astype(BF16))).astype(BF16)
    part = _dot(hid, wd_ref[...].astype(BF16))

    @pl.when(f == 0)
    def _():
        acc[...] = part

    @pl.when(f > 0)
    def _():
        acc[...] += part

    @pl.when(f == nf - 1)
    def _():
        wait_all(rbuf, sem_r.at[0])
        rbuf[...] = rbuf[...] + (acc[...] * gate[...]).reshape(rbuf.shape)

        for fq in range(rbuf.shape[0]):
            def start(u, carry, fq=fq):
                pltpu.make_async_copy(rbuf.at[fq, pl.ds(u, 1), :],
                                      out_hbm.at[pl.ds(idx_ref[base + fq * per_step + u], 1), :], sem_s.at[0]).start()
                return carry

            lax.fori_loop(0, per_step, start, 0, unroll=MOE_DMA_UNROLL)
        wait_all(rbuf, sem_s.at[0])

    @pl.when((e == n_e - 1) & (f == nf - 1))
    def _():
        wait_all(hbuf.at[1 - slot], sem_h.at[1 - slot])


def _moe_ffn(rows_idx, hx, x2, w_gate, w_up, w_down, layer):
    m, d = x2.shape
    _, n_e, _, ff = w_gate.shape
    rows = rows_idx.shape[0] // n_e
    tf = min(MOE_FF_TILE, ff)
    nf = ff // tf
    grid_spec = pltpu.PrefetchScalarGridSpec(
        num_scalar_prefetch=1, grid=(n_e, nf),
        in_specs=[pl.BlockSpec(memory_space=pl.ANY),
                  pl.BlockSpec((None, None, d, tf), lambda e, f, idx: (layer, e, 0, f)),
                  pl.BlockSpec((None, None, d, tf), lambda e, f, idx: (layer, e, 0, f)),
                  pl.BlockSpec((None, None, tf, d), lambda e, f, idx: (layer, e, f, 0)),
                  pl.BlockSpec(memory_space=pl.ANY)],
        out_specs=pl.BlockSpec(memory_space=pl.ANY),
        scratch_shapes=[pltpu.VMEM((2, nf, rows // nf, hx.shape[1]), jnp.uint32), pltpu.VMEM((rows, d), BF16),
                        pltpu.VMEM((rows, d), F32), pltpu.VMEM((nf, rows // nf, d), F32),
                        pltpu.VMEM((rows, 1), F32),
                        pltpu.SemaphoreType.DMA((2,)), pltpu.SemaphoreType.DMA((1,)),
                        pltpu.SemaphoreType.DMA((1,))])
    return pl.pallas_call(
        _moe_ffn_kernel, name="moe_ffn", grid_spec=grid_spec,
        out_shape=jax.ShapeDtypeStruct((m, d), F32),
        input_output_aliases={5: 0},
        compiler_params=_cparams("arbitrary", "arbitrary"),
    )(rows_idx, hx, w_gate, w_up, w_down, x2)


def _moe(x2, nw, w_router, w_gate, w_up, w_down, layer, batch, seq):
    cap = CAPACITY_FACTOR * seq // N_EXPERTS
    hx, afft = _router(x2, nw, w_router, batch, seq)
    idx = _topk(afft, cap)
    rows_idx = idx + (jnp.arange(batch, dtype=jnp.int32) * seq)[:, None, None]
    rows_idx = jnp.transpose(rows_idx, (1, 0, 2)).reshape(-1)
    return _moe_ffn(rows_idx, hx, x2, w_gate, w_up, w_down, layer)


DIL_BLOCK = 64


def _t5_bucket(rel):
    nb = REL_BUCKETS // 2
    max_exact = nb // 2
    ret = jnp.where(rel > 0, nb, 0)
    n = jnp.abs(rel)
    nf = jnp.maximum(n, 1).astype(F32)
    large = max_exact + (jnp.log(nf / max_exact) / math.log(REL_MAX_DIST / max_exact)
                         * (nb - max_exact)).astype(jnp.int32)
    large = jnp.minimum(large, nb - 1)
    return ret + jnp.where(n < max_exact, n, large)


def _dil_bias_kernel(table_ref, bucket_ref, o_ref):
    col = pl.program_id(0)
    bkt = bucket_ref[0]
    acc = jnp.zeros(bkt.shape, F32)
    for b in range(REL_BUCKETS):
        acc = jnp.where(bkt == b, table_ref[b, col], acc)
    s = lax.broadcasted_iota(jnp.int32, bkt.shape, 0)
    t = lax.broadcasted_iota(jnp.int32, bkt.shape, 1)
    o_ref[0] = jnp.where((t >= s) & (t <= s + 2 * DIL_BLOCK), acc, NEG_INF)


def _dil_bias(rel_table):
    qb = DIL_BLOCK
    rel_steps = jnp.arange(3 * qb)[None, :] - jnp.arange(qb)[:, None] - qb
    buckets = jnp.stack([_t5_bucket(rel_steps * dil) for _, dil in DIL_PATTERNS]).astype(jnp.int32)
    n_col = rel_table.shape[1]
    return pl.pallas_call(
        _dil_bias_kernel, name="dil_bias", grid=(n_col,),
        in_specs=[pl.BlockSpec(memory_space=pltpu.SMEM),
                  pl.BlockSpec((1, qb, 3 * qb), lambda c: (c // DIL_HEADS, 0, 0))],
        out_specs=pl.BlockSpec((1, qb, 3 * qb), lambda c: (c, 0, 0)),
        out_shape=jax.ShapeDtypeStruct((n_col, qb, 3 * qb), F32),
        compiler_params=_cparams("parallel"),
    )(rel_table, buckets)


def _dil_attn_kernel(q_ref, kp_ref, kc_ref, kn_ref, vp_ref, vc_ref, vn_ref, bias_ref, o_ref, lse_ref):
    jb = pl.program_id(2)
    nb = pl.num_programs(2)
    qb = q_ref.shape[0]
    t = lax.broadcasted_iota(jnp.int32, (qb, 3 * qb), 1)
    in_seq = ((jb > 0) | (t >= qb)) & ((jb < nb - 1) | (t < 2 * qb))
    kw = jnp.concatenate([kp_ref[...], kc_ref[...], kn_ref[...]], axis=0)
    vw = jnp.concatenate([vp_ref[...], vc_ref[...], vn_ref[...]], axis=0)
    lane = lax.broadcasted_iota(jnp.int32, (qb, LANES), 1)
    lse = jnp.zeros((qb, LANES), F32)
    heads = [slice(h * DIL_DH, (h + 1) * DIL_DH) for h in range(DIL_HEADS)]
    scores = [_dot_nt(q_ref[:, hs], kw[:, hs]) for hs in heads]
    probs, dens = [], []
    for h, s in enumerate(scores):
        s = jnp.where(in_seq, s * (DIL_DH ** -0.5) + bias_ref[h], NEG_INF)
        m = jnp.max(s, axis=1, keepdims=True)
        p = jnp.exp(s - m)
        den = jnp.sum(p, axis=1, keepdims=True)
        probs.append(p.astype(BF16))
        dens.append(den)
        lse = jnp.where(lane == h, m + jnp.log(den), lse)
    outs = [_dot(p, vw[:, hs]) for p, hs in zip(probs, heads)]
    for hs, o, den in zip(heads, outs, dens):
        o_ref[:, hs] = (o / den).astype(o_ref.dtype)
    lse_ref[0, 0, 0] = lse[:, :DIL_HEADS]


def _dil_group(proj, bias, gi, dil, batch, seq):
    qb = DIL_BLOCK
    l = seq // dil
    nb = l // qb
    hw = DIL_HEADS * DIL_DH
    n_grp = proj.shape[1] // hw
    p2 = proj.reshape(batch * l, dil * proj.shape[1])
    cq = 0

    def spec(which, shift):
        def imap(b, r, j):
            jj = jnp.clip(j + shift, 0, nb - 1)
            return (b * nb + jj, r * n_grp + cq + which)
        return pl.BlockSpec((qb, hw), imap)

    o, lse = pl.pallas_call(
        _dil_attn_kernel, name=f"dil_attn_d{dil}", grid=(batch, dil, nb),
        in_specs=[spec(0, 0), spec(1, -1), spec(1, 0), spec(1, 1), spec(2, -1), spec(2, 0), spec(2, 1),
                  pl.BlockSpec((DIL_HEADS, qb, 3 * qb), lambda b, r, j: (gi, 0, 0))],
        out_specs=[pl.BlockSpec((qb, hw), lambda b, r, j: (b * nb + j, r)),
                   pl.BlockSpec((1, 1, 1, qb, DIL_HEADS), lambda b, r, j: (b, j, r, 0, 0))],
        out_shape=[jax.ShapeDtypeStruct((batch * l, dil * hw), BF16),
                   jax.ShapeDtypeStruct((batch, nb, dil, qb, DIL_HEADS), F32)],
        compiler_params=_cparams("parallel", "parallel", "arbitrary"),
    )(p2, p2, p2, p2, p2, p2, p2, bias)
    lse = jnp.transpose(lse, (0, 1, 3, 2, 4)).reshape(batch * seq, DIL_HEADS)
    return o.reshape(batch * seq, hw), lse


def _dil_out_kernel(o0_ref, o1_ref, o2_ref, l0_ref, l1_ref, l2_ref, w_ref, res_ref, out_ref, comb_ref):
    @pl.when(pl.program_id(1) == 0)
    def _():
        l0, l1, l2 = l0_ref[...], l1_ref[...], l2_ref[...]
        m = jnp.maximum(jnp.maximum(l0, l1), l2)
        e0, e1, e2 = jnp.exp(l0 - m), jnp.exp(l1 - m), jnp.exp(l2 - m)
        den = e0 + e1 + e2
        w0, w1, w2 = e0 / den, e1 / den, e2 / den
        for h in range(DIL_HEADS):
            hs = slice(h * DIL_DH, (h + 1) * DIL_DH)
            comb_ref[:, hs] = (w0[:, h:h + 1] * o0_ref[:, hs].astype(F32) + w1[:, h:h + 1] * o1_ref[:, hs].astype(F32)
                               + w2[:, h:h + 1] * o2_ref[:, hs].astype(F32)).astype(BF16)

    out_ref[...] = res_ref[...] + _dot(comb_ref[...], w_ref[...].astype(BF16))


def _dil_out(os_, lses, w, res, *, tn=512):
    m, k = os_[0].shape
    w_stack, layer = w
    n = w_stack.shape[2]
    tm = min(ROW_TILE, m)
    ospec = pl.BlockSpec((tm, k), lambda i, j: (i, 0))
    lspec = pl.BlockSpec((tm, DIL_HEADS), lambda i, j: (i, 0))
    return pl.pallas_call(
        _dil_out_kernel, name="dil_out", grid=(m // tm, n // tn),
        in_specs=[ospec, ospec, ospec, lspec, lspec, lspec,
                  pl.BlockSpec((None, k, tn), lambda i, j: (layer, 0, j)),
                  pl.BlockSpec((tm, tn), lambda i, j: (i, j))],
        out_specs=pl.BlockSpec((tm, tn), lambda i, j: (i, j)),
        out_shape=jax.ShapeDtypeStruct((m, n), F32),
        scratch_shapes=[pltpu.VMEM((tm, k), BF16)],
        compiler_params=_cparams("parallel", "arbitrary"),
    )(*os_, *lses, w_stack, res)


def _dilated_mixer(x2, nw, w_in, w_out, rel_table, batch, seq):
    gw = w_in[0].shape[2] // len(DIL_PATTERNS)
    bias = _dil_bias(rel_table)
    outs = []
    for gi, (_, dil) in enumerate(DIL_PATTERNS):
        proj = _norm_proj(x2, nw, w_in, n_out=gw, tn=PROJ_COL_TILE, out_dtype=BF16, col_off=gi * gw)
        outs.append(_dil_group(proj, bias, gi, dil, batch, seq))
    return _dil_out([o for o, _ in outs], [l for _, l in outs], w_out, x2)


DN_CHUNK = 64
DN_REP = DN_V_HEADS // DN_K_HEADS
DN_INST = 2 * DN_REP
DN_ROWS = DN_INST * DN_CHUNK
DN_PREP_GROUP = 8
CONV_TILE = 256
CONV_HALO = 8


def _split3_bf16(x):
    hi = x.astype(BF16)
    r = x - hi.astype(F32)
    mid = r.astype(BF16)
    return hi, mid, (r - mid.astype(F32)).astype(BF16)


def _softplus(x):
    return jnp.maximum(x, 0.0) + jnp.log1p(jnp.exp(-jnp.abs(x)))


def _deltanet_kernel(q_ref, k_ref, v_ref, z_ref, tail_ref, cwq_ref, cwk_ref, cwv_ref, alog_ref, dtb_ref, nw_ref,
                     o_ref, pad_ref, qn_ref, kn_ref, vc_ref, mt_ref, bb_ref, qp_ref, op_ref, state_ref, oacc_ref):
    seq = q_ref.shape[0]
    nc = seq // DN_CHUNK
    c64 = DN_CHUNK
    dk = DN_DK

    pad_ref[0:CONV_HALO, :] = jnp.zeros((CONV_HALO, dk), F32)
    pad_ref[CONV_HALO + seq:, :] = jnp.zeros((CONV_HALO, dk), F32)

    def conv_into(src_ref, col, w_ref, dst_ref, l2_scale):
        pad_ref[CONV_HALO:CONV_HALO + seq, :] = src_ref[:, col:col + dk].astype(F32)
        w = w_ref[:, col:col + dk]

        def tile(r, carry):
            start = pl.multiple_of(r * CONV_TILE, CONV_TILE)
            win = pad_ref[pl.ds(start, CONV_TILE + 2 * CONV_HALO), :]
            y = jnp.zeros((CONV_TILE, dk), F32)
            for j in range(DN_CONV):
                lo = CONV_HALO + j - DN_CONV // 2
                y = y + win[lo:lo + CONV_TILE, :] * w[j:j + 1, :]
            y = _silu(y)
            if l2_scale is not None:
                y = y * (lax.rsqrt(jnp.sum(y * y, axis=1, keepdims=True) + EPS) * l2_scale)
            dst_ref[pl.ds(start, CONV_TILE), col:col + dk] = y.astype(dst_ref.dtype)
            return carry

        lax.fori_loop(0, seq // CONV_TILE, tile, 0)

    conv_into(q_ref, 0, cwq_ref, qn_ref, DN_DK ** -0.5)
    conv_into(k_ref, 0, cwk_ref, kn_ref, 1.0)
    for vl in range(DN_REP):
        conv_into(v_ref, vl * DN_DV, cwv_ref, vc_ref, None)

    rr = lax.broadcasted_iota(jnp.int32, (DN_ROWS, DN_ROWS), 0)
    cc = lax.broadcasted_iota(jnp.int32, (DN_ROWS, DN_ROWS), 1)
    same = (rr // c64) == (cc // c64)
    fwd_rows = rr < DN_REP * c64
    strict = same & ((fwd_rows & (rr > cc)) | (jnp.logical_not(fwd_rows) & (rr < cc)))
    eye = (rr == cc)
    eye_f = eye.astype(F32)
    r64 = lax.broadcasted_iota(jnp.int32, (c64, DN_ROWS), 0)
    c64i = lax.broadcasted_iota(jnp.int32, (c64, DN_ROWS), 1)
    eye_row = (r64 == (c64i % c64)).astype(F32)
    li = lax.broadcasted_iota(jnp.int32, (c64, c64), 0)
    lj = lax.broadcasted_iota(jnp.int32, (c64, c64), 1)
    tril = (lj <= li).astype(BF16)
    triu = (lj >= li).astype(BF16)
    lane8 = lax.broadcasted_iota(jnp.int32, (c64, 2 * DN_INST), 1)
    row_chain = lax.broadcasted_iota(jnp.int32, (1, DN_ROWS), 1) // c64
    tall_same = (lax.broadcasted_iota(jnp.int32, (DN_INST * dk, DN_ROWS), 0) // dk
                 == lax.broadcasted_iota(jnp.int32, (DN_INST * dk, DN_ROWS), 1) // c64)
    e_r = lax.broadcasted_iota(jnp.int32, (DN_INST * dk, dk), 0)
    e_c = lax.broadcasted_iota(jnp.int32, (DN_INST * dk, dk), 1)
    eye_tall = ((e_r % dk) == e_c).astype(F32)
    neg_a = -jnp.exp(alog_ref[0])
    dtb = dtb_ref[0]

    def stack(cols):
        return jnp.concatenate(cols, axis=0)

    def blockdiag(rows_):
        return jnp.where(same, jnp.concatenate([rows_] * DN_INST, axis=0), 0.0)

    def chunk_maps(c):
        rows = pl.ds(pl.multiple_of(c * c64, c64), c64)
        tl = tail_ref[0, 0, rows, :]
        beta8 = 1.0 / (1.0 + jnp.exp(-tl))
        g8 = neg_a * _softplus(tl + dtb)
        parts = _split3_bf16(g8)
        gcf = _dot(tril, parts[0]) + (_dot(tril, parts[1]) + _dot(tril, parts[2]))
        gcb = _dot(triu, parts[0]) + (_dot(triu, parts[1]) + _dot(triu, parts[2]))
        yield
        gc8 = jnp.where(lane8 < DN_INST + DN_REP, gcf, gcb)
        gtot8 = jnp.sum(g8, axis=0, keepdims=True)
        beta_b = jnp.broadcast_to(stack([beta8[:, i:i + 1] for i in range(DN_INST)]), (DN_ROWS, dk))
        gc_b = jnp.broadcast_to(stack([gc8[:, DN_INST + i:DN_INST + i + 1] for i in range(DN_INST)]), (DN_ROWS, dk))
        gtot = [gtot8[:, DN_INST + i:DN_INST + i + 1] for i in range(DN_INST)]
        gtot_b = stack([jnp.broadcast_to(t, (c64, dk)) for t in gtot])
        egc_b = jnp.exp(gc_b)

        kb = kn_ref[rows, :]
        qb = qn_ref[rows, :]
        vb = vc_ref[rows, :]
        k4 = stack([kb.astype(F32)] * DN_INST)
        q4 = stack([qb.astype(F32)] * DN_INST)
        v4 = stack([vb[:, (i % DN_REP) * DN_DV:(i % DN_REP + 1) * DN_DV] for i in range(DN_INST)]).astype(F32)
        k4t = k4.T

        gmat = jnp.concatenate([gc_b] * (DN_ROWS // dk), axis=1)
        grow = jnp.sum(jnp.where(eye, gmat, 0.0), axis=0, keepdims=True)
        decay = jnp.exp(jnp.where(strict, gmat - grow, NEG_INF))
        k4t_b = k4t.astype(BF16)
        kk_row = _dot(kb, k4t_b)
        qk_row = _dot(qb, k4t_b)
        yield
        n_bd = jnp.concatenate([beta_b] * (DN_ROWS // dk), axis=1) * stack([kk_row] * DN_INST) * decay
        qk_bd = stack([qk_row] * DN_INST) * (decay + eye_f)

        r_pow = -(n_bd[0:c64] + n_bd[c64:2 * c64] + n_bd[2 * c64:3 * c64] + n_bd[3 * c64:4 * c64])
        t_row = eye_row + r_pow
        r_pow = _dot(r_pow.astype(BF16), (-n_bd).astype(BF16))
        yield
        for _ in range(4):
            p_bd = blockdiag(r_pow).astype(BF16)
            rt = _dot(jnp.concatenate([r_pow, t_row], axis=0).astype(BF16), p_bd)
            yield
            r_pow = rt[0:c64]
            t_row = t_row + rt[c64:]
        t_row = t_row + _dot(t_row.astype(BF16), blockdiag(r_pow).astype(BF16))
        yield

        rhs = jnp.concatenate([k4 * (beta_b * egc_b), v4 * beta_b], axis=1).astype(BF16)
        wu = _dot(blockdiag(t_row).astype(BF16), rhs).astype(BF16)
        yield
        qk_wu = _dot(qk_bd.astype(BF16), wu)
        gtot_row = jnp.zeros((1, DN_ROWS), F32)
        for i in range(DN_INST):
            gtot_row = jnp.where(row_chain == i, gtot[i], gtot_row)
        kdt = k4t * jnp.exp(gtot_row - grow)
        kdt_wide = jnp.where(tall_same, stack([kdt] * DN_INST), 0.0).astype(BF16)
        kw = _dot(kdt_wide, wu)
        yield
        qp = (q4 * egc_b - qk_wu[:, :dk]).astype(BF16)
        op = qk_wu[:, dk:].astype(BF16)
        dl = stack([jnp.broadcast_to(jnp.exp(t), (dk, dk)) for t in gtot])
        return qp, op, (dl * eye_tall - kw[:, :dk]).astype(BF16), kw[:, dk:].astype(BF16)

    def run_interleaved(gens):
        results = [None] * len(gens)
        while any(r is None for r in results):
            for u, gen in enumerate(gens):
                if results[u] is None:
                    try:
                        next(gen)
                    except StopIteration as done:
                        results[u] = done.value
        return results

    def prep(g, carry):
        cs = [g * DN_PREP_GROUP + u for u in range(DN_PREP_GROUP)]
        maps = run_interleaved([chunk_maps(c) for c in cs])
        for c, (qp, op, mt, bb) in zip(cs, maps):
            qp_ref[c] = qp
            op_ref[c] = op
            mt_ref[c] = mt
            bb_ref[c] = bb
        return carry

    lax.fori_loop(0, nc // DN_PREP_GROUP, prep, 0)

    state_ref[...] = jnp.zeros_like(state_ref)
    oacc_ref[...] = jnp.zeros_like(oacc_ref)

    def sweep(j, carry):
        for i in range(DN_INST):
            c = j if i < DN_REP else nc - 1 - j
            s_bf = state_ref[i].astype(BF16)
            new_s = _dot(mt_ref[c, i * dk:(i + 1) * dk, :], s_bf) + bb_ref[c, i * dk:(i + 1) * dk, :].astype(F32)
            o = _dot(qp_ref[c, i * c64:(i + 1) * c64, :], s_bf) + op_ref[c, i * c64:(i + 1) * c64, :].astype(F32)
            state_ref[i] = new_s
            rows = pl.ds(pl.multiple_of(c * c64, c64), c64)
            vcols = slice((i % DN_REP) * DN_DV, (i % DN_REP + 1) * DN_DV)
            oacc_ref[rows, vcols] = oacc_ref[rows, vcols] + o
        return carry

    lax.fori_loop(0, nc, sweep, 0)

    def finish(r, carry):
        rows = pl.ds(pl.multiple_of(r * CONV_TILE, CONV_TILE), CONV_TILE)
        for vl in range(DN_REP):
            vcols = slice(vl * DN_DV, (vl + 1) * DN_DV)
            o = _rms_rows(oacc_ref[rows, vcols]) * nw_ref[...]
            o_ref[rows, vcols] = (o * _silu(z_ref[rows, vcols].astype(F32))).astype(o_ref.dtype)
        return carry

    lax.fori_loop(0, seq // CONV_TILE, finish, 0)


def _deltanet_core(proj, tail8, conv_w, alog8, dtb8, norm_w, batch, seq):
    nc = seq // DN_CHUNK
    kq = DN_K_HEADS
    vw = DN_REP * DN_DV
    kv = 2 * DN_K_HEADS * DN_DK // vw
    kz = kv + DN_K_HEADS
    once = pl.Buffered(1)
    return pl.pallas_call(
        _deltanet_kernel, name="deltanet_core", grid=(batch, DN_K_HEADS),
        in_specs=[pl.BlockSpec((seq, DN_DK), lambda b, h: (b, h), pipeline_mode=once),
                  pl.BlockSpec((seq, DN_DK), lambda b, h: (b, kq + h), pipeline_mode=once),
                  pl.BlockSpec((seq, vw), lambda b, h: (b, kv + h), pipeline_mode=once),
                  pl.BlockSpec((seq, vw), lambda b, h: (b, kz + h), pipeline_mode=once),
                  pl.BlockSpec((1, 1, seq, 2 * DN_INST), lambda b, h: (b, h, 0, 0), pipeline_mode=once),
                  pl.BlockSpec((DN_CONV, DN_DK), lambda b, h: (0, h)),
                  pl.BlockSpec((DN_CONV, DN_DK), lambda b, h: (0, kq + h)),
                  pl.BlockSpec((DN_CONV, vw), lambda b, h: (0, kv + h)),
                  pl.BlockSpec((1, 1, 2 * DN_INST), lambda b, h: (h, 0, 0)),
                  pl.BlockSpec((1, 1, 2 * DN_INST), lambda b, h: (h, 0, 0)),
                  pl.BlockSpec((1, DN_DV), lambda b, h: (0, 0))],
        out_specs=pl.BlockSpec((seq, vw), lambda b, h: (b, h)),
        out_shape=jax.ShapeDtypeStruct((batch * seq, DN_V_HEADS * DN_DV), BF16),
        scratch_shapes=[pltpu.VMEM((seq + 2 * CONV_HALO, DN_DK), F32),
                        pltpu.VMEM((seq, DN_DK), BF16), pltpu.VMEM((seq, DN_DK), BF16), pltpu.VMEM((seq, vw), BF16),
                        pltpu.VMEM((nc, DN_INST * DN_DK, DN_DK), BF16), pltpu.VMEM((nc, DN_INST * DN_DK, DN_DV), BF16),
                        pltpu.VMEM((nc, DN_ROWS, DN_DK), BF16), pltpu.VMEM((nc, DN_ROWS, DN_DV), BF16),
                        pltpu.VMEM((DN_INST, DN_DK, DN_DV), F32), pltpu.VMEM((seq, vw), F32)],
        compiler_params=_cparams("parallel", "parallel"),
    )(proj, proj, proj, proj, tail8, conv_w, conv_w, conv_w, alog8, dtb8, norm_w.reshape(1, DN_DV))


def _per_key_head(a):
    lead = a.shape[:-2]
    a = a.reshape(lead + (2, DN_K_HEADS, DN_REP))
    a = jnp.moveaxis(a, -2, 0)
    return a.reshape((DN_K_HEADS,) + lead + (DN_INST,))


def _deltanet_mixer(x2, nw, w_in, conv_w, a_log, dt_bias, norm_w, w_out, batch, seq):
    conv_dim = 2 * DN_K_HEADS * DN_DK + DN_V_HEADS * DN_DV
    main = conv_dim + DN_V_HEADS * DN_DV
    proj = _norm_proj(x2, nw, w_in, n_out=main, tn=PROJ_COL_TILE, out_dtype=BF16)
    tail = _norm_proj(x2, nw, w_in, n_out=LANES, tn=LANES, out_dtype=F32, col_off=main)
    tail = tail.reshape(batch, seq, 2, 2, DN_V_HEADS)
    tail8 = jnp.concatenate([_per_key_head(tail[:, :, 0]), _per_key_head(tail[:, :, 1])], axis=-1)
    tail8 = jnp.moveaxis(tail8, 0, 1)
    zeros = jnp.zeros((DN_K_HEADS, DN_INST), F32)
    alog8 = jnp.concatenate([zeros, _per_key_head(a_log)], axis=-1).reshape(DN_K_HEADS, 1, 2 * DN_INST)
    dtb8 = jnp.concatenate([zeros, _per_key_head(dt_bias)], axis=-1).reshape(DN_K_HEADS, 1, 2 * DN_INST)
    o = _deltanet_core(proj, tail8, conv_w, alog8, dtb8, norm_w, batch, seq)
    return _out_proj(o, w_out, x2)


def kernel(x, norm_mix_w, norm_ffn_w, final_norm_w, rel_bias_table, ret_w_in, ret_w_out, dil_w_in, dil_w_out,
           dn_w_in, dn_conv_w, dn_a_log, dn_dt_bias, dn_norm_w, dn_w_out, moe_w_router, moe_w_gate, moe_w_up,
           moe_w_down):
    batch, seq, d = x.shape
    x2 = x.reshape(batch * seq, d)
    for i in range(norm_mix_w.shape[0]):
        j = i // N_MIXERS
        kind = i % N_MIXERS
        if kind == 0:
            x2 = _retention_mixer(x2, norm_mix_w[i], (ret_w_in, j), (ret_w_out, j), batch, seq)
        elif kind == 1:
            x2 = _dilated_mixer(x2, norm_mix_w[i], (dil_w_in, j), (dil_w_out, j), rel_bias_table, batch, seq)
        else:
            x2 = _deltanet_mixer(x2, norm_mix_w[i], (dn_w_in, j), dn_conv_w[j], dn_a_log[j], dn_dt_bias[j],
                                 dn_norm_w[j], (dn_w_out, j), batch, seq)
        x2 = _moe(x2, norm_ffn_w[i], moe_w_router[i], moe_w_gate, moe_w_up, moe_w_down, i, batch, seq)
    return _final_norm(x2, final_norm_w).reshape(batch, seq, d)
```

```python
import functools
import math

import jax
import jax.numpy as jnp
import numpy as np
from jax import lax
from jax.experimental import pallas as pl
from jax.experimental.pallas import tpu as pltpu

F32 = jnp.float32
BF16 = jnp.bfloat16

D_MODEL = 2048
EPS = 1e-6
NEG_INF = -1e30
RET_HEADS = 8
RET_DK = D_MODEL // RET_HEADS
RET_DV = 2 * RET_DK
ROPE_BASE = 10000.0
RET_BWD_DECAY_OFFSET = 0.5
DIL_PATTERNS = ((128, 1), (512, 4), (2048, 16))
DIL_HEADS = 16
DIL_DH = D_MODEL // DIL_HEADS
REL_BUCKETS = 32
REL_MAX_DIST = 1024
DN_K_HEADS = 16
DN_V_HEADS = 32
DN_DK = 128
DN_DV = 128
DN_CONV = 5
N_EXPERTS = 16
EXPERT_FF = D_MODEL // 2
CAPACITY_FACTOR = 2
N_MIXERS = 3

LANES = 128
VMEM_LIMIT_BYTES = 56 * 1024 * 1024
ROW_TILE = 1024
PROJ_COL_TILE = 1024
RET_CHUNK = 256
RET_HEAD_GROUP = 2
MOE_FF_TILE = 256
MOE_DMA_UNROLL = 8


def _cparams(*sem):
    return pltpu.CompilerParams(dimension_semantics=sem, vmem_limit_bytes=VMEM_LIMIT_BYTES)


def _dot(a, b):
    return jnp.dot(a, b, preferred_element_type=F32)


def _dot_nt(a, b):
    return lax.dot_general(a, b, (((1,), (1,)), ((), ())), preferred_element_type=F32)


def _dot_tn(a, b):
    return lax.dot_general(a, b, (((0,), (0,)), ((), ())), preferred_element_type=F32)


def _silu(x):
    return x / (1.0 + jnp.exp(-x))


def _rms_rows(x):
    return x * lax.rsqrt(jnp.mean(x * x, axis=-1, keepdims=True) + EPS)


def _norm_proj_kernel(x_ref, nw_ref, w_ref, o_ref, hn_ref):
    @pl.when(pl.program_id(1) == 0)
    def _():
        hn_ref[...] = (_rms_rows(x_ref[...]) * nw_ref[...]).astype(BF16)

    o_ref[...] = _dot(hn_ref[...], w_ref[...].astype(BF16)).astype(o_ref.dtype)


def _norm_proj_residue_kernel(x_ref, nw_ref, w_ref, o_ref, hn_ref, lanes_ref, *, dil):
    @pl.when(pl.program_id(1) == 0)
    def _():
        hn_ref[...] = (_rms_rows(x_ref[...]) * nw_ref[...]).astype(BF16)

    acc = _dot(hn_ref[...], w_ref[...].astype(BF16))
    per = acc.shape[0] // dil
    for c in range(acc.shape[1] // LANES):
        cols = slice(c * LANES, (c + 1) * LANES)
        lanes_ref[c] = acc[:, cols]
        for r in range(dil):
            o_ref[r * per:(r + 1) * per, cols] = lanes_ref[c, pl.ds(r, per, stride=dil), :].astype(o_ref.dtype)


def _norm_proj_rope_kernel(x_ref, nw_ref, w_ref, cos_ref, sin_ref, o_ref, hn_ref, *, n_q_tiles, n_rope_tiles, k_scale):
    j = pl.program_id(1)

    @pl.when(j == 0)
    def _():
        hn_ref[...] = (_rms_rows(x_ref[...]) * nw_ref[...]).astype(BF16)

    acc = _dot(hn_ref[...], w_ref[...].astype(BF16))

    @pl.when(j < n_rope_tiles)
    def _():
        c = cos_ref[...]
        s = sin_ref[...]
        half = c.shape[1]
        scale = jnp.where(j >= n_q_tiles, k_scale, 1.0).astype(F32)
        for hh in range(acc.shape[1] // (2 * half)):
            x1 = acc[:, 2 * hh * half:(2 * hh + 1) * half]
            x2 = acc[:, (2 * hh + 1) * half:(2 * hh + 2) * half]
            o_ref[:, 2 * hh * half:(2 * hh + 1) * half] = ((x1 * c - x2 * s) * scale).astype(o_ref.dtype)
            o_ref[:, (2 * hh + 1) * half:(2 * hh + 2) * half] = ((x2 * c + x1 * s) * scale).astype(o_ref.dtype)

    @pl.when(j >= n_rope_tiles)
    def _():
        o_ref[...] = acc.astype(o_ref.dtype)


def _norm_proj(x2, nw, w, *, n_out, tn, out_dtype, col_off=0, rope=None, residue_major=1):
    m, k = x2.shape
    w_stack, layer = w
    tm = min(ROW_TILE, m)
    grid = (m // tm, n_out // tn)
    off = col_off // tn
    in_specs = [pl.BlockSpec((tm, k), lambda i, j: (i, 0)),
                pl.BlockSpec((1, k), lambda i, j: (0, 0)),
                pl.BlockSpec((None, k, tn), lambda i, j: (layer, 0, j + off))]
    args = [x2, nw.reshape(1, k), w_stack]
    scratch = [pltpu.VMEM((tm, k), BF16)]
    if residue_major > 1:
        body = functools.partial(_norm_proj_residue_kernel, dil=residue_major)
        scratch.append(pltpu.VMEM((tn // LANES, tm, LANES), F32))
    elif rope is None:
        body = _norm_proj_kernel
    else:
        cos, sin, seq, n_q_cols, n_rope_cols, k_scale = rope
        nsb = seq // tm
        in_specs += [pl.BlockSpec((tm, cos.shape[1]), lambda i, j: (i % nsb, 0)),
                     pl.BlockSpec((tm, cos.shape[1]), lambda i, j: (i % nsb, 0))]
        args += [cos, sin]
        body = functools.partial(_norm_proj_rope_kernel, n_q_tiles=n_q_cols // tn, n_rope_tiles=n_rope_cols // tn,
                                 k_scale=k_scale)
    return pl.pallas_call(
        body, name="norm_proj", grid=grid, in_specs=in_specs,
        out_specs=pl.BlockSpec((tm, tn), lambda i, j: (i, j)),
        out_shape=jax.ShapeDtypeStruct((m, n_out), out_dtype),
        scratch_shapes=scratch,
        compiler_params=_cparams("parallel", "arbitrary"),
    )(*args)


def _out_proj_kernel(a_ref, w_ref, res_ref, o_ref):
    o_ref[...] = res_ref[...] + _dot(a_ref[...], w_ref[...].astype(BF16))


def _out_proj(a, w, res, *, tn=512):
    m, k = a.shape
    w_stack, layer = w
    n = w_stack.shape[2]
    tm = min(ROW_TILE, m)
    return pl.pallas_call(
        _out_proj_kernel, name="out_proj", grid=(m // tm, n // tn),
        in_specs=[pl.BlockSpec((tm, k), lambda i, j: (i, 0)),
                  pl.BlockSpec((None, k, tn), lambda i, j: (layer, 0, j)),
                  pl.BlockSpec((tm, tn), lambda i, j: (i, j))],
        out_specs=pl.BlockSpec((tm, tn), lambda i, j: (i, j)),
        out_shape=jax.ShapeDtypeStruct((m, n), F32),
        compiler_params=_cparams("parallel", "arbitrary"),
    )(a, w_stack, res)


def _final_norm_kernel(x_ref, nw_ref, o_ref):
    o_ref[...] = _rms_rows(x_ref[...]) * nw_ref[...]


def _final_norm(x2, nw):
    m, k = x2.shape
    tm = min(ROW_TILE, m)
    return pl.pallas_call(
        _final_norm_kernel, name="final_norm", grid=(m // tm,),
        in_specs=[pl.BlockSpec((tm, k), lambda i: (i, 0)), pl.BlockSpec((1, k), lambda i: (0, 0))],
        out_specs=pl.BlockSpec((tm, k), lambda i: (i, 0)),
        out_shape=jax.ShapeDtypeStruct((m, k), F32),
        compiler_params=_cparams("parallel"),
    )(x2, nw.reshape(1, k))


def _retention_kernel(q_ref, k_ref, v_ref, g_ref, dmat_ref, vec_ref, o_ref, state_ref, oacc_ref):
    sweep = pl.program_id(2)
    c = pl.program_id(3)
    nc = pl.num_programs(3)
    hg = range(RET_HEAD_GROUP)
    qs = [slice(h * RET_DK, (h + 1) * RET_DK) for h in hg]
    vs = [slice(h * RET_DV, (h + 1) * RET_DV) for h in hg]

    @pl.when(c == 0)
    def _():
        state_ref[...] = jnp.zeros_like(state_ref)

    @pl.when(sweep == 0)
    def _():
        scores = [_dot_nt(q_ref[:, qs[h]], k_ref[:, qs[h]]) for h in hg]
        inters = [_dot(q_ref[:, qs[h]], state_ref[h].astype(BF16)) for h in hg]
        kds = [(k_ref[:, qs[h]].astype(F32) * vec_ref[h][:, 2:3]).astype(BF16) for h in hg]
        upds = [_dot_tn(kds[h], v_ref[:, vs[h]]) for h in hg]
        probs = [(scores[h] * dmat_ref[h]).astype(BF16) for h in hg]
        intras = [_dot(probs[h], v_ref[:, vs[h]]) for h in hg]
        for h in hg:
            vec = vec_ref[h]
            oacc_ref[c, :, vs[h]] = intras[h] + inters[h] * vec[:, 0:1]
            state_ref[h] = state_ref[h] * vec[0:1, 4:5] + upds[h]

    @pl.when(sweep == 1)
    def _():
        inters = [_dot(q_ref[:, qs[h]], state_ref[h].astype(BF16)) for h in hg]
        kds = [(k_ref[:, qs[h]].astype(F32) * vec_ref[h][:, 3:4]).astype(BF16) for h in hg]
        upds = [_dot_tn(kds[h], v_ref[:, vs[h]]) for h in hg]
        for h in hg:
            vec = vec_ref[h]
            o = _rms_rows(oacc_ref[nc - 1 - c, :, vs[h]] + inters[h] * vec[:, 1:2])
            o_ref[:, vs[h]] = (o * _silu(g_ref[:, vs[h]].astype(F32))).astype(o_ref.dtype)
            state_ref[h] = state_ref[h] * vec[0:1, 5:6] + upds[h]


def _retention_tables(chunk):
    hh = jnp.arange(RET_HEADS, dtype=F32)
    lg_f = jnp.log1p(-jnp.exp2(-5.0 - hh))
    lg_b = jnp.log1p(-jnp.exp2(-(5.0 + RET_BWD_DECAY_OFFSET) - hh))
    t = jnp.arange(chunk, dtype=F32)
    diff = t[:, None] - t[None, :]
    dmat = jnp.where(diff[None] >= 0,
                     jnp.exp(jnp.maximum(diff, 0.0)[None] * lg_f[:, None, None]),
                     jnp.exp(jnp.maximum(-diff, 0.0)[None] * lg_b[:, None, None]))
    cols = [jnp.exp((t[None, :] + 1.0) * lg_f[:, None]),
            jnp.exp((chunk - t)[None, :] * lg_b[:, None]),
            jnp.exp((chunk - 1.0 - t)[None, :] * lg_f[:, None]),
            jnp.exp(t[None, :] * lg_b[:, None]),
            jnp.broadcast_to(jnp.exp(chunk * lg_f)[:, None], (RET_HEADS, chunk)),
            jnp.broadcast_to(jnp.exp(chunk * lg_b)[:, None], (RET_HEADS, chunk))]
    cols += [jnp.zeros((RET_HEADS, chunk), F32)] * 2
    return dmat, jnp.stack(cols, axis=-1)


def _retention_core(proj, batch, seq):
    cq = min(RET_CHUNK, seq)
    nc = seq // cq
    dmat, vec = _retention_tables(cq)
    g = RET_HEAD_GROUP
    n_hg = RET_HEADS // g
    kq = n_hg
    kv = 2 * RET_HEADS * RET_DK // (g * RET_DV)
    kg = kv + n_hg

    def row(b, s, c):
        return b * nc + c + s * (nc - 1 - 2 * c)

    def row_out(b, s, c):
        return b * nc + nc - 1 - c * s

    return pl.pallas_call(
        _retention_kernel, name="retention_core", grid=(batch, n_hg, 2, nc),
        in_specs=[pl.BlockSpec((cq, g * RET_DK), lambda b, h, s, c: (row(b, s, c), h)),
                  pl.BlockSpec((cq, g * RET_DK), lambda b, h, s, c: (row(b, s, c), kq + h)),
                  pl.BlockSpec((cq, g * RET_DV), lambda b, h, s, c: (row(b, s, c), kv + h)),
                  pl.BlockSpec((cq, g * RET_DV), lambda b, h, s, c: (row_out(b, s, c), kg + h)),
                  pl.BlockSpec((g, cq, cq), lambda b, h, s, c: (h, 0, 0)),
                  pl.BlockSpec((g, cq, 8), lambda b, h, s, c: (h, 0, 0))],
        out_specs=pl.BlockSpec((cq, g * RET_DV), lambda b, h, s, c: (row_out(b, s, c), h)),
        out_shape=jax.ShapeDtypeStruct((batch * seq, RET_HEADS * RET_DV), BF16),
        scratch_shapes=[pltpu.VMEM((g, RET_DK, RET_DV), F32), pltpu.VMEM((nc, cq, g * RET_DV), F32)],
        compiler_params=_cparams("parallel", "parallel", "arbitrary", "arbitrary"),
    )(proj, proj, proj, proj, dmat, vec)


def _rope_tables(seq, half):
    inv = ROPE_BASE ** (-np.arange(half, dtype=np.float64) / half)
    ang = np.arange(seq, dtype=np.float64)[:, None] * inv[None, :]
    return jnp.asarray(np.cos(ang), F32), jnp.asarray(np.sin(ang), F32)


def _retention_mixer(x2, nw, w_in, w_out, batch, seq):
    cos, sin = _rope_tables(seq, RET_DK // 2)
    qk_cols = 2 * RET_HEADS * RET_DK
    proj = _norm_proj(x2, nw, w_in, n_out=w_in[0].shape[2], tn=PROJ_COL_TILE, out_dtype=BF16,
                      rope=(cos, sin, seq, qk_cols // 2, qk_cols, RET_DK ** -0.5))
    o = _retention_core(proj, batch, seq)
    return _out_proj(o, w_out, x2)


def _split_bf16(x):
    hi = x.astype(BF16)
    return hi, (x - hi.astype(F32)).astype(BF16)


def _router_kernel(x_ref, nw_ref, wr_ref, wrt_ref, hx_ref, afft_ref):
    half = x_ref.shape[1] // 2
    h = _rms_rows(x_ref[...]) * nw_ref[...]
    h_hi, h_lo = _split_bf16(h)
    bits = lax.bitcast_convert_type(h_hi.astype(F32), jnp.uint32)
    hx_ref[:, :half] = (bits[:, half:] & jnp.uint32(0xFFFF0000)) | (bits[:, :half] >> 16)
    w_hi, w_lo = _split_bf16(wr_ref[...])
    lg = _dot(h_hi, w_hi) + (_dot(h_lo, w_hi) + _dot(h_hi, w_lo))
    lane = lax.broadcasted_iota(jnp.int32, lg.shape, 1)
    lg = jnp.where(lane < N_EXPERTS, lg, NEG_INF)
    e = jnp.exp(lg - jnp.max(lg, axis=1, keepdims=True))
    hx_ref[:, half:] = lax.bitcast_convert_type(e / jnp.sum(e, axis=1, keepdims=True), jnp.uint32)
    wt_hi, wt_lo = _split_bf16(wrt_ref[...])
    lgt = _dot_nt(wt_hi, h_hi) + (_dot_nt(wt_hi, h_lo) + _dot_nt(wt_lo, h_hi))
    et = jnp.exp(lgt - jnp.max(lgt, axis=0, keepdims=True))
    afft_ref[0] = et / jnp.sum(et, axis=0, keepdims=True)


def _router(x2, nw, w_router, batch, seq):
    m, d = x2.shape
    tm = min(512, seq)
    nsb = seq // tm
    hw = d // 2 + LANES
    wr = jnp.pad(w_router, ((0, 0), (0, LANES - N_EXPERTS)))
    return pl.pallas_call(
        _router_kernel, name="moe_router", grid=(m // tm,),
        in_specs=[pl.BlockSpec((tm, d), lambda i: (i, 0)),
                  pl.BlockSpec((1, d), lambda i: (0, 0)),
                  pl.BlockSpec((d, LANES), lambda i: (0, 0)),
                  pl.BlockSpec((N_EXPERTS, d), lambda i: (0, 0))],
        out_specs=[pl.BlockSpec((tm, hw), lambda i: (i, 0)),
                   pl.BlockSpec((1, N_EXPERTS, tm), lambda i: (i // nsb, 0, i % nsb))],
        out_shape=[jax.ShapeDtypeStruct((m, hw), jnp.uint32),
                   jax.ShapeDtypeStruct((batch, N_EXPERTS, seq), F32)],
        compiler_params=_cparams("parallel"),
    )(x2, nw.reshape(1, d), wr, w_router.T)


def _topk_kernel(aff_ref, idx_ref, loc_ref, off_ref, end_ref, *, cap, n_groups, n_blk):
    n = n_groups * n_blk
    bits = lax.bitcast_convert_type(aff_ref[...], jnp.int32)
    ri = lax.broadcasted_iota(jnp.int32, (n, n), 0)
    ci = lax.broadcasted_iota(jnp.int32, (n, n), 1)
    same = (ri // n_blk) == (ci // n_blk)
    grp_ones = same.astype(BF16)
    grp_before = (same & (ci < ri)).astype(BF16)
    li = lax.broadcasted_iota(jnp.int32, (LANES, LANES), 0)
    lj = lax.broadcasted_iota(jnp.int32, (LANES, LANES), 1)
    incl = (li <= lj).astype(BF16)
    ones = jnp.ones((LANES, LANES), BF16)

    def row_total(mask):
        return _dot(mask.astype(BF16), ones)

    def group_count(mask):
        return _dot(grp_ones, row_total(mask).astype(BF16))

    def search(i, tau):
        cand = tau | jnp.left_shift(jnp.int32(1), 30 - i)
        return jnp.where(group_count(bits >= cand) >= cap, cand, tau)

    tau = lax.fori_loop(0, 31, search, jnp.zeros((n, LANES), jnp.int32))

    def cumsum(mask):
        mb = mask.astype(BF16)
        tot = _dot(mb, ones)
        return _dot(mb, incl), _dot(grp_before, tot.astype(BF16)), tot

    gt = bits > tau
    eq = bits == tau
    need = cap - group_count(gt)
    eq_loc, eq_off, _ = cumsum(eq)
    sel = gt | (eq & (eq_loc + eq_off <= need))
    loc, off, tot = cumsum(sel)
    loc_ref[...] = loc
    off_ref[...] = off
    end_ref[...] = off + tot

    slot = lax.broadcasted_iota(jnp.int32, (cap, LANES), 0).astype(F32)
    lane = lax.broadcasted_iota(jnp.int32, (cap, LANES), 1)
    eye = lax.broadcasted_iota(jnp.int32, (n_blk, LANES), 0) == lax.broadcasted_iota(jnp.int32, (n_blk, LANES), 1)
    pad = jnp.zeros((LANES - n_blk, LANES), BF16)

    def compact(g, acc):
        rows = pl.ds(pl.multiple_of(g * n_blk, n_blk), n_blk)
        end_row = jnp.sum(jnp.where(eye, end_ref[rows, :], 0.0), axis=0, keepdims=True)
        off_row = jnp.sum(jnp.where(eye, off_ref[rows, :], 0.0), axis=0, keepdims=True)
        blk = jnp.sum(((end_row <= slot) & (lane < n_blk)).astype(F32), axis=1, keepdims=True)
        onehot = lane == blk.astype(jnp.int32)
        loc_pad = jnp.concatenate([loc_ref[rows, :].astype(BF16), pad], axis=0)
        in_blk = _dot(onehot.astype(BF16), loc_pad)
        rank = slot - jnp.sum(jnp.where(onehot, off_row, 0.0), axis=1, keepdims=True)
        pos = blk * LANES + jnp.sum((in_blk <= rank).astype(F32), axis=1, keepdims=True)
        return jnp.where(lane == g, pos.astype(jnp.int32), acc)

    idx_ref[...] = lax.fori_loop(0, n_groups, compact, jnp.zeros((cap, LANES), jnp.int32))


def _topk(afft, cap):
    batch, n_e, seq = afft.shape
    n_groups = batch * n_e
    n_blk = seq // LANES
    n = n_groups * n_blk
    assert n_groups <= LANES and n_blk <= LANES
    idx = pl.pallas_call(
        functools.partial(_topk_kernel, cap=cap, n_groups=n_groups, n_blk=n_blk), name="moe_topk",
        out_shape=jax.ShapeDtypeStruct((cap, LANES), jnp.int32),
        scratch_shapes=[pltpu.VMEM((n, LANES), F32)] * 3,
        compiler_params=pltpu.CompilerParams(vmem_limit_bytes=VMEM_LIMIT_BYTES),
    )(afft.reshape(n, LANES))
    return idx[:, :n_groups].T.reshape(batch, n_e, cap)


def _moe_ffn_kernel(idx_ref, hx_hbm, wg_ref, wu_ref, wd_ref, xres_hbm, out_hbm,
                    hbuf, xbf, acc, rbuf, gate, sem_h, sem_r, sem_s):
    del xres_hbm
    e = pl.program_id(0)
    f = pl.program_id(1)
    n_e = pl.num_programs(0)
    nf = pl.num_programs(1)
    rows = xbf.shape[0]
    d = xbf.shape[1]
    per_step = rows // nf
    slot = e % 2
    base = e * rows
    nxt = ((e + 1) % n_e) * rows

    def h_copy(row_id, fq, u, s):
        return pltpu.make_async_copy(hx_hbm.at[pl.ds(row_id, 1), :], hbuf.at[s, fq, pl.ds(u, 1), :], sem_h.at[s])

    def wait_all(buf, sem):
        pltpu.make_async_copy(buf, buf, sem).wait()

    @pl.when((e == 0) & (f == 0))
    def _():
        def start(r, carry):
            h_copy(idx_ref[r], r // per_step, r % per_step, 0).start()
            return carry

        lax.fori_loop(0, rows, start, 0, unroll=MOE_DMA_UNROLL)

    @pl.when(f == 0)
    def _():
        wait_all(hbuf.at[slot], sem_h.at[slot])
        half = d // 2
        packed = hbuf[slot].reshape(rows, half + LANES)
        words = packed[:, :half]
        xbf[:, :half] = lax.bitcast_convert_type(words << 16, F32).astype(BF16)
        xbf[:, half:] = lax.bitcast_convert_type(words & jnp.uint32(0xFFFF0000), F32).astype(BF16)
        lane = lax.broadcasted_iota(jnp.int32, (rows, LANES), 1)
        aff = lax.bitcast_convert_type(packed[:, half:], F32)
        gate[...] = jnp.sum(jnp.where(lane == e, aff, 0.0), axis=1, keepdims=True)

    for u in range(per_step):
        r = f * per_step + u
        h_copy(idx_ref[nxt + r], f, u, 1 - slot).start()
        pltpu.make_async_copy(out_hbm.at[pl.ds(idx_ref[base + r], 1), :], rbuf.at[f, pl.ds(u, 1), :],
                              sem_r.at[0]).start()

    x = xbf[...]
    hid = (_silu(_dot(x, wg_ref[...].astype(BF16))) * _dot(x, wu_ref[...].astype(BF16))).astype(BF16)
    part = _dot(hid, wd_ref[...].astype(BF16))

    @pl.when(f == 0)
    def _():
        acc[...] = part

    @pl.when(f > 0)
    def _():
        acc[...] += part

    @pl.when(f == nf - 1)
    def _():
        wait_all(rbuf, sem_r.at[0])
        rbuf[...] = rbuf[...] + (acc[...] * gate[...]).reshape(rbuf.shape)

        for fq in range(rbuf.shape[0]):
            def start(u, carry, fq=fq):
                pltpu.make_async_copy(rbuf.at[fq, pl.ds(u, 1), :],
                                      out_hbm.at[pl.ds(idx_ref[base + fq * per_step + u], 1), :], sem_s.at[0]).start()
                return carry

            lax.fori_loop(0, per_step, start, 0, unroll=MOE_DMA_UNROLL)
        wait_all(rbuf, sem_s.at[0])

    @pl.when((e == n_e - 1) & (f == nf - 1))
    def _():
        wait_all(hbuf.at[1 - slot], sem_h.at[1 - slot])


def _moe_ffn(rows_idx, hx, x2, w_gate, w_up, w_down, layer):
    m, d = x2.shape
    _, n_e, _, ff = w_gate.shape
    rows = rows_idx.shape[0] // n_e
    tf = min(MOE_FF_TILE, ff)
    nf = ff // tf
    grid_spec = pltpu.PrefetchScalarGridSpec(
        num_scalar_prefetch=1, grid=(n_e, nf),
        in_specs=[pl.BlockSpec(memory_space=pl.ANY),
                  pl.BlockSpec((None, None, d, tf), lambda e, f, idx: (layer, e, 0, f)),
                  pl.BlockSpec((None, None, d, tf), lambda e, f, idx: (layer, e, 0, f)),
                  pl.BlockSpec((None, None, tf, d), lambda e, f, idx: (layer, e, f, 0)),
                  pl.BlockSpec(memory_space=pl.ANY)],
        out_specs=pl.BlockSpec(memory_space=pl.ANY),
        scratch_shapes=[pltpu.VMEM((2, nf, rows // nf, hx.shape[1]), jnp.uint32), pltpu.VMEM((rows, d), BF16),
                        pltpu.VMEM((rows, d), F32), pltpu.VMEM((nf, rows // nf, d), F32),
                        pltpu.VMEM((rows, 1), F32),
                        pltpu.SemaphoreType.DMA((2,)), pltpu.SemaphoreType.DMA((1,)),
                        pltpu.SemaphoreType.DMA((1,))])
    return pl.pallas_call(
        _moe_ffn_kernel, name="moe_ffn", grid_spec=grid_spec,
        out_shape=jax.ShapeDtypeStruct((m, d), F32),
        input_output_aliases={5: 0},
        compiler_params=_cparams("arbitrary", "arbitrary"),
    )(rows_idx, hx, w_gate, w_up, w_down, x2)


def _moe(x2, nw, w_router, w_gate, w_up, w_down, layer, batch, seq):
    cap = CAPACITY_FACTOR * seq // N_EXPERTS
    hx, afft = _router(x2, nw, w_router, batch, seq)
    idx = _topk(afft, cap)
    rows_idx = idx + (jnp.arange(batch, dtype=jnp.int32) * seq)[:, None, None]
    rows_idx = jnp.transpose(rows_idx, (1, 0, 2)).reshape(-1)
    return _moe_ffn(rows_idx, hx, x2, w_gate, w_up, w_down, layer)


DIL_BLOCK = 64


def _t5_bucket(rel):
    nb = REL_BUCKETS // 2
    max_exact = nb // 2
    ret = jnp.where(rel > 0, nb, 0)
    n = jnp.abs(rel)
    nf = jnp.maximum(n, 1).astype(F32)
    large = max_exact + (jnp.log(nf / max_exact) / math.log(REL_MAX_DIST / max_exact)
                         * (nb - max_exact)).astype(jnp.int32)
    large = jnp.minimum(large, nb - 1)
    return ret + jnp.where(n < max_exact, n, large)


def _dil_bias_kernel(table_ref, bucket_ref, o_ref):
    col = pl.program_id(0)
    bkt = bucket_ref[0]
    acc = jnp.zeros(bkt.shape, F32)
    for b in range(REL_BUCKETS):
        acc = jnp.where(bkt == b, table_ref[b, col], acc)
    s = lax.broadcasted_iota(jnp.int32, bkt.shape, 0)
    t = lax.broadcasted_iota(jnp.int32, bkt.shape, 1)
    o_ref[0] = jnp.where((t >= s) & (t <= s + 2 * DIL_BLOCK), acc, NEG_INF)


def _dil_bias(rel_table):
    qb = DIL_BLOCK
    rel_steps = jnp.arange(3 * qb)[None, :] - jnp.arange(qb)[:, None] - qb
    buckets = jnp.stack([_t5_bucket(rel_steps * dil) for _, dil in DIL_PATTERNS]).astype(jnp.int32)
    n_col = rel_table.shape[1]
    return pl.pallas_call(
        _dil_bias_kernel, name="dil_bias", grid=(n_col,),
        in_specs=[pl.BlockSpec(memory_space=pltpu.SMEM),
                  pl.BlockSpec((1, qb, 3 * qb), lambda c: (c // DIL_HEADS, 0, 0))],
        out_specs=pl.BlockSpec((1, qb, 3 * qb), lambda c: (c, 0, 0)),
        out_shape=jax.ShapeDtypeStruct((n_col, qb, 3 * qb), F32),
        compiler_params=_cparams("parallel"),
    )(rel_table, buckets)


def _dil_attn_kernel(q_ref, kp_ref, kc_ref, kn_ref, vp_ref, vc_ref, vn_ref, bias_ref, o_ref, lse_ref):
    jb = pl.program_id(2)
    nb = pl.num_programs(2)
    qb = q_ref.shape[0]
    t = lax.broadcasted_iota(jnp.int32, (qb, 3 * qb), 1)
    in_seq = ((jb > 0) | (t >= qb)) & ((jb < nb - 1) | (t < 2 * qb))
    kw = jnp.concatenate([kp_ref[...], kc_ref[...], kn_ref[...]], axis=0)
    vw = jnp.concatenate([vp_ref[...], vc_ref[...], vn_ref[...]], axis=0)
    lane = lax.broadcasted_iota(jnp.int32, (qb, LANES), 1)
    lse = jnp.zeros((qb, LANES), F32)
    heads = [slice(h * DIL_DH, (h + 1) * DIL_DH) for h in range(DIL_HEADS)]
    scores = [_dot_nt(q_ref[:, hs], kw[:, hs]) for hs in heads]
    probs, dens = [], []
    for h, s in enumerate(scores):
        s = jnp.where(in_seq, s * (DIL_DH ** -0.5) + bias_ref[h], NEG_INF)
        m = jnp.max(s, axis=1, keepdims=True)
        p = jnp.exp(s - m)
        den = jnp.sum(p, axis=1, keepdims=True)
        probs.append(p.astype(BF16))
        dens.append(den)
        lse = jnp.where(lane == h, m + jnp.log(den), lse)
    outs = [_dot(p, vw[:, hs]) for p, hs in zip(probs, heads)]
    for hs, o, den in zip(heads, outs, dens):
        o_ref[:, hs] = (o / den).astype(o_ref.dtype)
    lse_ref[0] = lse[:, :DIL_HEADS]


def _dil_group(proj, bias, gi, dil, batch, seq):
    qb = DIL_BLOCK
    nb = seq // dil // qb
    hw = DIL_HEADS * DIL_DH
    m = batch * seq
    tm = min(ROW_TILE, m)
    bpt = tm // (dil * qb)
    tiles_per_seq = seq // tm

    def row_block(b, r, j):
        return (b * tiles_per_seq + j // bpt) * (tm // qb) + r * bpt + j % bpt

    def spec(which, shift):
        def imap(b, r, j):
            return (row_block(b, r, jnp.clip(j + shift, 0, nb - 1)), which)
        return pl.BlockSpec((qb, hw), imap)

    o, lse = pl.pallas_call(
        _dil_attn_kernel, name=f"dil_attn_d{dil}", grid=(batch, dil, nb),
        in_specs=[spec(0, 0), spec(1, -1), spec(1, 0), spec(1, 1), spec(2, -1), spec(2, 0), spec(2, 1),
                  pl.BlockSpec((DIL_HEADS, qb, 3 * qb), lambda b, r, j: (gi, 0, 0))],
        out_specs=[pl.BlockSpec((qb, hw), lambda b, r, j: (row_block(b, r, j), 0)),
                   pl.BlockSpec((1, qb, DIL_HEADS), lambda b, r, j: (row_block(b, r, j), 0, 0))],
        out_shape=[jax.ShapeDtypeStruct((m, hw), BF16),
                   jax.ShapeDtypeStruct((m // qb, qb, DIL_HEADS), F32)],
        compiler_params=_cparams("parallel", "parallel", "arbitrary"),
    )(proj, proj, proj, proj, proj, proj, proj, bias)
    lse = lse.reshape(m // tm, dil, tm // dil, DIL_HEADS)
    return o, jnp.transpose(lse, (0, 2, 1, 3)).reshape(m, DIL_HEADS)


def _dil_out_kernel(o0_ref, o1_ref, o2_ref, l0_ref, l1_ref, l2_ref, w_ref, res_ref, out_ref, comb_ref, tok_ref, *,
                    dils):
    @pl.when(pl.program_id(1) == 0)
    def _():
        l0, l1, l2 = l0_ref[...], l1_ref[...], l2_ref[...]
        m = jnp.maximum(jnp.maximum(l0, l1), l2)
        e0, e1, e2 = jnp.exp(l0 - m), jnp.exp(l1 - m), jnp.exp(l2 - m)
        den = e0 + e1 + e2
        wts = (e0 / den, e1 / den, e2 / den)
        rows = comb_ref.shape[0]
        for h in range(DIL_HEADS):
            hs = slice(h * DIL_DH, (h + 1) * DIL_DH)
            total = None
            for g, (o_ref, dil) in enumerate(zip((o0_ref, o1_ref, o2_ref), dils)):
                if dil == 1:
                    o = o_ref[:, hs].astype(F32)
                else:
                    per = rows // dil
                    for r in range(dil):
                        tok_ref[g, pl.ds(r, per, stride=dil), :] = o_ref[r * per:(r + 1) * per, hs].astype(F32)
                    o = tok_ref[g]
                term = wts[g][:, h:h + 1] * o
                total = term if total is None else total + term
            comb_ref[:, hs] = total.astype(BF16)

    out_ref[...] = res_ref[...] + _dot(comb_ref[...], w_ref[...].astype(BF16))


def _dil_out(os_, lses, w, res, *, tn=512):
    m, k = os_[0].shape
    w_stack, layer = w
    n = w_stack.shape[2]
    tm = min(ROW_TILE, m)
    ospec = pl.BlockSpec((tm, k), lambda i, j: (i, 0))
    lspec = pl.BlockSpec((tm, DIL_HEADS), lambda i, j: (i, 0))
    dils = tuple(dil for _, dil in DIL_PATTERNS)
    return pl.pallas_call(
        functools.partial(_dil_out_kernel, dils=dils), name="dil_out", grid=(m // tm, n // tn),
        in_specs=[ospec, ospec, ospec, lspec, lspec, lspec,
                  pl.BlockSpec((None, k, tn), lambda i, j: (layer, 0, j)),
                  pl.BlockSpec((tm, tn), lambda i, j: (i, j))],
        out_specs=pl.BlockSpec((tm, tn), lambda i, j: (i, j)),
        out_shape=jax.ShapeDtypeStruct((m, n), F32),
        scratch_shapes=[pltpu.VMEM((tm, k), BF16), pltpu.VMEM((len(dils), tm, DIL_DH), F32)],
        compiler_params=_cparams("parallel", "arbitrary"),
    )(*os_, *lses, w_stack, res)


def _dilated_mixer(x2, nw, w_in, w_out, rel_table, batch, seq):
    gw = w_in[0].shape[2] // len(DIL_PATTERNS)
    bias = _dil_bias(rel_table)
    outs = []
    for gi, (_, dil) in enumerate(DIL_PATTERNS):
        proj = _norm_proj(x2, nw, w_in, n_out=gw, tn=PROJ_COL_TILE, out_dtype=BF16, col_off=gi * gw,
                          residue_major=dil)
        outs.append(_dil_group(proj, bias, gi, dil, batch, seq))
    return _dil_out([o for o, _ in outs], [l for _, l in outs], w_out, x2)


DN_CHUNK = 64
DN_REP = DN_V_HEADS // DN_K_HEADS
DN_INST = 2 * DN_REP
DN_ROWS = DN_INST * DN_CHUNK
DN_PREP_GROUP = 8
CONV_TILE = 256
CONV_HALO = 8


def _split3_bf16(x):
    hi = x.astype(BF16)
    r = x - hi.astype(F32)
    mid = r.astype(BF16)
    return hi, mid, (r - mid.astype(F32)).astype(BF16)


def _softplus(x):
    return jnp.maximum(x, 0.0) + jnp.log1p(jnp.exp(-jnp.abs(x)))


def _deltanet_kernel(q_ref, k_ref, v_ref, z_ref, tail_ref, cwq_ref, cwk_ref, cwv_ref, alog_ref, dtb_ref, nw_ref,
                     o_ref, pad_ref, qn_ref, kn_ref, vc_ref, mt_ref, bb_ref, qp_ref, op_ref, state_ref, oacc_ref):
    seq = q_ref.shape[0]
    nc = seq // DN_CHUNK
    c64 = DN_CHUNK
    dk = DN_DK

    n_pad = pad_ref.shape[0]
    pad_ref[:, 0:CONV_HALO, :] = jnp.zeros((n_pad, CONV_HALO, dk), F32)
    pad_ref[:, CONV_HALO + seq:, :] = jnp.zeros((n_pad, CONV_HALO, dk), F32)

    def conv_into(jobs):
        ws = []
        for p, (src_ref, col, w_ref, _, _) in enumerate(jobs):
            pad_ref[p, CONV_HALO:CONV_HALO + seq, :] = src_ref[:, col:col + dk].astype(F32)
            ws.append(w_ref[:, col:col + dk])

        def tile(r, carry):
            start = pl.multiple_of(r * CONV_TILE, CONV_TILE)
            ys = []
            for p in range(len(jobs)):
                win = pad_ref[p, pl.ds(start, CONV_TILE + 2 * CONV_HALO), :]
                y = jnp.zeros((CONV_TILE, dk), F32)
                for j in range(DN_CONV):
                    lo = CONV_HALO + j - DN_CONV // 2
                    y = y + win[lo:lo + CONV_TILE, :] * ws[p][j:j + 1, :]
                ys.append(_silu(y))
            for y, (_, col, _, dst_ref, l2_scale) in zip(ys, jobs):
                if l2_scale is not None:
                    y = y * (lax.rsqrt(jnp.sum(y * y, axis=1, keepdims=True) + EPS) * l2_scale)
                dst_ref[pl.ds(start, CONV_TILE), col:col + dk] = y.astype(dst_ref.dtype)
            return carry

        lax.fori_loop(0, seq // CONV_TILE, tile, 0)

    conv_into([(q_ref, 0, cwq_ref, qn_ref, DN_DK ** -0.5), (k_ref, 0, cwk_ref, kn_ref, 1.0)])
    conv_into([(v_ref, vl * DN_DV, cwv_ref, vc_ref, None) for vl in range(DN_REP)])

    rr = lax.broadcasted_iota(jnp.int32, (DN_ROWS, DN_ROWS), 0)
    cc = lax.broadcasted_iota(jnp.int32, (DN_ROWS, DN_ROWS), 1)
    same = (rr // c64) == (cc // c64)
    fwd_rows = rr < DN_REP * c64
    strict = same & ((fwd_rows & (rr > cc)) | (jnp.logical_not(fwd_rows) & (rr < cc)))
    eye = (rr == cc)
    eye_f = eye.astype(F32)
    r64 = lax.broadcasted_iota(jnp.int32, (c64, DN_ROWS), 0)
    c64i = lax.broadcasted_iota(jnp.int32, (c64, DN_ROWS), 1)
    eye_row = (r64 == (c64i % c64)).astype(F32)
    li = lax.broadcasted_iota(jnp.int32, (c64, c64), 0)
    lj = lax.broadcasted_iota(jnp.int32, (c64, c64), 1)
    tril = (lj <= li).astype(BF16)
    triu = (lj >= li).astype(BF16)
    lane8 = lax.broadcasted_iota(jnp.int32, (c64, 2 * DN_INST), 1)
    row_chain = lax.broadcasted_iota(jnp.int32, (1, DN_ROWS), 1) // c64
    tall_same = (lax.broadcasted_iota(jnp.int32, (DN_INST * dk, DN_ROWS), 0) // dk
                 == lax.broadcasted_iota(jnp.int32, (DN_INST * dk, DN_ROWS), 1) // c64)
    e_r = lax.broadcasted_iota(jnp.int32, (DN_INST * dk, dk), 0)
    e_c = lax.broadcasted_iota(jnp.int32, (DN_INST * dk, dk), 1)
    eye_tall = ((e_r % dk) == e_c).astype(F32)
    neg_a = -jnp.exp(alog_ref[0])
    dtb = dtb_ref[0]

    def stack(cols):
        return jnp.concatenate(cols, axis=0)

    def blockdiag(rows_):
        return jnp.where(same, jnp.concatenate([rows_] * DN_INST, axis=0), 0.0)

    def chunk_maps(c):
        rows = pl.ds(pl.multiple_of(c * c64, c64), c64)
        tl = tail_ref[0, 0, rows, :]
        beta8 = 1.0 / (1.0 + jnp.exp(-tl))
        g8 = neg_a * _softplus(tl + dtb)
        parts = _split3_bf16(g8)
        gcf = _dot(tril, parts[0]) + (_dot(tril, parts[1]) + _dot(tril, parts[2]))
        gcb = _dot(triu, parts[0]) + (_dot(triu, parts[1]) + _dot(triu, parts[2]))
        yield
        gc8 = jnp.where(lane8 < DN_INST + DN_REP, gcf, gcb)
        gtot8 = jnp.sum(g8, axis=0, keepdims=True)
        beta_b = jnp.broadcast_to(stack([beta8[:, i:i + 1] for i in range(DN_INST)]), (DN_ROWS, dk))
        gc_b = jnp.broadcast_to(stack([gc8[:, DN_INST + i:DN_INST + i + 1] for i in range(DN_INST)]), (DN_ROWS, dk))
        gtot = [gtot8[:, DN_INST + i:DN_INST + i + 1] for i in range(DN_INST)]
        gtot_b = stack([jnp.broadcast_to(t, (c64, dk)) for t in gtot])
        egc_b = jnp.exp(gc_b)

        kb = kn_ref[rows, :]
        qb = qn_ref[rows, :]
        vb = vc_ref[rows, :]
        k4 = stack([kb.astype(F32)] * DN_INST)
        q4 = stack([qb.astype(F32)] * DN_INST)
        v4 = stack([vb[:, (i % DN_REP) * DN_DV:(i % DN_REP + 1) * DN_DV] for i in range(DN_INST)]).astype(F32)
        k4t = k4.T

        gmat = jnp.concatenate([gc_b] * (DN_ROWS // dk), axis=1)
        grow = jnp.sum(jnp.where(eye, gmat, 0.0), axis=0, keepdims=True)
        decay = jnp.exp(jnp.where(strict, gmat - grow, NEG_INF))
        k4t_b = k4t.astype(BF16)
        kk_row = _dot(kb, k4t_b)
        qk_row = _dot(qb, k4t_b)
        yield
        n_bd = jnp.concatenate([beta_b] * (DN_ROWS // dk), axis=1) * stack([kk_row] * DN_INST) * decay
        qk_bd = stack([qk_row] * DN_INST) * (decay + eye_f)

        r_pow = -(n_bd[0:c64] + n_bd[c64:2 * c64] + n_bd[2 * c64:3 * c64] + n_bd[3 * c64:4 * c64])
        t_row = eye_row + r_pow
        r_pow = _dot(r_pow.astype(BF16), (-n_bd).astype(BF16))
        yield
        for _ in range(4):
            p_bd = blockdiag(r_pow).astype(BF16)
            rt = _dot(jnp.concatenate([r_pow, t_row], axis=0).astype(BF16), p_bd)
            yield
            r_pow = rt[0:c64]
            t_row = t_row + rt[c64:]
        t_row = t_row + _dot(t_row.astype(BF16), blockdiag(r_pow).astype(BF16))
        yield

        rhs = jnp.concatenate([k4 * (beta_b * egc_b), v4 * beta_b], axis=1).astype(BF16)
        wu = _dot(blockdiag(t_row).astype(BF16), rhs).astype(BF16)
        yield
        qk_wu = _dot(qk_bd.astype(BF16), wu)
        gtot_row = jnp.zeros((1, DN_ROWS), F32)
        for i in range(DN_INST):
            gtot_row = jnp.where(row_chain == i, gtot[i], gtot_row)
        kdt = k4t * jnp.exp(gtot_row - grow)
        kdt_wide = jnp.where(tall_same, stack([kdt] * DN_INST), 0.0).astype(BF16)
        kw = _dot(kdt_wide, wu)
        yield
        qp = (q4 * egc_b - qk_wu[:, :dk]).astype(BF16)
        op = qk_wu[:, dk:].astype(BF16)
        dl = stack([jnp.broadcast_to(jnp.exp(t), (dk, dk)) for t in gtot])
        return qp, op, (dl * eye_tall - kw[:, :dk]).astype(BF16), kw[:, dk:].astype(BF16)

    def run_interleaved(gens):
        results = [None] * len(gens)
        while any(r is None for r in results):
            for u, gen in enumerate(gens):
                if results[u] is None:
                    try:
                        next(gen)
                    except StopIteration as done:
                        results[u] = done.value
        return results

    def prep(g, carry):
        cs = [g * DN_PREP_GROUP + u for u in range(DN_PREP_GROUP)]
        maps = run_interleaved([chunk_maps(c) for c in cs])
        for c, (qp, op, mt, bb) in zip(cs, maps):
            qp_ref[c] = qp
            op_ref[c] = op
            mt_ref[c] = mt
            bb_ref[c] = bb
        return carry

    lax.fori_loop(0, nc // DN_PREP_GROUP, prep, 0)

    state_ref[...] = jnp.zeros_like(state_ref)
    oacc_ref[...] = jnp.zeros_like(oacc_ref)

    def sweep(j, carry):
        for i in range(DN_INST):
            c = j if i < DN_REP else nc - 1 - j
            s_bf = state_ref[i].astype(BF16)
            new_s = _dot(mt_ref[c, i * dk:(i + 1) * dk, :], s_bf) + bb_ref[c, i * dk:(i + 1) * dk, :].astype(F32)
            o = _dot(qp_ref[c, i * c64:(i + 1) * c64, :], s_bf) + op_ref[c, i * c64:(i + 1) * c64, :].astype(F32)
            state_ref[i] = new_s
            rows = pl.ds(pl.multiple_of(c * c64, c64), c64)
            vcols = slice((i % DN_REP) * DN_DV, (i % DN_REP + 1) * DN_DV)
            oacc_ref[rows, vcols] = oacc_ref[rows, vcols] + o
        return carry

    lax.fori_loop(0, nc, sweep, 0)

    def finish(r, carry):
        rows = pl.ds(pl.multiple_of(r * CONV_TILE, CONV_TILE), CONV_TILE)
        for vl in range(DN_REP):
            vcols = slice(vl * DN_DV, (vl + 1) * DN_DV)
            o = _rms_rows(oacc_ref[rows, vcols]) * nw_ref[...]
            o_ref[rows, vcols] = (o * _silu(z_ref[rows, vcols].astype(F32))).astype(o_ref.dtype)
        return carry

    lax.fori_loop(0, seq // CONV_TILE, finish, 0)


def _deltanet_core(proj, tail8, conv_w, alog8, dtb8, norm_w, batch, seq):
    nc = seq // DN_CHUNK
    assert seq % (DN_CHUNK * DN_PREP_GROUP) == 0 and seq % CONV_TILE == 0
    kq = DN_K_HEADS
    vw = DN_REP * DN_DV
    kv = 2 * DN_K_HEADS * DN_DK // vw
    kz = kv + DN_K_HEADS
    once = pl.Buffered(1)
    return pl.pallas_call(
        _deltanet_kernel, name="deltanet_core", grid=(batch, DN_K_HEADS),
        in_specs=[pl.BlockSpec((seq, DN_DK), lambda b, h: (b, h), pipeline_mode=once),
                  pl.BlockSpec((seq, DN_DK), lambda b, h: (b, kq + h), pipeline_mode=once),
                  pl.BlockSpec((seq, vw), lambda b, h: (b, kv + h), pipeline_mode=once),
                  pl.BlockSpec((seq, vw), lambda b, h: (b, kz + h), pipeline_mode=once),
                  pl.BlockSpec((1, 1, seq, 2 * DN_INST), lambda b, h: (b, h, 0, 0), pipeline_mode=once),
                  pl.BlockSpec((DN_CONV, DN_DK), lambda b, h: (0, h)),
                  pl.BlockSpec((DN_CONV, DN_DK), lambda b, h: (0, kq + h)),
                  pl.BlockSpec((DN_CONV, vw), lambda b, h: (0, kv + h)),
                  pl.BlockSpec((1, 1, 2 * DN_INST), lambda b, h: (h, 0, 0)),
                  pl.BlockSpec((1, 1, 2 * DN_INST), lambda b, h: (h, 0, 0)),
                  pl.BlockSpec((1, DN_DV), lambda b, h: (0, 0))],
        out_specs=pl.BlockSpec((seq, vw), lambda b, h: (b, h)),
        out_shape=jax.ShapeDtypeStruct((batch * seq, DN_V_HEADS * DN_DV), BF16),
        scratch_shapes=[pltpu.VMEM((2, seq + 2 * CONV_HALO, DN_DK), F32),
                        pltpu.VMEM((seq, DN_DK), BF16), pltpu.VMEM((seq, DN_DK), BF16), pltpu.VMEM((seq, vw), BF16),
                        pltpu.VMEM((nc, DN_INST * DN_DK, DN_DK), BF16), pltpu.VMEM((nc, DN_INST * DN_DK, DN_DV), BF16),
                        pltpu.VMEM((nc, DN_ROWS, DN_DK), BF16), pltpu.VMEM((nc, DN_ROWS, DN_DV), BF16),
                        pltpu.VMEM((DN_INST, DN_DK, DN_DV), F32), pltpu.VMEM((seq, vw), F32)],
        compiler_params=_cparams("parallel", "parallel"),
    )(proj, proj, proj, proj, tail8, conv_w, conv_w, conv_w, alog8, dtb8, norm_w.reshape(1, DN_DV))


def _per_key_head(a):
    lead = a.shape[:-2]
    a = a.reshape(lead + (2, DN_K_HEADS, DN_REP))
    a = jnp.moveaxis(a, -2, 0)
    return a.reshape((DN_K_HEADS,) + lead + (DN_INST,))


def _deltanet_mixer(x2, nw, w_in, conv_w, a_log, dt_bias, norm_w, w_out, batch, seq):
    conv_dim = 2 * DN_K_HEADS * DN_DK + DN_V_HEADS * DN_DV
    main = conv_dim + DN_V_HEADS * DN_DV
    proj = _norm_proj(x2, nw, w_in, n_out=main, tn=PROJ_COL_TILE, out_dtype=BF16)
    tail = _norm_proj(x2, nw, w_in, n_out=LANES, tn=LANES, out_dtype=F32, col_off=main)
    tail = tail.reshape(batch, seq, 2, 2, DN_V_HEADS)
    tail8 = jnp.concatenate([_per_key_head(tail[:, :, 0]), _per_key_head(tail[:, :, 1])], axis=-1)
    tail8 = jnp.moveaxis(tail8, 0, 1)
    zeros = jnp.zeros((DN_K_HEADS, DN_INST), F32)
    alog8 = jnp.concatenate([zeros, _per_key_head(a_log)], axis=-1).reshape(DN_K_HEADS, 1, 2 * DN_INST)
    dtb8 = jnp.concatenate([zeros, _per_key_head(dt_bias)], axis=-1).reshape(DN_K_HEADS, 1, 2 * DN_INST)
    o = _deltanet_core(proj, tail8, conv_w, alog8, dtb8, norm_w, batch, seq)
    return _out_proj(o, w_out, x2)


def kernel(x, norm_mix_w, norm_ffn_w, final_norm_w, rel_bias_table, ret_w_in, ret_w_out, dil_w_in, dil_w_out,
           dn_w_in, dn_conv_w, dn_a_log, dn_dt_bias, dn_norm_w, dn_w_out, moe_w_router, moe_w_gate, moe_w_up,
           moe_w_down):
    batch, seq, d = x.shape
    x2 = x.reshape(batch * seq, d)
    for i in range(norm_mix_w.shape[0]):
        j = i // N_MIXERS
        kind = i % N_MIXERS
        if kind == 0:
            x2 = _retention_mixer(x2, norm_mix_w[i], (ret_w_in, j), (ret_w_out, j), batch, seq)
        elif kind == 1:
            x2 = _dilated_mixer(x2, norm_mix_w[i], (dil_w_in, j), (dil_w_out, j), rel_bias_table, batch, seq)
        else:
            x2 = _deltanet_mixer(x2, norm_mix_w[i], (dn_w_in, j), dn_conv_w[j], dn_a_log[j], dn_dt_bias[j],
                                 dn_norm_w[j], (dn_w_out, j), batch, seq)
        x2 = _moe(x2, norm_ffn_w[i], moe_w_router[i], moe_w_gate, moe_w_up, moe_w_down, i, batch, seq)
    return _final_norm(x2, final_norm_w).reshape(batch, seq, d)
```

```python
import functools
import math

import jax
import jax.numpy as jnp
import numpy as np
from jax import lax
from jax.experimental import pallas as pl
from jax.experimental.pallas import tpu as pltpu

F32 = jnp.float32
BF16 = jnp.bfloat16

D_MODEL = 2048
EPS = 1e-6
NEG_INF = -1e30
RET_HEADS = 8
RET_DK = D_MODEL // RET_HEADS
RET_DV = 2 * RET_DK
ROPE_BASE = 10000.0
RET_BWD_DECAY_OFFSET = 0.5
DIL_PATTERNS = ((128, 1), (512, 4), (2048, 16))
DIL_HEADS = 16
DIL_DH = D_MODEL // DIL_HEADS
REL_BUCKETS = 32
REL_MAX_DIST = 1024
DN_K_HEADS = 16
DN_V_HEADS = 32
DN_DK = 128
DN_DV = 128
DN_CONV = 5
N_EXPERTS = 16
EXPERT_FF = D_MODEL // 2
CAPACITY_FACTOR = 2
N_MIXERS = 3

LANES = 128
VMEM_LIMIT_BYTES = 56 * 1024 * 1024
ROW_TILE = 1024
PROJ_COL_TILE = 1024
RET_CHUNK = 256
RET_HEAD_GROUP = 2
MOE_FF_TILE = 256
MOE_DMA_UNROLL = 8


def _cparams(*sem):
    return pltpu.CompilerParams(dimension_semantics=sem, vmem_limit_bytes=VMEM_LIMIT_BYTES)


def _dot(a, b):
    return jnp.dot(a, b, preferred_element_type=F32)


def _dot_nt(a, b):
    return lax.dot_general(a, b, (((1,), (1,)), ((), ())), preferred_element_type=F32)


def _dot_tn(a, b):
    return lax.dot_general(a, b, (((0,), (0,)), ((), ())), preferred_element_type=F32)


def _silu(x):
    return x / (1.0 + jnp.exp(-x))


def _rms_rows(x):
    return x * lax.rsqrt(jnp.mean(x * x, axis=-1, keepdims=True) + EPS)


def _norm_proj_kernel(x_ref, nw_ref, w_ref, o_ref, hn_ref):
    @pl.when(pl.program_id(1) == 0)
    def _():
        hn_ref[...] = (_rms_rows(x_ref[...]) * nw_ref[...]).astype(BF16)

    o_ref[...] = _dot(hn_ref[...], w_ref[...].astype(BF16)).astype(o_ref.dtype)


def _norm_proj_residue_kernel(x_ref, nw_ref, w_ref, o_ref, hn_ref, lanes_ref, *, dil):
    @pl.when(pl.program_id(1) == 0)
    def _():
        hn_ref[...] = (_rms_rows(x_ref[...]) * nw_ref[...]).astype(BF16)

    acc = _dot(hn_ref[...], w_ref[...].astype(BF16))
    per = acc.shape[0] // dil
    for c in range(acc.shape[1] // LANES):
        cols = slice(c * LANES, (c + 1) * LANES)
        lanes_ref[c] = acc[:, cols]
        for r in range(dil):
            o_ref[r * per:(r + 1) * per, cols] = lanes_ref[c, pl.ds(r, per, stride=dil), :].astype(o_ref.dtype)


def _norm_proj_rope_kernel(x_ref, nw_ref, w_ref, cos_ref, sin_ref, o_ref, hn_ref, *, n_q_tiles, n_rope_tiles, k_scale):
    j = pl.program_id(1)

    @pl.when(j == 0)
    def _():
        hn_ref[...] = (_rms_rows(x_ref[...]) * nw_ref[...]).astype(BF16)

    acc = _dot(hn_ref[...], w_ref[...].astype(BF16))

    @pl.when(j < n_rope_tiles)
    def _():
        c = cos_ref[...]
        s = sin_ref[...]
        half = c.shape[1]
        scale = jnp.where(j >= n_q_tiles, k_scale, 1.0).astype(F32)
        for hh in range(acc.shape[1] // (2 * half)):
            x1 = acc[:, 2 * hh * half:(2 * hh + 1) * half]
            x2 = acc[:, (2 * hh + 1) * half:(2 * hh + 2) * half]
            o_ref[:, 2 * hh * half:(2 * hh + 1) * half] = ((x1 * c - x2 * s) * scale).astype(o_ref.dtype)
            o_ref[:, (2 * hh + 1) * half:(2 * hh + 2) * half] = ((x2 * c + x1 * s) * scale).astype(o_ref.dtype)

    @pl.when(j >= n_rope_tiles)
    def _():
        o_ref[...] = acc.astype(o_ref.dtype)


def _norm_proj(x2, nw, w, *, n_out, tn, out_dtype, col_off=0, rope=None, residue_major=1):
    m, k = x2.shape
    w_stack, layer = w
    tm = min(ROW_TILE, m)
    grid = (m // tm, n_out // tn)
    off = col_off // tn
    in_specs = [pl.BlockSpec((tm, k), lambda i, j: (i, 0)),
                pl.BlockSpec((1, k), lambda i, j: (0, 0)),
                pl.BlockSpec((None, k, tn), lambda i, j: (layer, 0, j + off))]
    args = [x2, nw.reshape(1, k), w_stack]
    scratch = [pltpu.VMEM((tm, k), BF16)]
    if residue_major > 1:
        body = functools.partial(_norm_proj_residue_kernel, dil=residue_major)
        scratch.append(pltpu.VMEM((tn // LANES, tm, LANES), F32))
    elif rope is None:
        body = _norm_proj_kernel
    else:
        cos, sin, seq, n_q_cols, n_rope_cols, k_scale = rope
        nsb = seq // tm
        in_specs += [pl.BlockSpec((tm, cos.shape[1]), lambda i, j: (i % nsb, 0)),
                     pl.BlockSpec((tm, cos.shape[1]), lambda i, j: (i % nsb, 0))]
        args += [cos, sin]
        body = functools.partial(_norm_proj_rope_kernel, n_q_tiles=n_q_cols // tn, n_rope_tiles=n_rope_cols // tn,
                                 k_scale=k_scale)
    return pl.pallas_call(
        body, name="norm_proj", grid=grid, in_specs=in_specs,
        out_specs=pl.BlockSpec((tm, tn), lambda i, j: (i, j)),
        out_shape=jax.ShapeDtypeStruct((m, n_out), out_dtype),
        scratch_shapes=scratch,
        compiler_params=_cparams("parallel", "arbitrary"),
    )(*args)


def _out_proj_kernel(a_ref, w_ref, res_ref, o_ref):
    o_ref[...] = res_ref[...] + _dot(a_ref[...], w_ref[...].astype(BF16))


def _out_proj(a, w, res, *, tn=512):
    m, k = a.shape
    w_stack, layer = w
    n = w_stack.shape[2]
    tm = min(ROW_TILE, m)
    return pl.pallas_call(
        _out_proj_kernel, name="out_proj", grid=(m // tm, n // tn),
        in_specs=[pl.BlockSpec((tm, k), lambda i, j: (i, 0)),
                  pl.BlockSpec((None, k, tn), lambda i, j: (layer, 0, j)),
                  pl.BlockSpec((tm, tn), lambda i, j: (i, j))],
        out_specs=pl.BlockSpec((tm, tn), lambda i, j: (i, j)),
        out_shape=jax.ShapeDtypeStruct((m, n), F32),
        compiler_params=_cparams("parallel", "arbitrary"),
    )(a, w_stack, res)


def _final_norm_kernel(x_ref, nw_ref, o_ref):
    o_ref[...] = _rms_rows(x_ref[...]) * nw_ref[...]


def _final_norm(x2, nw):
    m, k = x2.shape
    tm = min(ROW_TILE, m)
    return pl.pallas_call(
        _final_norm_kernel, name="final_norm", grid=(m // tm,),
        in_specs=[pl.BlockSpec((tm, k), lambda i: (i, 0)), pl.BlockSpec((1, k), lambda i: (0, 0))],
        out_specs=pl.BlockSpec((tm, k), lambda i: (i, 0)),
        out_shape=jax.ShapeDtypeStruct((m, k), F32),
        compiler_params=_cparams("parallel"),
    )(x2, nw.reshape(1, k))


def _retention_kernel(q_ref, k_ref, v_ref, g_ref, dmat_ref, vec_ref, o_ref, state_ref, oacc_ref):
    sweep = pl.program_id(2)
    c = pl.program_id(3)
    nc = pl.num_programs(3)
    hg = range(RET_HEAD_GROUP)
    qs = [slice(h * RET_DK, (h + 1) * RET_DK) for h in hg]
    vs = [slice(h * RET_DV, (h + 1) * RET_DV) for h in hg]

    @pl.when(c == 0)
    def _():
        state_ref[...] = jnp.zeros_like(state_ref)

    @pl.when(sweep == 0)
    def _():
        scores = [_dot_nt(q_ref[:, qs[h]], k_ref[:, qs[h]]) for h in hg]
        inters = [_dot(q_ref[:, qs[h]], state_ref[h].astype(BF16)) for h in hg]
        kds = [(k_ref[:, qs[h]].astype(F32) * vec_ref[h][:, 2:3]).astype(BF16) for h in hg]
        upds = [_dot_tn(kds[h], v_ref[:, vs[h]]) for h in hg]
        probs = [(scores[h] * dmat_ref[h]).astype(BF16) for h in hg]
        intras = [_dot(probs[h], v_ref[:, vs[h]]) for h in hg]
        for h in hg:
            vec = vec_ref[h]
            oacc_ref[c, :, vs[h]] = intras[h] + inters[h] * vec[:, 0:1]
            state_ref[h] = state_ref[h] * vec[0:1, 4:5] + upds[h]

    @pl.when(sweep == 1)
    def _():
        inters = [_dot(q_ref[:, qs[h]], state_ref[h].astype(BF16)) for h in hg]
        kds = [(k_ref[:, qs[h]].astype(F32) * vec_ref[h][:, 3:4]).astype(BF16) for h in hg]
        upds = [_dot_tn(kds[h], v_ref[:, vs[h]]) for h in hg]
        for h in hg:
            vec = vec_ref[h]
            o = _rms_rows(oacc_ref[nc - 1 - c, :, vs[h]] + inters[h] * vec[:, 1:2])
            o_ref[:, vs[h]] = (o * _silu(g_ref[:, vs[h]].astype(F32))).astype(o_ref.dtype)
            state_ref[h] = state_ref[h] * vec[0:1, 5:6] + upds[h]


def _retention_tables(chunk):
    hh = jnp.arange(RET_HEADS, dtype=F32)
    lg_f = jnp.log1p(-jnp.exp2(-5.0 - hh))
    lg_b = jnp.log1p(-jnp.exp2(-(5.0 + RET_BWD_DECAY_OFFSET) - hh))
    t = jnp.arange(chunk, dtype=F32)
    diff = t[:, None] - t[None, :]
    dmat = jnp.where(diff[None] >= 0,
                     jnp.exp(jnp.maximum(diff, 0.0)[None] * lg_f[:, None, None]),
                     jnp.exp(jnp.maximum(-diff, 0.0)[None] * lg_b[:, None, None]))
    cols = [jnp.exp((t[None, :] + 1.0) * lg_f[:, None]),
            jnp.exp((chunk - t)[None, :] * lg_b[:, None]),
            jnp.exp((chunk - 1.0 - t)[None, :] * lg_f[:, None]),
            jnp.exp(t[None, :] * lg_b[:, None]),
            jnp.broadcast_to(jnp.exp(chunk * lg_f)[:, None], (RET_HEADS, chunk)),
            jnp.broadcast_to(jnp.exp(chunk * lg_b)[:, None], (RET_HEADS, chunk))]
    cols += [jnp.zeros((RET_HEADS, chunk), F32)] * 2
    return dmat, jnp.stack(cols, axis=-1)


def _retention_core(proj, batch, seq):
    cq = min(RET_CHUNK, seq)
    nc = seq // cq
    dmat, vec = _retention_tables(cq)
    g = RET_HEAD_GROUP
    n_hg = RET_HEADS // g
    kq = n_hg
    kv = 2 * RET_HEADS * RET_DK // (g * RET_DV)
    kg = kv + n_hg

    def row(b, s, c):
        return b * nc + c + s * (nc - 1 - 2 * c)

    def row_out(b, s, c):
        return b * nc + nc - 1 - c * s

    return pl.pallas_call(
        _retention_kernel, name="retention_core", grid=(batch, n_hg, 2, nc),
        in_specs=[pl.BlockSpec((cq, g * RET_DK), lambda b, h, s, c: (row(b, s, c), h)),
                  pl.BlockSpec((cq, g * RET_DK), lambda b, h, s, c: (row(b, s, c), kq + h)),
                  pl.BlockSpec((cq, g * RET_DV), lambda b, h, s, c: (row(b, s, c), kv + h)),
                  pl.BlockSpec((cq, g * RET_DV), lambda b, h, s, c: (row_out(b, s, c), kg + h)),
                  pl.BlockSpec((g, cq, cq), lambda b, h, s, c: (h, 0, 0)),
                  pl.BlockSpec((g, cq, 8), lambda b, h, s, c: (h, 0, 0))],
        out_specs=pl.BlockSpec((cq, g * RET_DV), lambda b, h, s, c: (row_out(b, s, c), h)),
        out_shape=jax.ShapeDtypeStruct((batch * seq, RET_HEADS * RET_DV), BF16),
        scratch_shapes=[pltpu.VMEM((g, RET_DK, RET_DV), F32), pltpu.VMEM((nc, cq, g * RET_DV), F32)],
        compiler_params=_cparams("parallel", "parallel", "arbitrary", "arbitrary"),
    )(proj, proj, proj, proj, dmat, vec)


def _rope_tables(seq, half):
    inv = ROPE_BASE ** (-np.arange(half, dtype=np.float64) / half)
    ang = np.arange(seq, dtype=np.float64)[:, None] * inv[None, :]
    return jnp.asarray(np.cos(ang), F32), jnp.asarray(np.sin(ang), F32)


def _retention_mixer(x2, nw, w_in, w_out, batch, seq):
    cos, sin = _rope_tables(seq, RET_DK // 2)
    qk_cols = 2 * RET_HEADS * RET_DK
    proj = _norm_proj(x2, nw, w_in, n_out=w_in[0].shape[2], tn=PROJ_COL_TILE, out_dtype=BF16,
                      rope=(cos, sin, seq, qk_cols // 2, qk_cols, RET_DK ** -0.5))
    o = _retention_core(proj, batch, seq)
    return _out_proj(o, w_out, x2)


def _split_bf16(x):
    hi = x.astype(BF16)
    return hi, (x - hi.astype(F32)).astype(BF16)


def _router_kernel(x_ref, nw_ref, wr_ref, hx_ref, afft_ref):
    half = x_ref.shape[1] // 2
    h = _rms_rows(x_ref[...]) * nw_ref[...]
    h_hi, h_lo = _split_bf16(h)
    bits = lax.bitcast_convert_type(h_hi.astype(F32), jnp.uint32)
    hx_ref[:, :half] = (bits[:, half:] & jnp.uint32(0xFFFF0000)) | (bits[:, :half] >> 16)
    w_hi, w_lo = _split_bf16(wr_ref[...])
    lg = _dot(h_hi, w_hi) + (_dot(h_lo, w_hi) + _dot(h_hi, w_lo))
    lane = lax.broadcasted_iota(jnp.int32, lg.shape, 1)
    lg = jnp.where(lane < N_EXPERTS, lg, NEG_INF)
    e = jnp.exp(lg - jnp.max(lg, axis=1, keepdims=True))
    aff = e / jnp.sum(e, axis=1, keepdims=True)
    hx_ref[:, half:] = lax.bitcast_convert_type(aff, jnp.uint32)
    afft_ref[0] = aff.T[:N_EXPERTS, :]


def _router(x2, nw, w_router, batch, seq):
    m, d = x2.shape
    tm = min(512, seq)
    nsb = seq // tm
    hw = d // 2 + LANES
    wr = jnp.pad(w_router, ((0, 0), (0, LANES - N_EXPERTS)))
    return pl.pallas_call(
        _router_kernel, name="moe_router", grid=(m // tm,),
        in_specs=[pl.BlockSpec((tm, d), lambda i: (i, 0)),
                  pl.BlockSpec((1, d), lambda i: (0, 0)),
                  pl.BlockSpec((d, LANES), lambda i: (0, 0))],
        out_specs=[pl.BlockSpec((tm, hw), lambda i: (i, 0)),
                   pl.BlockSpec((1, N_EXPERTS, tm), lambda i: (i // nsb, 0, i % nsb))],
        out_shape=[jax.ShapeDtypeStruct((m, hw), jnp.uint32),
                   jax.ShapeDtypeStruct((batch, N_EXPERTS, seq), F32)],
        compiler_params=_cparams("parallel"),
    )(x2, nw.reshape(1, d), wr)


def _topk_kernel(aff_ref, idx_ref, loc_ref, off_ref, end_ref, *, cap, n_groups, n_blk):
    n = n_groups * n_blk
    bits = lax.bitcast_convert_type(aff_ref[...], jnp.int32)
    ri = lax.broadcasted_iota(jnp.int32, (n, n), 0)
    ci = lax.broadcasted_iota(jnp.int32, (n, n), 1)
    same = (ri // n_blk) == (ci // n_blk)
    grp_ones = same.astype(BF16)
    grp_before = (same & (ci < ri)).astype(BF16)
    li = lax.broadcasted_iota(jnp.int32, (LANES, LANES), 0)
    lj = lax.broadcasted_iota(jnp.int32, (LANES, LANES), 1)
    incl = (li <= lj).astype(BF16)
    ones = jnp.ones((LANES, LANES), BF16)

    def row_total(mask):
        return _dot(mask.astype(BF16), ones)

    def group_count(mask):
        return _dot(grp_ones, row_total(mask).astype(BF16))

    member = (lax.broadcasted_iota(jnp.int32, (n_groups, n), 1) // n_blk
              == lax.broadcasted_iota(jnp.int32, (n_groups, n), 0)).astype(BF16)
    spread = (lax.broadcasted_iota(jnp.int32, (n, n_groups), 0) // n_blk
              == lax.broadcasted_iota(jnp.int32, (n, n_groups), 1)).astype(BF16)

    def search(i, tau):
        cand = tau | jnp.left_shift(jnp.int32(1), 30 - i)
        per_lane = _dot(member, (bits >= cand).astype(BF16))
        take = (jnp.sum(per_lane, axis=1, keepdims=True) >= cap).astype(BF16)
        take_rows = _dot(spread, jnp.broadcast_to(take, (n_groups, LANES)))
        return jnp.where(take_rows > 0.5, cand, tau)

    tau = lax.fori_loop(0, 31, search, jnp.zeros((n, LANES), jnp.int32))

    def cumsum(mask):
        mb = mask.astype(BF16)
        tot = _dot(mb, ones)
        return _dot(mb, incl), _dot(grp_before, tot.astype(BF16)), tot

    gt = bits > tau
    eq = bits == tau
    need = cap - group_count(gt)
    eq_loc, eq_off, _ = cumsum(eq)
    sel = gt | (eq & (eq_loc + eq_off <= need))
    loc, off, tot = cumsum(sel)
    loc_ref[...] = loc
    off_ref[...] = off
    end_ref[...] = off + tot

    slot = lax.broadcasted_iota(jnp.int32, (cap, LANES), 0).astype(F32)
    lane = lax.broadcasted_iota(jnp.int32, (cap, LANES), 1)
    eye = lax.broadcasted_iota(jnp.int32, (n_blk, LANES), 0) == lax.broadcasted_iota(jnp.int32, (n_blk, LANES), 1)
    pad = jnp.zeros((LANES - n_blk, LANES), BF16)

    def compact(g, acc):
        rows = pl.ds(pl.multiple_of(g * n_blk, n_blk), n_blk)
        end_row = jnp.sum(jnp.where(eye, end_ref[rows, :], 0.0), axis=0, keepdims=True)
        off_row = jnp.sum(jnp.where(eye, off_ref[rows, :], 0.0), axis=0, keepdims=True)
        blk = jnp.sum(((end_row <= slot) & (lane < n_blk)).astype(F32), axis=1, keepdims=True)
        onehot = lane == blk.astype(jnp.int32)
        loc_pad = jnp.concatenate([loc_ref[rows, :].astype(BF16), pad], axis=0)
        in_blk = _dot(onehot.astype(BF16), loc_pad)
        rank = slot - jnp.sum(jnp.where(onehot, off_row, 0.0), axis=1, keepdims=True)
        pos = blk * LANES + jnp.sum((in_blk <= rank).astype(F32), axis=1, keepdims=True)
        return jnp.where(lane == g, pos.astype(jnp.int32), acc)

    idx_ref[...] = lax.fori_loop(0, n_groups, compact, jnp.zeros((cap, LANES), jnp.int32))


def _topk(afft, cap):
    batch, n_e, seq = afft.shape
    n_groups = batch * n_e
    n_blk = seq // LANES
    n = n_groups * n_blk
    assert n_groups <= LANES and n_blk <= LANES
    idx = pl.pallas_call(
        functools.partial(_topk_kernel, cap=cap, n_groups=n_groups, n_blk=n_blk), name="moe_topk",
        out_shape=jax.ShapeDtypeStruct((cap, LANES), jnp.int32),
        scratch_shapes=[pltpu.VMEM((n, LANES), F32)] * 3,
        compiler_params=pltpu.CompilerParams(vmem_limit_bytes=VMEM_LIMIT_BYTES),
    )(afft.reshape(n, LANES))
    return idx[:, :n_groups].T.reshape(batch, n_e, cap)


def _moe_ffn_kernel(idx_ref, hx_hbm, wg_ref, wu_ref, wd_ref, xres_hbm, out_hbm,
                    hbuf, xbf, acc, rbuf, gate, sem_h, sem_r, sem_s):
    del xres_hbm
    e = pl.program_id(0)
    f = pl.program_id(1)
    n_e = pl.num_programs(0)
    nf = pl.num_programs(1)
    rows = xbf.shape[0]
    d = xbf.shape[1]
    per_step = rows // nf
    slot = e % 2
    base = e * rows
    nxt = ((e + 1) % n_e) * rows

    def h_copy(row_id, fq, u, s):
        return pltpu.make_async_copy(hx_hbm.at[pl.ds(row_id, 1), :], hbuf.at[s, fq, pl.ds(u, 1), :], sem_h.at[s])

    def wait_all(buf, sem):
        pltpu.make_async_copy(buf, buf, sem).wait()

    @pl.when((e == 0) & (f == 0))
    def _():
        def start(r, carry):
            h_copy(idx_ref[r], r // per_step, r % per_step, 0).start()
            return carry

        lax.fori_loop(0, rows, start, 0, unroll=MOE_DMA_UNROLL)

    @pl.when(f == 0)
    def _():
        wait_all(hbuf.at[slot], sem_h.at[slot])
        half = d // 2
        packed = hbuf[slot].reshape(rows, half + LANES)
        words = packed[:, :half]
        xbf[:, :half] = lax.bitcast_convert_type(words << 16, F32).astype(BF16)
        xbf[:, half:] = lax.bitcast_convert_type(words & jnp.uint32(0xFFFF0000), F32).astype(BF16)
        lane = lax.broadcasted_iota(jnp.int32, (rows, LANES), 1)
        aff = lax.bitcast_convert_type(packed[:, half:], F32)
        gate[...] = jnp.sum(jnp.where(lane == e, aff, 0.0), axis=1, keepdims=True)

    for u in range(per_step):
        r = f * per_step + u
        h_copy(idx_ref[nxt + r], f, u, 1 - slot).start()
        pltpu.make_async_copy(out_hbm.at[pl.ds(idx_ref[base + r], 1), :], rbuf.at[f, pl.ds(u, 1), :],
                              sem_r.at[0]).start()

    x = xbf[...]
    hid = (_silu(_dot(x, wg_ref[...].astype(BF16))) * _dot(x, wu_ref[...].astype(BF16))).astype(BF16)
    part = _dot(hid, wd_ref[...].astype(BF16))

    n_steps = rbuf.shape[0]
    if n_steps > 1:
        @pl.when(f == 0)
        def _():
            acc[...] = part

        @pl.when((f > 0) & (f < nf - 1))
        def _():
            acc[...] += part

    @pl.when(f == nf - 1)
    def _():
        wait_all(rbuf, sem_r.at[0])
        total = acc[...] + part if n_steps > 1 else part
        rbuf[...] = rbuf[...] + (total * gate[...]).reshape(rbuf.shape)

        for fq in range(rbuf.shape[0]):
            def start(u, carry, fq=fq):
                pltpu.make_async_copy(rbuf.at[fq, pl.ds(u, 1), :],
                                      out_hbm.at[pl.ds(idx_ref[base + fq * per_step + u], 1), :], sem_s.at[0]).start()
                return carry

            lax.fori_loop(0, per_step, start, 0, unroll=MOE_DMA_UNROLL)
        wait_all(rbuf, sem_s.at[0])

    @pl.when((e == n_e - 1) & (f == nf - 1))
    def _():
        wait_all(hbuf.at[1 - slot], sem_h.at[1 - slot])


def _moe_ffn(rows_idx, hx, x2, w_gate, w_up, w_down, layer):
    m, d = x2.shape
    _, n_e, _, ff = w_gate.shape
    rows = rows_idx.shape[0] // n_e
    tf = min(MOE_FF_TILE, ff)
    nf = ff // tf
    grid_spec = pltpu.PrefetchScalarGridSpec(
        num_scalar_prefetch=1, grid=(n_e, nf),
        in_specs=[pl.BlockSpec(memory_space=pl.ANY),
                  pl.BlockSpec((None, None, d, tf), lambda e, f, idx: (layer, e, 0, f)),
                  pl.BlockSpec((None, None, d, tf), lambda e, f, idx: (layer, e, 0, f)),
                  pl.BlockSpec((None, None, tf, d), lambda e, f, idx: (layer, e, f, 0)),
                  pl.BlockSpec(memory_space=pl.ANY)],
        out_specs=pl.BlockSpec(memory_space=pl.ANY),
        scratch_shapes=[pltpu.VMEM((2, nf, rows // nf, hx.shape[1]), jnp.uint32), pltpu.VMEM((rows, d), BF16),
                        pltpu.VMEM((rows, d), F32), pltpu.VMEM((nf, rows // nf, d), F32),
                        pltpu.VMEM((rows, 1), F32),
                        pltpu.SemaphoreType.DMA((2,)), pltpu.SemaphoreType.DMA((1,)),
                        pltpu.SemaphoreType.DMA((1,))])
    return pl.pallas_call(
        _moe_ffn_kernel, name="moe_ffn", grid_spec=grid_spec,
        out_shape=jax.ShapeDtypeStruct((m, d), F32),
        input_output_aliases={5: 0},
        compiler_params=_cparams("arbitrary", "arbitrary"),
    )(rows_idx, hx, w_gate, w_up, w_down, x2)


def _moe(x2, nw, w_router, w_gate, w_up, w_down, layer, batch, seq):
    cap = CAPACITY_FACTOR * seq // N_EXPERTS
    hx, afft = _router(x2, nw, w_router, batch, seq)
    idx = _topk(afft, cap)
    rows_idx = idx + (jnp.arange(batch, dtype=jnp.int32) * seq)[:, None, None]
    rows_idx = jnp.transpose(rows_idx, (1, 0, 2)).reshape(-1)
    return _moe_ffn(rows_idx, hx, x2, w_gate, w_up, w_down, layer)


DIL_BLOCK = 64


def _t5_bucket(rel):
    nb = REL_BUCKETS // 2
    max_exact = nb // 2
    ret = jnp.where(rel > 0, nb, 0)
    n = jnp.abs(rel)
    nf = jnp.maximum(n, 1).astype(F32)
    large = max_exact + (jnp.log(nf / max_exact) / math.log(REL_MAX_DIST / max_exact)
                         * (nb - max_exact)).astype(jnp.int32)
    large = jnp.minimum(large, nb - 1)
    return ret + jnp.where(n < max_exact, n, large)


def _dil_bias_kernel(table_ref, bucket_ref, o_ref):
    col = pl.program_id(0)
    bkt = bucket_ref[0]
    acc = jnp.zeros(bkt.shape, F32)
    for b in range(REL_BUCKETS):
        acc = jnp.where(bkt == b, table_ref[b, col], acc)
    s = lax.broadcasted_iota(jnp.int32, bkt.shape, 0)
    t = lax.broadcasted_iota(jnp.int32, bkt.shape, 1)
    o_ref[0] = jnp.where((t >= s) & (t <= s + 2 * DIL_BLOCK), acc, NEG_INF)


def _dil_bias(rel_table):
    qb = DIL_BLOCK
    rel_steps = jnp.arange(3 * qb)[None, :] - jnp.arange(qb)[:, None] - qb
    buckets = jnp.stack([_t5_bucket(rel_steps * dil) for _, dil in DIL_PATTERNS]).astype(jnp.int32)
    n_col = rel_table.shape[1]
    return pl.pallas_call(
        _dil_bias_kernel, name="dil_bias", grid=(n_col,),
        in_specs=[pl.BlockSpec(memory_space=pltpu.SMEM),
                  pl.BlockSpec((1, qb, 3 * qb), lambda c: (c // DIL_HEADS, 0, 0))],
        out_specs=pl.BlockSpec((1, qb, 3 * qb), lambda c: (c, 0, 0)),
        out_shape=jax.ShapeDtypeStruct((n_col, qb, 3 * qb), F32),
        compiler_params=_cparams("parallel"),
    )(rel_table, buckets)


def _dil_attn_kernel(q_ref, kp_ref, kc_ref, kn_ref, vp_ref, vc_ref, vn_ref, bias_ref, o_ref, lse_ref):
    jb = pl.program_id(2)
    nb = pl.num_programs(2)
    qb = q_ref.shape[0]
    t = lax.broadcasted_iota(jnp.int32, (qb, 3 * qb), 1)
    in_seq = ((jb > 0) | (t >= qb)) & ((jb < nb - 1) | (t < 2 * qb))
    kw = jnp.concatenate([kp_ref[...], kc_ref[...], kn_ref[...]], axis=0)
    vw = jnp.concatenate([vp_ref[...], vc_ref[...], vn_ref[...]], axis=0)
    lane = lax.broadcasted_iota(jnp.int32, (qb, LANES), 1)
    lse = jnp.zeros((qb, LANES), F32)
    heads = [slice(h * DIL_DH, (h + 1) * DIL_DH) for h in range(DIL_HEADS)]
    scores = [_dot_nt(q_ref[:, hs], kw[:, hs]) for hs in heads]
    probs, dens = [], []
    for h, s in enumerate(scores):
        s = jnp.where(in_seq, s * (DIL_DH ** -0.5) + bias_ref[h], NEG_INF)
        m = jnp.max(s, axis=1, keepdims=True)
        p = jnp.exp(s - m)
        den = jnp.sum(p, axis=1, keepdims=True)
        probs.append(p.astype(BF16))
        dens.append(den)
        lse = jnp.where(lane == h, m + jnp.log(den), lse)
    outs = [_dot(p, vw[:, hs]) for p, hs in zip(probs, heads)]
    for hs, o, den in zip(heads, outs, dens):
        o_ref[:, hs] = (o / den).astype(o_ref.dtype)
    lse_ref[0] = lse[:, :DIL_HEADS]


def _dil_group(proj, bias, gi, dil, batch, seq):
    qb = DIL_BLOCK
    nb = seq // dil // qb
    hw = DIL_HEADS * DIL_DH
    m = batch * seq
    tm = min(ROW_TILE, m)
    bpt = tm // (dil * qb)
    tiles_per_seq = seq // tm

    def row_block(b, r, j):
        return (b * tiles_per_seq + j // bpt) * (tm // qb) + r * bpt + j % bpt

    def spec(which, shift):
        def imap(b, r, j):
            return (row_block(b, r, jnp.clip(j + shift, 0, nb - 1)), which)
        return pl.BlockSpec((qb, hw), imap)

    o, lse = pl.pallas_call(
        _dil_attn_kernel, name=f"dil_attn_d{dil}", grid=(batch, dil, nb),
        in_specs=[spec(0, 0), spec(1, -1), spec(1, 0), spec(1, 1), spec(2, -1), spec(2, 0), spec(2, 1),
                  pl.BlockSpec((DIL_HEADS, qb, 3 * qb), lambda b, r, j: (gi, 0, 0))],
        out_specs=[pl.BlockSpec((qb, hw), lambda b, r, j: (row_block(b, r, j), 0)),
                   pl.BlockSpec((1, qb, DIL_HEADS), lambda b, r, j: (row_block(b, r, j), 0, 0))],
        out_shape=[jax.ShapeDtypeStruct((m, hw), BF16),
                   jax.ShapeDtypeStruct((m // qb, qb, DIL_HEADS), F32)],
        compiler_params=_cparams("parallel", "parallel", "arbitrary"),
    )(proj, proj, proj, proj, proj, proj, proj, bias)
    lse = lse.reshape(m // tm, dil, tm // dil, DIL_HEADS)
    return o, jnp.transpose(lse, (0, 2, 1, 3)).reshape(m, DIL_HEADS)


def _dil_out_kernel(o0_ref, o1_ref, o2_ref, l0_ref, l1_ref, l2_ref, w_ref, res_ref, out_ref, comb_ref, tok_ref, *,
                    dils):
    @pl.when(pl.program_id(1) == 0)
    def _():
        l0, l1, l2 = l0_ref[...], l1_ref[...], l2_ref[...]
        m = jnp.maximum(jnp.maximum(l0, l1), l2)
        e0, e1, e2 = jnp.exp(l0 - m), jnp.exp(l1 - m), jnp.exp(l2 - m)
        den = e0 + e1 + e2
        wts = (e0 / den, e1 / den, e2 / den)
        rows = comb_ref.shape[0]
        for h in range(DIL_HEADS):
            hs = slice(h * DIL_DH, (h + 1) * DIL_DH)
            total = None
            for g, (o_ref, dil) in enumerate(zip((o0_ref, o1_ref, o2_ref), dils)):
                if dil == 1:
                    o = o_ref[:, hs].astype(F32)
                else:
                    per = rows // dil
                    for r in range(dil):
                        tok_ref[g, pl.ds(r, per, stride=dil), :] = o_ref[r * per:(r + 1) * per, hs].astype(F32)
                    o = tok_ref[g]
                term = wts[g][:, h:h + 1] * o
                total = term if total is None else total + term
            comb_ref[:, hs] = total.astype(BF16)

    out_ref[...] = res_ref[...] + _dot(comb_ref[...], w_ref[...].astype(BF16))


def _dil_out(os_, lses, w, res, *, tn=512):
    m, k = os_[0].shape
    w_stack, layer = w
    n = w_stack.shape[2]
    tm = min(ROW_TILE, m)
    ospec = pl.BlockSpec((tm, k), lambda i, j: (i, 0))
    lspec = pl.BlockSpec((tm, DIL_HEADS), lambda i, j: (i, 0))
    dils = tuple(dil for _, dil in DIL_PATTERNS)
    return pl.pallas_call(
        functools.partial(_dil_out_kernel, dils=dils), name="dil_out", grid=(m // tm, n // tn),
        in_specs=[ospec, ospec, ospec, lspec, lspec, lspec,
                  pl.BlockSpec((None, k, tn), lambda i, j: (layer, 0, j)),
                  pl.BlockSpec((tm, tn), lambda i, j: (i, j))],
        out_specs=pl.BlockSpec((tm, tn), lambda i, j: (i, j)),
        out_shape=jax.ShapeDtypeStruct((m, n), F32),
        scratch_shapes=[pltpu.VMEM((tm, k), BF16), pltpu.VMEM((len(dils), tm, DIL_DH), F32)],
        compiler_params=_cparams("parallel", "arbitrary"),
    )(*os_, *lses, w_stack, res)


def _dilated_mixer(x2, nw, w_in, w_out, rel_table, batch, seq):
    gw = w_in[0].shape[2] // len(DIL_PATTERNS)
    bias = _dil_bias(rel_table)
    outs = []
    for gi, (_, dil) in enumerate(DIL_PATTERNS):
        proj = _norm_proj(x2, nw, w_in, n_out=gw, tn=PROJ_COL_TILE, out_dtype=BF16, col_off=gi * gw,
                          residue_major=dil)
        outs.append(_dil_group(proj, bias, gi, dil, batch, seq))
    return _dil_out([o for o, _ in outs], [l for _, l in outs], w_out, x2)


DN_CHUNK = 64
DN_REP = DN_V_HEADS // DN_K_HEADS
DN_INST = 2 * DN_REP
DN_ROWS = DN_INST * DN_CHUNK
DN_PREP_GROUP = 8
CONV_TILE = 256
CONV_HALO = 8


def _split3_bf16(x):
    hi = x.astype(BF16)
    r = x - hi.astype(F32)
    mid = r.astype(BF16)
    return hi, mid, (r - mid.astype(F32)).astype(BF16)


def _softplus(x):
    return jnp.maximum(x, 0.0) + jnp.log1p(jnp.exp(-jnp.abs(x)))


def _deltanet_kernel(q_ref, k_ref, v_ref, z_ref, tail_ref, cwq_ref, cwk_ref, cwv_ref, alog_ref, dtb_ref, nw_ref,
                     o_ref, pad_ref, qn_ref, kn_ref, vc_ref, mq_ref, bo_ref, state_ref, oacc_ref):
    seq = q_ref.shape[0]
    nc = seq // DN_CHUNK
    c64 = DN_CHUNK
    dk = DN_DK

    n_pad = pad_ref.shape[0]
    pad_ref[:, 0:CONV_HALO, :] = jnp.zeros((n_pad, CONV_HALO, dk), F32)
    pad_ref[:, CONV_HALO + seq:, :] = jnp.zeros((n_pad, CONV_HALO, dk), F32)

    def conv_into(jobs):
        ws = []
        for p, (src_ref, col, w_ref, _, _) in enumerate(jobs):
            pad_ref[p, CONV_HALO:CONV_HALO + seq, :] = src_ref[:, col:col + dk].astype(F32)
            ws.append(w_ref[:, col:col + dk])

        def tile(r, carry):
            start = pl.multiple_of(r * CONV_TILE, CONV_TILE)
            ys = []
            for p in range(len(jobs)):
                win = pad_ref[p, pl.ds(start, CONV_TILE + 2 * CONV_HALO), :]
                y = jnp.zeros((CONV_TILE, dk), F32)
                for j in range(DN_CONV):
                    lo = CONV_HALO + j - DN_CONV // 2
                    y = y + win[lo:lo + CONV_TILE, :] * ws[p][j:j + 1, :]
                ys.append(_silu(y))
            for y, (_, col, _, dst_ref, l2_scale) in zip(ys, jobs):
                if l2_scale is not None:
                    y = y * (lax.rsqrt(jnp.sum(y * y, axis=1, keepdims=True) + EPS) * l2_scale)
                dst_ref[pl.ds(start, CONV_TILE), col:col + dk] = y.astype(dst_ref.dtype)
            return carry

        lax.fori_loop(0, seq // CONV_TILE, tile, 0)

    conv_into([(q_ref, 0, cwq_ref, qn_ref, DN_DK ** -0.5), (k_ref, 0, cwk_ref, kn_ref, 1.0)])
    conv_into([(v_ref, vl * DN_DV, cwv_ref, vc_ref, None) for vl in range(DN_REP)])

    rr = lax.broadcasted_iota(jnp.int32, (DN_ROWS, DN_ROWS), 0)
    cc = lax.broadcasted_iota(jnp.int32, (DN_ROWS, DN_ROWS), 1)
    same = (rr // c64) == (cc // c64)
    fwd_rows = rr < DN_REP * c64
    strict = same & ((fwd_rows & (rr > cc)) | (jnp.logical_not(fwd_rows) & (rr < cc)))
    eye = (rr == cc)
    eye_f = eye.astype(F32)
    r64 = lax.broadcasted_iota(jnp.int32, (c64, DN_ROWS), 0)
    c64i = lax.broadcasted_iota(jnp.int32, (c64, DN_ROWS), 1)
    eye_row = (r64 == (c64i % c64)).astype(F32)
    li = lax.broadcasted_iota(jnp.int32, (c64, c64), 0)
    lj = lax.broadcasted_iota(jnp.int32, (c64, c64), 1)
    tril = (lj <= li).astype(BF16)
    triu = (lj >= li).astype(BF16)
    lane_all = lax.broadcasted_iota(jnp.int32, (c64, LANES), 1)
    hk = pl.program_id(1)
    beta_cols = [(i // DN_REP) * DN_V_HEADS + hk * DN_REP + i % DN_REP for i in range(DN_INST)]
    gate_cols = [2 * DN_V_HEADS + col for col in beta_cols]
    row_chain = lax.broadcasted_iota(jnp.int32, (1, DN_ROWS), 1) // c64
    tall_same = (lax.broadcasted_iota(jnp.int32, (DN_INST * dk, DN_ROWS), 0) // dk
                 == lax.broadcasted_iota(jnp.int32, (DN_INST * dk, DN_ROWS), 1) // c64)
    e_r = lax.broadcasted_iota(jnp.int32, (DN_INST * dk, dk), 0)
    e_c = lax.broadcasted_iota(jnp.int32, (DN_INST * dk, dk), 1)
    eye_tall = ((e_r % dk) == e_c).astype(F32)
    neg_a = -jnp.exp(alog_ref[...])
    dtb = dtb_ref[...]

    def stack(cols):
        return jnp.concatenate(cols, axis=0)

    def blockdiag(rows_):
        return jnp.where(same, jnp.concatenate([rows_] * DN_INST, axis=0), 0.0)

    def chunk_maps(c):
        rows = pl.ds(pl.multiple_of(c * c64, c64), c64)
        tl = tail_ref[rows, :]
        beta_all = 1.0 / (1.0 + jnp.exp(-tl))
        g_all = neg_a * _softplus(tl + dtb)
        parts = _split3_bf16(g_all)
        gcf = _dot(tril, parts[0]) + (_dot(tril, parts[1]) + _dot(tril, parts[2]))
        gcb = _dot(triu, parts[0]) + (_dot(triu, parts[1]) + _dot(triu, parts[2]))
        yield

        def pick(a, col):
            return jnp.sum(jnp.where(lane_all == col, a, 0.0), axis=1, keepdims=True)

        beta_b = jnp.broadcast_to(stack([pick(beta_all, beta_cols[i]) for i in range(DN_INST)]), (DN_ROWS, dk))
        gc_b = jnp.broadcast_to(stack([pick(gcf if i < DN_REP else gcb, gate_cols[i]) for i in range(DN_INST)]),
                                (DN_ROWS, dk))
        g_tot = jnp.sum(g_all, axis=0, keepdims=True)
        gtot = [jnp.sum(jnp.where(lane_all[0:1] == gate_cols[i], g_tot, 0.0), axis=1, keepdims=True)
                for i in range(DN_INST)]
        gtot_b = stack([jnp.broadcast_to(t, (c64, dk)) for t in gtot])
        egc_b = jnp.exp(gc_b)

        kb = kn_ref[rows, :]
        qb = qn_ref[rows, :]
        vb = vc_ref[rows, :]
        k4 = stack([kb.astype(F32)] * DN_INST)
        q4 = stack([qb.astype(F32)] * DN_INST)
        v4 = stack([vb[:, (i % DN_REP) * DN_DV:(i % DN_REP + 1) * DN_DV] for i in range(DN_INST)]).astype(F32)
        k4t = k4.T

        gmat = jnp.concatenate([gc_b] * (DN_ROWS // dk), axis=1)
        grow = jnp.sum(jnp.where(eye, gmat, 0.0), axis=0, keepdims=True)
        decay = jnp.exp(jnp.where(strict, gmat - grow, NEG_INF))
        k4t_b = k4t.astype(BF16)
        kk_row = _dot(kb, k4t_b)
        qk_row = _dot(qb, k4t_b)
        yield
        n_bd = jnp.concatenate([beta_b] * (DN_ROWS // dk), axis=1) * stack([kk_row] * DN_INST) * decay
        qk_bd = stack([qk_row] * DN_INST) * (decay + eye_f)

        r_pow = -(n_bd[0:c64] + n_bd[c64:2 * c64] + n_bd[2 * c64:3 * c64] + n_bd[3 * c64:4 * c64])
        t_row = eye_row + r_pow
        r_pow = _dot(r_pow.astype(BF16), (-n_bd).astype(BF16))
        yield
        for _ in range(4):
            p_bd = blockdiag(r_pow).astype(BF16)
            rt = _dot(jnp.concatenate([r_pow, t_row], axis=0).astype(BF16), p_bd)
            yield
            r_pow = rt[0:c64]
            t_row = t_row + rt[c64:]
        t_row = t_row + _dot(t_row.astype(BF16), blockdiag(r_pow).astype(BF16))
        yield

        rhs = jnp.concatenate([k4 * (beta_b * egc_b), v4 * beta_b], axis=1).astype(BF16)
        wu = _dot(blockdiag(t_row).astype(BF16), rhs).astype(BF16)
        yield
        qk_wu = _dot(qk_bd.astype(BF16), wu)
        gtot_row = jnp.zeros((1, DN_ROWS), F32)
        for i in range(DN_INST):
            gtot_row = jnp.where(row_chain == i, gtot[i], gtot_row)
        kdt = k4t * jnp.exp(gtot_row - grow)
        kdt_wide = jnp.where(tall_same, stack([kdt] * DN_INST), 0.0).astype(BF16)
        kw = _dot(kdt_wide, wu)
        yield
        qp = (q4 * egc_b - qk_wu[:, :dk]).astype(BF16)
        op = qk_wu[:, dk:].astype(BF16)
        dl = stack([jnp.broadcast_to(jnp.exp(t), (dk, dk)) for t in gtot])
        return qp, op, (dl * eye_tall - kw[:, :dk]).astype(BF16), kw[:, dk:].astype(BF16)

    def run_interleaved(gens):
        results = [None] * len(gens)
        while any(r is None for r in results):
            for u, gen in enumerate(gens):
                if results[u] is None:
                    try:
                        next(gen)
                    except StopIteration as done:
                        results[u] = done.value
        return results

    def prep(g, carry):
        cs = [g * DN_PREP_GROUP + u for u in range(DN_PREP_GROUP)]
        maps = run_interleaved([chunk_maps(c) for c in cs])
        for c, (qp, op, mt, bb) in zip(cs, maps):
            mq_ref[c] = stack([x for i in range(DN_INST)
                               for x in (mt[i * dk:(i + 1) * dk], qp[i * c64:(i + 1) * c64])])
            bo_ref[c] = stack([x for i in range(DN_INST)
                               for x in (bb[i * dk:(i + 1) * dk], op[i * c64:(i + 1) * c64])])
        return carry

    lax.fori_loop(0, nc // DN_PREP_GROUP, prep, 0)

    state_ref[...] = jnp.zeros_like(state_ref)
    oacc_ref[...] = jnp.zeros_like(oacc_ref)

    def sweep(j, carry):
        for i in range(DN_INST):
            c = j if i < DN_REP else nc - 1 - j
            span = slice(i * (dk + c64), (i + 1) * (dk + c64))
            res = _dot(mq_ref[c, span, :], state_ref[i].astype(BF16)) + bo_ref[c, span, :].astype(F32)
            state_ref[i] = res[:dk]
            rows = pl.ds(pl.multiple_of(c * c64, c64), c64)
            vcols = slice((i % DN_REP) * DN_DV, (i % DN_REP + 1) * DN_DV)
            oacc_ref[rows, vcols] = oacc_ref[rows, vcols] + res[dk:]
        return carry

    lax.fori_loop(0, nc, sweep, 0)

    def finish(r, carry):
        rows = pl.ds(pl.multiple_of(r * CONV_TILE, CONV_TILE), CONV_TILE)
        for vl in range(DN_REP):
            vcols = slice(vl * DN_DV, (vl + 1) * DN_DV)
            o = _rms_rows(oacc_ref[rows, vcols]) * nw_ref[...]
            o_ref[rows, vcols] = (o * _silu(z_ref[rows, vcols].astype(F32))).astype(o_ref.dtype)
        return carry

    lax.fori_loop(0, seq // CONV_TILE, finish, 0)


def _deltanet_core(proj, tail, conv_w, a_log, dt_bias, norm_w, batch, seq):
    gates = 2 * DN_V_HEADS
    alog = jnp.concatenate([jnp.zeros((gates,), F32), a_log.reshape(gates)]).reshape(1, LANES)
    dtb = jnp.concatenate([jnp.zeros((gates,), F32), dt_bias.reshape(gates)]).reshape(1, LANES)
    nc = seq // DN_CHUNK
    assert seq % (DN_CHUNK * DN_PREP_GROUP) == 0 and seq % CONV_TILE == 0
    kq = DN_K_HEADS
    vw = DN_REP * DN_DV
    kv = 2 * DN_K_HEADS * DN_DK // vw
    kz = kv + DN_K_HEADS
    once = pl.Buffered(1)
    return pl.pallas_call(
        _deltanet_kernel, name="deltanet_core", grid=(batch, DN_K_HEADS),
        in_specs=[pl.BlockSpec((seq, DN_DK), lambda b, h: (b, h), pipeline_mode=once),
                  pl.BlockSpec((seq, DN_DK), lambda b, h: (b, kq + h), pipeline_mode=once),
                  pl.BlockSpec((seq, vw), lambda b, h: (b, kv + h), pipeline_mode=once),
                  pl.BlockSpec((seq, vw), lambda b, h: (b, kz + h), pipeline_mode=once),
                  pl.BlockSpec((seq, LANES), lambda b, h: (b, 0), pipeline_mode=once),
                  pl.BlockSpec((DN_CONV, DN_DK), lambda b, h: (0, h)),
                  pl.BlockSpec((DN_CONV, DN_DK), lambda b, h: (0, kq + h)),
                  pl.BlockSpec((DN_CONV, vw), lambda b, h: (0, kv + h)),
                  pl.BlockSpec((1, LANES), lambda b, h: (0, 0)),
                  pl.BlockSpec((1, LANES), lambda b, h: (0, 0)),
                  pl.BlockSpec((1, DN_DV), lambda b, h: (0, 0))],
        out_specs=pl.BlockSpec((seq, vw), lambda b, h: (b, h)),
        out_shape=jax.ShapeDtypeStruct((batch * seq, DN_V_HEADS * DN_DV), BF16),
        scratch_shapes=[pltpu.VMEM((2, seq + 2 * CONV_HALO, DN_DK), F32),
                        pltpu.VMEM((seq, DN_DK), BF16), pltpu.VMEM((seq, DN_DK), BF16), pltpu.VMEM((seq, vw), BF16),
                        pltpu.VMEM((nc, DN_INST * (DN_DK + DN_CHUNK), DN_DK), BF16),
                        pltpu.VMEM((nc, DN_INST * (DN_DK + DN_CHUNK), DN_DV), BF16),
                        pltpu.VMEM((DN_INST, DN_DK, DN_DV), F32), pltpu.VMEM((seq, vw), F32)],
        compiler_params=_cparams("parallel", "parallel"),
    )(proj, proj, proj, proj, tail, conv_w, conv_w, conv_w, alog, dtb, norm_w.reshape(1, DN_DV))


def _deltanet_mixer(x2, nw, w_in, conv_w, a_log, dt_bias, norm_w, w_out, batch, seq):
    conv_dim = 2 * DN_K_HEADS * DN_DK + DN_V_HEADS * DN_DV
    main = conv_dim + DN_V_HEADS * DN_DV
    proj = _norm_proj(x2, nw, w_in, n_out=main, tn=PROJ_COL_TILE, out_dtype=BF16)
    tail = _norm_proj(x2, nw, w_in, n_out=LANES, tn=LANES, out_dtype=F32, col_off=main)
    o = _deltanet_core(proj, tail, conv_w, a_log, dt_bias, norm_w, batch, seq)
    return _out_proj(o, w_out, x2)


def kernel(x, norm_mix_w, norm_ffn_w, final_norm_w, rel_bias_table, ret_w_in, ret_w_out, dil_w_in, dil_w_out,
           dn_w_in, dn_conv_w, dn_a_log, dn_dt_bias, dn_norm_w, dn_w_out, moe_w_router, moe_w_gate, moe_w_up,
           moe_w_down):
    batch, seq, d = x.shape
    x2 = x.reshape(batch * seq, d)
    for i in range(norm_mix_w.shape[0]):
        j = i // N_MIXERS
        kind = i % N_MIXERS
        if kind == 0:
            x2 = _retention_mixer(x2, norm_mix_w[i], (ret_w_in, j), (ret_w_out, j), batch, seq)
        elif kind == 1:
            x2 = _dilated_mixer(x2, norm_mix_w[i], (dil_w_in, j), (dil_w_out, j), rel_bias_table, batch, seq)
        else:
            x2 = _deltanet_mixer(x2, norm_mix_w[i], (dn_w_in, j), dn_conv_w[j], dn_a_log[j], dn_dt_bias[j],
                                 dn_norm_w[j], (dn_w_out, j), batch, seq)
        x2 = _moe(x2, norm_ffn_w[i], moe_w_router[i], moe_w_gate, moe_w_up, moe_w_down, i, batch, seq)
    return _final_norm(x2, final_norm_w).reshape(batch, seq, d)
```

```python
import functools
import math

import jax
import jax.numpy as jnp
import numpy as np
from jax import lax
from jax.experimental import pallas as pl
from jax.experimental.pallas import tpu as pltpu

F32 = jnp.float32
BF16 = jnp.bfloat16

D_MODEL = 2048
EPS = 1e-6
NEG_INF = -1e30
RET_HEADS = 8
RET_DK = D_MODEL // RET_HEADS
RET_DV = 2 * RET_DK
ROPE_BASE = 10000.0
RET_BWD_DECAY_OFFSET = 0.5
DIL_PATTERNS = ((128, 1), (512, 4), (2048, 16))
DIL_HEADS = 16
DIL_DH = D_MODEL // DIL_HEADS
REL_BUCKETS = 32
REL_MAX_DIST = 1024
DN_K_HEADS = 16
DN_V_HEADS = 32
DN_DK = 128
DN_DV = 128
DN_CONV = 5
N_EXPERTS = 16
EXPERT_FF = D_MODEL // 2
CAPACITY_FACTOR = 2
N_MIXERS = 3

LANES = 128
VMEM_LIMIT_BYTES = 56 * 1024 * 1024
ROW_TILE = 1024
PROJ_COL_TILE = 1024
PROJ_PIECE = 256
RET_CHUNK = 256
RET_HEAD_GROUP = 2
MOE_FF_TILE = 256
MOE_DMA_UNROLL = 8


def _cparams(*sem):
    return pltpu.CompilerParams(dimension_semantics=sem, vmem_limit_bytes=VMEM_LIMIT_BYTES)


def _dot(a, b):
    return jnp.dot(a, b, preferred_element_type=F32)


def _dot_nt(a, b):
    return lax.dot_general(a, b, (((1,), (1,)), ((), ())), preferred_element_type=F32)


def _dot_tn(a, b):
    return lax.dot_general(a, b, (((0,), (0,)), ((), ())), preferred_element_type=F32)


def _silu(x):
    return x / (1.0 + jnp.exp(-x))


def _rms_rows(x):
    return x * lax.rsqrt(jnp.mean(x * x, axis=-1, keepdims=True) + EPS)


def _norm_proj_kernel(x_ref, nw_ref, w_ref, o_ref, hn_ref):
    @pl.when(pl.program_id(1) == 0)
    def _():
        hn_ref[...] = (_rms_rows(x_ref[...]) * nw_ref[...]).astype(BF16)

    o_ref[...] = _dot(hn_ref[...], w_ref[...].astype(BF16)).astype(o_ref.dtype)


def _norm_proj_residue_kernel(x_ref, nw_ref, w_ref, o_ref, hn_ref, lanes_ref, *, dil):
    @pl.when(pl.program_id(1) == 0)
    def _():
        h = _rms_rows(x_ref[...]) * nw_ref[...]
        per = h.shape[0] // dil
        for c in range(h.shape[1] // LANES):
            cols = slice(c * LANES, (c + 1) * LANES)
            lanes_ref[...] = h[:, cols]
            for r in range(dil):
                hn_ref[r * per:(r + 1) * per, cols] = lanes_ref[pl.ds(r, per, stride=dil), :].astype(BF16)

    o_ref[...] = _dot(hn_ref[...], w_ref[...].astype(BF16)).astype(o_ref.dtype)


def _norm_proj_rope_kernel(x_ref, nw_ref, w_ref, cos_ref, sin_ref, o_ref, hn_ref, *, n_q_tiles, n_rope_tiles, k_scale):
    j = pl.program_id(1)

    @pl.when(j == 0)
    def _():
        hn_ref[...] = (_rms_rows(x_ref[...]) * nw_ref[...]).astype(BF16)

    c = cos_ref[...]
    s = sin_ref[...]
    half = c.shape[1]
    hn = hn_ref[...]
    rotate = j < n_rope_tiles
    scale = jnp.where(j >= n_q_tiles, k_scale, 1.0).astype(F32)
    accs = [_dot(hn, w_ref[:, 2 * hh * half:(2 * hh + 2) * half].astype(BF16))
            for hh in range(o_ref.shape[1] // (2 * half))]
    for hh, acc in enumerate(accs):
        x1 = acc[:, :half]
        x2 = acc[:, half:]
        o_ref[:, 2 * hh * half:(2 * hh + 1) * half] = jnp.where(rotate, (x1 * c - x2 * s) * scale, x1).astype(o_ref.dtype)
        o_ref[:, (2 * hh + 1) * half:(2 * hh + 2) * half] = jnp.where(rotate, (x2 * c + x1 * s) * scale,
                                                                       x2).astype(o_ref.dtype)


def _norm_proj(x2, nw, w, *, n_out, tn, out_dtype, col_off=0, rope=None, residue_major=1):
    m, k = x2.shape
    w_stack, layer = w
    tm = min(ROW_TILE, m)
    grid = (m // tm, n_out // tn)
    off = col_off // tn
    in_specs = [pl.BlockSpec((tm, k), lambda i, j: (i, 0)),
                pl.BlockSpec((1, k), lambda i, j: (0, 0)),
                pl.BlockSpec((None, k, tn), lambda i, j: (layer, 0, j + off))]
    args = [x2, nw.reshape(1, k), w_stack]
    scratch = [pltpu.VMEM((tm, k), BF16)]
    if residue_major > 1:
        body = functools.partial(_norm_proj_residue_kernel, dil=residue_major)
        scratch.append(pltpu.VMEM((tm, LANES), F32))
    elif rope is None:
        body = _norm_proj_kernel
    else:
        cos, sin, seq, n_q_cols, n_rope_cols, k_scale = rope
        nsb = seq // tm
        in_specs += [pl.BlockSpec((tm, cos.shape[1]), lambda i, j: (i % nsb, 0)),
                     pl.BlockSpec((tm, cos.shape[1]), lambda i, j: (i % nsb, 0))]
        args += [cos, sin]
        body = functools.partial(_norm_proj_rope_kernel, n_q_tiles=n_q_cols // tn, n_rope_tiles=n_rope_cols // tn,
                                 k_scale=k_scale)
    return pl.pallas_call(
        body, name="norm_proj", grid=grid, in_specs=in_specs,
        out_specs=pl.BlockSpec((tm, tn), lambda i, j: (i, j)),
        out_shape=jax.ShapeDtypeStruct((m, n_out), out_dtype),
        scratch_shapes=scratch,
        compiler_params=_cparams("parallel", "arbitrary"),
    )(*args)


def _out_proj_kernel(a_ref, w_ref, res_ref, o_ref):
    o_ref[...] = res_ref[...] + _dot(a_ref[...], w_ref[...].astype(BF16))


def _out_proj(a, w, res, *, tn=512):
    m, k = a.shape
    w_stack, layer = w
    n = w_stack.shape[2]
    tm = min(ROW_TILE, m)
    return pl.pallas_call(
        _out_proj_kernel, name="out_proj", grid=(m // tm, n // tn),
        in_specs=[pl.BlockSpec((tm, k), lambda i, j: (i, 0)),
                  pl.BlockSpec((None, k, tn), lambda i, j: (layer, 0, j)),
                  pl.BlockSpec((tm, tn), lambda i, j: (i, j))],
        out_specs=pl.BlockSpec((tm, tn), lambda i, j: (i, j)),
        out_shape=jax.ShapeDtypeStruct((m, n), F32),
        compiler_params=_cparams("parallel", "arbitrary"),
    )(a, w_stack, res)


def _final_norm_kernel(x_ref, nw_ref, o_ref):
    o_ref[...] = _rms_rows(x_ref[...]) * nw_ref[...]


def _final_norm(x2, nw):
    m, k = x2.shape
    tm = min(ROW_TILE, m)
    return pl.pallas_call(
        _final_norm_kernel, name="final_norm", grid=(m // tm,),
        in_specs=[pl.BlockSpec((tm, k), lambda i: (i, 0)), pl.BlockSpec((1, k), lambda i: (0, 0))],
        out_specs=pl.BlockSpec((tm, k), lambda i: (i, 0)),
        out_shape=jax.ShapeDtypeStruct((m, k), F32),
        compiler_params=_cparams("parallel"),
    )(x2, nw.reshape(1, k))


def _retention_kernel(q_ref, k_ref, v_ref, g_ref, dmat_ref, vec_ref, o_ref, state_ref, oacc_ref):
    sweep = pl.program_id(2)
    c = pl.program_id(3)
    nc = pl.num_programs(3)
    hg = range(RET_HEAD_GROUP)
    qs = [slice(h * RET_DK, (h + 1) * RET_DK) for h in hg]
    vs = [slice(h * RET_DV, (h + 1) * RET_DV) for h in hg]

    @pl.when(c == 0)
    def _():
        state_ref[...] = jnp.zeros_like(state_ref)

    @pl.when(sweep == 0)
    def _():
        scores = [_dot_nt(q_ref[:, qs[h]], k_ref[:, qs[h]]) for h in hg]
        inters = [_dot(q_ref[:, qs[h]], state_ref[h].astype(BF16)) for h in hg]
        kds = [(k_ref[:, qs[h]].astype(F32) * vec_ref[h][:, 2:3]).astype(BF16) for h in hg]
        upds = [_dot_tn(kds[h], v_ref[:, vs[h]]) for h in hg]
        probs = [(scores[h] * dmat_ref[h]).astype(BF16) for h in hg]
        intras = [_dot(probs[h], v_ref[:, vs[h]]) for h in hg]
        for h in hg:
            vec = vec_ref[h]
            oacc_ref[c, :, vs[h]] = intras[h] + inters[h] * vec[:, 0:1]
            state_ref[h] = state_ref[h] * vec[0:1, 4:5] + upds[h]

    @pl.when(sweep == 1)
    def _():
        inters = [_dot(q_ref[:, qs[h]], state_ref[h].astype(BF16)) for h in hg]
        kds = [(k_ref[:, qs[h]].astype(F32) * vec_ref[h][:, 3:4]).astype(BF16) for h in hg]
        upds = [_dot_tn(kds[h], v_ref[:, vs[h]]) for h in hg]
        for h in hg:
            vec = vec_ref[h]
            o = _rms_rows(oacc_ref[nc - 1 - c, :, vs[h]] + inters[h] * vec[:, 1:2])
            o_ref[:, vs[h]] = (o * _silu(g_ref[:, vs[h]].astype(F32))).astype(o_ref.dtype)
            state_ref[h] = state_ref[h] * vec[0:1, 5:6] + upds[h]


def _retention_tables(chunk):
    hh = jnp.arange(RET_HEADS, dtype=F32)
    lg_f = jnp.log1p(-jnp.exp2(-5.0 - hh))
    lg_b = jnp.log1p(-jnp.exp2(-(5.0 + RET_BWD_DECAY_OFFSET) - hh))
    t = jnp.arange(chunk, dtype=F32)
    diff = t[:, None] - t[None, :]
    dmat = jnp.where(diff[None] >= 0,
                     jnp.exp(jnp.maximum(diff, 0.0)[None] * lg_f[:, None, None]),
                     jnp.exp(jnp.maximum(-diff, 0.0)[None] * lg_b[:, None, None]))
    cols = [jnp.exp((t[None, :] + 1.0) * lg_f[:, None]),
            jnp.exp((chunk - t)[None, :] * lg_b[:, None]),
            jnp.exp((chunk - 1.0 - t)[None, :] * lg_f[:, None]),
            jnp.exp(t[None, :] * lg_b[:, None]),
            jnp.broadcast_to(jnp.exp(chunk * lg_f)[:, None], (RET_HEADS, chunk)),
            jnp.broadcast_to(jnp.exp(chunk * lg_b)[:, None], (RET_HEADS, chunk))]
    cols += [jnp.zeros((RET_HEADS, chunk), F32)] * 2
    return dmat, jnp.stack(cols, axis=-1)


def _retention_core(proj, batch, seq):
    cq = min(RET_CHUNK, seq)
    nc = seq // cq
    dmat, vec = _retention_tables(cq)
    g = RET_HEAD_GROUP
    n_hg = RET_HEADS // g
    kq = n_hg
    kv = 2 * RET_HEADS * RET_DK // (g * RET_DV)
    kg = kv + n_hg

    def row(b, s, c):
        return b * nc + c + s * (nc - 1 - 2 * c)

    def row_out(b, s, c):
        return b * nc + nc - 1 - c * s

    return pl.pallas_call(
        _retention_kernel, name="retention_core", grid=(batch, n_hg, 2, nc),
        in_specs=[pl.BlockSpec((cq, g * RET_DK), lambda b, h, s, c: (row(b, s, c), h)),
                  pl.BlockSpec((cq, g * RET_DK), lambda b, h, s, c: (row(b, s, c), kq + h)),
                  pl.BlockSpec((cq, g * RET_DV), lambda b, h, s, c: (row(b, s, c), kv + h)),
                  pl.BlockSpec((cq, g * RET_DV), lambda b, h, s, c: (row_out(b, s, c), kg + h)),
                  pl.BlockSpec((g, cq, cq), lambda b, h, s, c: (h, 0, 0)),
                  pl.BlockSpec((g, cq, 8), lambda b, h, s, c: (h, 0, 0))],
        out_specs=pl.BlockSpec((cq, g * RET_DV), lambda b, h, s, c: (row_out(b, s, c), h)),
        out_shape=jax.ShapeDtypeStruct((batch * seq, RET_HEADS * RET_DV), BF16),
        scratch_shapes=[pltpu.VMEM((g, RET_DK, RET_DV), F32), pltpu.VMEM((nc, cq, g * RET_DV), F32)],
        compiler_params=_cparams("parallel", "parallel", "arbitrary", "arbitrary"),
    )(proj, proj, proj, proj, dmat, vec)


def _rope_tables(seq, half):
    inv = ROPE_BASE ** (-np.arange(half, dtype=np.float64) / half)
    ang = np.arange(seq, dtype=np.float64)[:, None] * inv[None, :]
    return jnp.asarray(np.cos(ang), F32), jnp.asarray(np.sin(ang), F32)


def _retention_mixer(x2, nw, w_in, w_out, batch, seq):
    cos, sin = _rope_tables(seq, RET_DK // 2)
    qk_cols = 2 * RET_HEADS * RET_DK
    proj = _norm_proj(x2, nw, w_in, n_out=w_in[0].shape[2], tn=PROJ_COL_TILE, out_dtype=BF16,
                      rope=(cos, sin, seq, qk_cols // 2, qk_cols, RET_DK ** -0.5))
    o = _retention_core(proj, batch, seq)
    return _out_proj(o, w_out, x2)


def _split_bf16(x):
    hi = x.astype(BF16)
    return hi, (x - hi.astype(F32)).astype(BF16)


def _router_kernel(x_ref, nw_ref, wr_ref, hx_ref, afft_ref):
    half = x_ref.shape[1] // 2
    h = _rms_rows(x_ref[...]) * nw_ref[...]
    h_hi, h_lo = _split_bf16(h)
    bits = lax.bitcast_convert_type(h_hi.astype(F32), jnp.uint32)
    hx_ref[:, :half] = (bits[:, half:] & jnp.uint32(0xFFFF0000)) | (bits[:, :half] >> 16)
    w_hi, w_lo = _split_bf16(wr_ref[...])
    lg = _dot(h_hi, w_hi) + (_dot(h_lo, w_hi) + _dot(h_hi, w_lo))
    lane = lax.broadcasted_iota(jnp.int32, lg.shape, 1)
    lg = jnp.where(lane < N_EXPERTS, lg, NEG_INF)
    e = jnp.exp(lg - jnp.max(lg, axis=1, keepdims=True))
    aff = e / jnp.sum(e, axis=1, keepdims=True)
    hx_ref[:, half:] = lax.bitcast_convert_type(aff, jnp.uint32)
    afft_ref[0] = aff.T[:N_EXPERTS, :]


def _router(x2, nw, w_router, batch, seq):
    m, d = x2.shape
    tm = min(512, seq)
    nsb = seq // tm
    hw = d // 2 + LANES
    wr = jnp.pad(w_router, ((0, 0), (0, LANES - N_EXPERTS)))
    return pl.pallas_call(
        _router_kernel, name="moe_router", grid=(m // tm,),
        in_specs=[pl.BlockSpec((tm, d), lambda i: (i, 0)),
                  pl.BlockSpec((1, d), lambda i: (0, 0)),
                  pl.BlockSpec((d, LANES), lambda i: (0, 0))],
        out_specs=[pl.BlockSpec((tm, hw), lambda i: (i, 0)),
                   pl.BlockSpec((1, N_EXPERTS, tm), lambda i: (i // nsb, 0, i % nsb))],
        out_shape=[jax.ShapeDtypeStruct((m, hw), jnp.uint32),
                   jax.ShapeDtypeStruct((batch, N_EXPERTS, seq), F32)],
        compiler_params=_cparams("parallel"),
    )(x2, nw.reshape(1, d), wr)


def _topk_kernel(aff_ref, idx_ref, loc_ref, off_ref, end_ref, *, cap, n_groups, n_blk):
    n = n_groups * n_blk
    bits = lax.bitcast_convert_type(aff_ref[...], jnp.int32)
    ri = lax.broadcasted_iota(jnp.int32, (n, n), 0)
    ci = lax.broadcasted_iota(jnp.int32, (n, n), 1)
    same = (ri // n_blk) == (ci // n_blk)
    grp_ones = same.astype(BF16)
    grp_before = (same & (ci < ri)).astype(BF16)
    li = lax.broadcasted_iota(jnp.int32, (LANES, LANES), 0)
    lj = lax.broadcasted_iota(jnp.int32, (LANES, LANES), 1)
    incl = (li <= lj).astype(BF16)
    ones = jnp.ones((LANES, LANES), BF16)

    def row_total(mask):
        return _dot(mask.astype(BF16), ones)

    def group_count(mask):
        return _dot(grp_ones, row_total(mask).astype(BF16))

    member = (lax.broadcasted_iota(jnp.int32, (n_groups, n), 1) // n_blk
              == lax.broadcasted_iota(jnp.int32, (n_groups, n), 0)).astype(BF16)
    spread = (lax.broadcasted_iota(jnp.int32, (n, n_groups), 0) // n_blk
              == lax.broadcasted_iota(jnp.int32, (n, n_groups), 1)).astype(BF16)

    def search(i, tau):
        cand = tau | jnp.left_shift(jnp.int32(1), 30 - i)
        per_lane = _dot(member, (bits >= cand).astype(BF16))
        take = (jnp.sum(per_lane, axis=1, keepdims=True) >= cap).astype(BF16)
        take_rows = _dot(spread, jnp.broadcast_to(take, (n_groups, LANES)))
        return jnp.where(take_rows > 0.5, cand, tau)

    tau = lax.fori_loop(0, 31, search, jnp.zeros((n, LANES), jnp.int32))

    def cumsum(mask):
        mb = mask.astype(BF16)
        tot = _dot(mb, ones)
        return _dot(mb, incl), _dot(grp_before, tot.astype(BF16)), tot

    gt = bits > tau
    eq = bits == tau
    need = cap - group_count(gt)
    eq_loc, eq_off, _ = cumsum(eq)
    sel = gt | (eq & (eq_loc + eq_off <= need))
    loc, off, tot = cumsum(sel)
    loc_ref[...] = loc
    off_ref[...] = off
    end_ref[...] = off + tot

    slot = lax.broadcasted_iota(jnp.int32, (cap, LANES), 0).astype(F32)
    lane = lax.broadcasted_iota(jnp.int32, (cap, LANES), 1)
    eye = lax.broadcasted_iota(jnp.int32, (n_blk, LANES), 0) == lax.broadcasted_iota(jnp.int32, (n_blk, LANES), 1)
    pad = jnp.zeros((LANES - n_blk, LANES), BF16)

    def compact(g, acc):
        rows = pl.ds(pl.multiple_of(g * n_blk, n_blk), n_blk)
        end_row = jnp.sum(jnp.where(eye, end_ref[rows, :], 0.0), axis=0, keepdims=True)
        off_row = jnp.sum(jnp.where(eye, off_ref[rows, :], 0.0), axis=0, keepdims=True)
        blk = jnp.sum(((end_row <= slot) & (lane < n_blk)).astype(F32), axis=1, keepdims=True)
        onehot = lane == blk.astype(jnp.int32)
        loc_pad = jnp.concatenate([loc_ref[rows, :].astype(BF16), pad], axis=0)
        in_blk = _dot(onehot.astype(BF16), loc_pad)
        rank = slot - jnp.sum(jnp.where(onehot, off_row, 0.0), axis=1, keepdims=True)
        pos = blk * LANES + jnp.sum((in_blk <= rank).astype(F32), axis=1, keepdims=True)
        return jnp.where(lane == g, pos.astype(jnp.int32), acc)

    idx_ref[...] = lax.fori_loop(0, n_groups, compact, jnp.zeros((cap, LANES), jnp.int32))


def _topk(afft, cap):
    batch, n_e, seq = afft.shape
    n_groups = batch * n_e
    n_blk = seq // LANES
    n = n_groups * n_blk
    assert n_groups <= LANES and n_blk <= LANES
    idx = pl.pallas_call(
        functools.partial(_topk_kernel, cap=cap, n_groups=n_groups, n_blk=n_blk), name="moe_topk",
        out_shape=jax.ShapeDtypeStruct((cap, LANES), jnp.int32),
        scratch_shapes=[pltpu.VMEM((n, LANES), F32)] * 3,
        compiler_params=pltpu.CompilerParams(vmem_limit_bytes=VMEM_LIMIT_BYTES),
    )(afft.reshape(n, LANES))
    return idx[:, :n_groups].T.reshape(batch, n_e, cap)


def _moe_ffn_kernel(idx_ref, hx_hbm, wg_ref, wu_ref, wd_ref, xres_hbm, out_hbm,
                    hbuf, xbf, acc, rbuf, gate, sem_h, sem_r, sem_s):
    del xres_hbm
    e = pl.program_id(0)
    f = pl.program_id(1)
    n_e = pl.num_programs(0)
    nf = pl.num_programs(1)
    rows = xbf.shape[0]
    d = xbf.shape[1]
    per_step = rows // nf
    slot = e % 2
    base = e * rows
    nxt = ((e + 1) % n_e) * rows

    def h_copy(row_id, fq, u, s):
        return pltpu.make_async_copy(hx_hbm.at[pl.ds(row_id, 1), :], hbuf.at[s, fq, pl.ds(u, 1), :], sem_h.at[s])

    def wait_all(buf, sem):
        pltpu.make_async_copy(buf, buf, sem).wait()

    @pl.when((e == 0) & (f == 0))
    def _():
        def start(r, carry):
            h_copy(idx_ref[r], r // per_step, r % per_step, 0).start()
            return carry

        lax.fori_loop(0, rows, start, 0, unroll=MOE_DMA_UNROLL)

    @pl.when(f == 0)
    def _():
        wait_all(hbuf.at[slot], sem_h.at[slot])
        half = d // 2
        packed = hbuf[slot].reshape(rows, half + LANES)
        words = packed[:, :half]
        xbf[:, :half] = lax.bitcast_convert_type(words << 16, F32).astype(BF16)
        xbf[:, half:] = lax.bitcast_convert_type(words & jnp.uint32(0xFFFF0000), F32).astype(BF16)
        lane = lax.broadcasted_iota(jnp.int32, (rows, LANES), 1)
        aff = lax.bitcast_convert_type(packed[:, half:], F32)
        gate[...] = jnp.sum(jnp.where(lane == e, aff, 0.0), axis=1, keepdims=True)

    for u in range(per_step):
        r = f * per_step + u
        h_copy(idx_ref[nxt + r], f, u, 1 - slot).start()
        pltpu.make_async_copy(out_hbm.at[pl.ds(idx_ref[base + r], 1), :], rbuf.at[f, pl.ds(u, 1), :],
                              sem_r.at[0]).start()

    x = xbf[...]
    hid = (_silu(_dot(x, wg_ref[...].astype(BF16))) * _dot(x, wu_ref[...].astype(BF16))).astype(BF16)
    part = _dot(hid, wd_ref[...].astype(BF16))

    @pl.when(f == 0)
    def _():
        acc[...] = part

    @pl.when(f > 0)
    def _():
        acc[...] += part

    @pl.when(f == nf - 1)
    def _():
        wait_all(rbuf, sem_r.at[0])
        rbuf[...] = rbuf[...] + (acc[...] * gate[...]).reshape(rbuf.shape)

        for fq in range(rbuf.shape[0]):
            def start(u, carry, fq=fq):
                pltpu.make_async_copy(rbuf.at[fq, pl.ds(u, 1), :],
                                      out_hbm.at[pl.ds(idx_ref[base + fq * per_step + u], 1), :], sem_s.at[0]).start()
                return carry

            lax.fori_loop(0, per_step, start, 0, unroll=MOE_DMA_UNROLL)
        wait_all(rbuf, sem_s.at[0])

    @pl.when((e == n_e - 1) & (f == nf - 1))
    def _():
        wait_all(hbuf.at[1 - slot], sem_h.at[1 - slot])


def _moe_ffn(rows_idx, hx, x2, w_gate, w_up, w_down, layer):
    m, d = x2.shape
    _, n_e, _, ff = w_gate.shape
    rows = rows_idx.shape[0] // n_e
    tf = min(MOE_FF_TILE, ff)
    nf = ff // tf
    grid_spec = pltpu.PrefetchScalarGridSpec(
        num_scalar_prefetch=1, grid=(n_e, nf),
        in_specs=[pl.BlockSpec(memory_space=pl.ANY),
                  pl.BlockSpec((None, None, d, tf), lambda e, f, idx: (layer, e, 0, f)),
                  pl.BlockSpec((None, None, d, tf), lambda e, f, idx: (layer, e, 0, f)),
                  pl.BlockSpec((None, None, tf, d), lambda e, f, idx: (layer, e, f, 0)),
                  pl.BlockSpec(memory_space=pl.ANY)],
        out_specs=pl.BlockSpec(memory_space=pl.ANY),
        scratch_shapes=[pltpu.VMEM((2, nf, rows // nf, hx.shape[1]), jnp.uint32), pltpu.VMEM((rows, d), BF16),
                        pltpu.VMEM((rows, d), F32), pltpu.VMEM((nf, rows // nf, d), F32),
                        pltpu.VMEM((rows, 1), F32),
                        pltpu.SemaphoreType.DMA((2,)), pltpu.SemaphoreType.DMA((1,)),
                        pltpu.SemaphoreType.DMA((1,))])
    return pl.pallas_call(
        _moe_ffn_kernel, name="moe_ffn", grid_spec=grid_spec,
        out_shape=jax.ShapeDtypeStruct((m, d), F32),
        input_output_aliases={5: 0},
        compiler_params=_cparams("arbitrary", "arbitrary"),
    )(rows_idx, hx, w_gate, w_up, w_down, x2)


def _moe(x2, nw, w_router, w_gate, w_up, w_down, layer, batch, seq):
    cap = CAPACITY_FACTOR * seq // N_EXPERTS
    hx, afft = _router(x2, nw, w_router, batch, seq)
    idx = _topk(afft, cap)
    rows_idx = idx + (jnp.arange(batch, dtype=jnp.int32) * seq)[:, None, None]
    rows_idx = jnp.transpose(rows_idx, (1, 0, 2)).reshape(-1)
    return _moe_ffn(rows_idx, hx, x2, w_gate, w_up, w_down, layer)


DIL_BLOCK = 64


def _t5_bucket(rel):
    nb = REL_BUCKETS // 2
    max_exact = nb // 2
    ret = jnp.where(rel > 0, nb, 0)
    n = jnp.abs(rel)
    nf = jnp.maximum(n, 1).astype(F32)
    large = max_exact + (jnp.log(nf / max_exact) / math.log(REL_MAX_DIST / max_exact)
                         * (nb - max_exact)).astype(jnp.int32)
    large = jnp.minimum(large, nb - 1)
    return ret + jnp.where(n < max_exact, n, large)


def _dil_bias_kernel(table_ref, bucket_ref, o_ref):
    col = pl.program_id(0)
    bkt = bucket_ref[0]
    acc = jnp.zeros(bkt.shape, F32)
    for b in range(REL_BUCKETS):
        acc = jnp.where(bkt == b, table_ref[b, col], acc)
    s = lax.broadcasted_iota(jnp.int32, bkt.shape, 0)
    t = lax.broadcasted_iota(jnp.int32, bkt.shape, 1)
    o_ref[0] = jnp.where((t >= s) & (t <= s + 2 * DIL_BLOCK), acc, NEG_INF)


def _dil_bias(rel_table):
    qb = DIL_BLOCK
    rel_steps = jnp.arange(3 * qb)[None, :] - jnp.arange(qb)[:, None] - qb
    buckets = jnp.stack([_t5_bucket(rel_steps * dil) for _, dil in DIL_PATTERNS]).astype(jnp.int32)
    n_col = rel_table.shape[1]
    return pl.pallas_call(
        _dil_bias_kernel, name="dil_bias", grid=(n_col,),
        in_specs=[pl.BlockSpec(memory_space=pltpu.SMEM),
                  pl.BlockSpec((1, qb, 3 * qb), lambda c: (c // DIL_HEADS, 0, 0))],
        out_specs=pl.BlockSpec((1, qb, 3 * qb), lambda c: (c, 0, 0)),
        out_shape=jax.ShapeDtypeStruct((n_col, qb, 3 * qb), F32),
        compiler_params=_cparams("parallel"),
    )(rel_table, buckets)


def _dil_attn_kernel(q_ref, kp_ref, kc_ref, kn_ref, vp_ref, vc_ref, vn_ref, bias_ref, o_ref, lse_ref):
    jb = pl.program_id(2)
    nb = pl.num_programs(2)
    qb = q_ref.shape[0]
    t = lax.broadcasted_iota(jnp.int32, (qb, 3 * qb), 1)
    in_seq = ((jb > 0) | (t >= qb)) & ((jb < nb - 1) | (t < 2 * qb))
    kw = jnp.concatenate([kp_ref[...], kc_ref[...], kn_ref[...]], axis=0)
    vw = jnp.concatenate([vp_ref[...], vc_ref[...], vn_ref[...]], axis=0)
    lane = lax.broadcasted_iota(jnp.int32, (qb, LANES), 1)
    lse = jnp.zeros((qb, LANES), F32)
    heads = [slice(h * DIL_DH, (h + 1) * DIL_DH) for h in range(DIL_HEADS)]
    scores = [_dot_nt(q_ref[:, hs], kw[:, hs]) for hs in heads]
    probs, dens = [], []
    for h, s in enumerate(scores):
        s = jnp.where(in_seq, s * (DIL_DH ** -0.5) + bias_ref[h], NEG_INF)
        m = jnp.max(s, axis=1, keepdims=True)
        p = jnp.exp(s - m)
        den = jnp.sum(p, axis=1, keepdims=True)
        probs.append(p.astype(BF16))
        dens.append(den)
        lse = jnp.where(lane == h, m + jnp.log(den), lse)
    outs = [_dot(p, vw[:, hs]) for p, hs in zip(probs, heads)]
    for hs, o, den in zip(heads, outs, dens):
        o_ref[:, hs] = (o / den).astype(o_ref.dtype)
    lse_ref[0] = lse[:, :DIL_HEADS]


def _dil_group(proj, bias, gi, dil, batch, seq):
    qb = DIL_BLOCK
    nb = seq // dil // qb
    hw = DIL_HEADS * DIL_DH
    m = batch * seq
    tm = min(ROW_TILE, m)
    bpt = tm // (dil * qb)
    tiles_per_seq = seq // tm

    def row_block(b, r, j):
        return (b * tiles_per_seq + j // bpt) * (tm // qb) + r * bpt + j % bpt

    def spec(which, shift):
        def imap(b, r, j):
            return (row_block(b, r, jnp.clip(j + shift, 0, nb - 1)), which)
        return pl.BlockSpec((qb, hw), imap)

    o, lse = pl.pallas_call(
        _dil_attn_kernel, name=f"dil_attn_d{dil}", grid=(batch, dil, nb),
        in_specs=[spec(0, 0), spec(1, -1), spec(1, 0), spec(1, 1), spec(2, -1), spec(2, 0), spec(2, 1),
                  pl.BlockSpec((DIL_HEADS, qb, 3 * qb), lambda b, r, j: (gi, 0, 0))],
        out_specs=[pl.BlockSpec((qb, hw), lambda b, r, j: (row_block(b, r, j), 0)),
                   pl.BlockSpec((1, qb, DIL_HEADS), lambda b, r, j: (row_block(b, r, j), 0, 0))],
        out_shape=[jax.ShapeDtypeStruct((m, hw), BF16),
                   jax.ShapeDtypeStruct((m // qb, qb, DIL_HEADS), F32)],
        compiler_params=_cparams("parallel", "parallel", "arbitrary"),
    )(proj, proj, proj, proj, proj, proj, proj, bias)
    lse = lse.reshape(m // tm, dil, tm // dil, DIL_HEADS)
    return o, jnp.transpose(lse, (0, 2, 1, 3)).reshape(m, DIL_HEADS)


def _dil_out_kernel(o0_ref, o1_ref, o2_ref, l0_ref, l1_ref, l2_ref, w_ref, res_ref, out_ref, comb_ref, tok_ref, *,
                    dils):
    @pl.when(pl.program_id(1) == 0)
    def _():
        l0, l1, l2 = l0_ref[...], l1_ref[...], l2_ref[...]
        m = jnp.maximum(jnp.maximum(l0, l1), l2)
        e0, e1, e2 = jnp.exp(l0 - m), jnp.exp(l1 - m), jnp.exp(l2 - m)
        den = e0 + e1 + e2
        wts = (e0 / den, e1 / den, e2 / den)
        rows = comb_ref.shape[0]
        for h in range(DIL_HEADS):
            hs = slice(h * DIL_DH, (h + 1) * DIL_DH)
            total = None
            for g, (o_ref, dil) in enumerate(zip((o0_ref, o1_ref, o2_ref), dils)):
                if dil == 1:
                    o = o_ref[:, hs].astype(F32)
                else:
                    per = rows // dil
                    for r in range(dil):
                        tok_ref[g, pl.ds(r, per, stride=dil), :] = o_ref[r * per:(r + 1) * per, hs].astype(F32)
                    o = tok_ref[g]
                term = wts[g][:, h:h + 1] * o
                total = term if total is None else total + term
            comb_ref[:, hs] = total.astype(BF16)

    out_ref[...] = res_ref[...] + _dot(comb_ref[...], w_ref[...].astype(BF16))


def _dil_out(os_, lses, w, res, *, tn=512):
    m, k = os_[0].shape
    w_stack, layer = w
    n = w_stack.shape[2]
    tm = min(ROW_TILE, m)
    ospec = pl.BlockSpec((tm, k), lambda i, j: (i, 0))
    lspec = pl.BlockSpec((tm, DIL_HEADS), lambda i, j: (i, 0))
    dils = tuple(dil for _, dil in DIL_PATTERNS)
    return pl.pallas_call(
        functools.partial(_dil_out_kernel, dils=dils), name="dil_out", grid=(m // tm, n // tn),
        in_specs=[ospec, ospec, ospec, lspec, lspec, lspec,
                  pl.BlockSpec((None, k, tn), lambda i, j: (layer, 0, j)),
                  pl.BlockSpec((tm, tn), lambda i, j: (i, j))],
        out_specs=pl.BlockSpec((tm, tn), lambda i, j: (i, j)),
        out_shape=jax.ShapeDtypeStruct((m, n), F32),
        scratch_shapes=[pltpu.VMEM((tm, k), BF16), pltpu.VMEM((len(dils), tm, DIL_DH), F32)],
        compiler_params=_cparams("parallel", "arbitrary"),
    )(*os_, *lses, w_stack, res)


def _dilated_mixer(x2, nw, w_in, w_out, rel_table, batch, seq):
    gw = w_in[0].shape[2] // len(DIL_PATTERNS)
    bias = _dil_bias(rel_table)
    outs = []
    for gi, (_, dil) in enumerate(DIL_PATTERNS):
        proj = _norm_proj(x2, nw, w_in, n_out=gw, tn=PROJ_COL_TILE, out_dtype=BF16, col_off=gi * gw,
                          residue_major=dil)
        outs.append(_dil_group(proj, bias, gi, dil, batch, seq))
    return _dil_out([o for o, _ in outs], [l for _, l in outs], w_out, x2)


DN_CHUNK = 64
DN_REP = DN_V_HEADS // DN_K_HEADS
DN_INST = 2 * DN_REP
DN_ROWS = DN_INST * DN_CHUNK
DN_PREP_GROUP = 8
CONV_TILE = 256
CONV_HALO = 8


def _split3_bf16(x):
    hi = x.astype(BF16)
    r = x - hi.astype(F32)
    mid = r.astype(BF16)
    return hi, mid, (r - mid.astype(F32)).astype(BF16)


def _softplus(x):
    return jnp.maximum(x, 0.0) + jnp.log1p(jnp.exp(-jnp.abs(x)))


def _deltanet_kernel(q_ref, k_ref, v_ref, z_ref, tail_ref, cwq_ref, cwk_ref, cwv_ref, alog_ref, dtb_ref, nw_ref,
                     o_ref, qn_ref, kn_ref, vc_ref, mq_ref, bo_ref, state_ref, oacc_ref):
    seq = q_ref.shape[0]
    nc = seq // DN_CHUNK
    c64 = DN_CHUNK
    dk = DN_DK

    pads = [slice(p * dk, (p + 1) * dk) for p in range(oacc_ref.shape[1] // dk)]
    oacc_ref[0:CONV_HALO, :] = jnp.zeros((CONV_HALO, oacc_ref.shape[1]), F32)
    oacc_ref[CONV_HALO + seq:, :] = jnp.zeros((CONV_HALO, oacc_ref.shape[1]), F32)

    def conv_into(jobs):
        ws = []
        for p, (src_ref, col, w_ref, _, _) in enumerate(jobs):
            oacc_ref[CONV_HALO:CONV_HALO + seq, pads[p]] = src_ref[:, col:col + dk].astype(F32)
            ws.append(w_ref[:, col:col + dk])

        def tile(r, carry):
            start = pl.multiple_of(r * CONV_TILE, CONV_TILE)
            ys = []
            for p in range(len(jobs)):
                win = oacc_ref[pl.ds(start, CONV_TILE + 2 * CONV_HALO), pads[p]]
                y = jnp.zeros((CONV_TILE, dk), F32)
                for j in range(DN_CONV):
                    lo = CONV_HALO + j - DN_CONV // 2
                    y = y + win[lo:lo + CONV_TILE, :] * ws[p][j:j + 1, :]
                ys.append(_silu(y))
            for y, (_, col, _, dst_ref, l2_scale) in zip(ys, jobs):
                if l2_scale is not None:
                    y = y * (lax.rsqrt(jnp.sum(y * y, axis=1, keepdims=True) + EPS) * l2_scale)
                dst_ref[pl.ds(start, CONV_TILE), col:col + dk] = y.astype(dst_ref.dtype)
            return carry

        lax.fori_loop(0, seq // CONV_TILE, tile, 0)

    conv_into([(q_ref, 0, cwq_ref, qn_ref, DN_DK ** -0.5), (k_ref, 0, cwk_ref, kn_ref, 1.0)])
    conv_into([(v_ref, vl * DN_DV, cwv_ref, vc_ref, None) for vl in range(DN_REP)])

    rr = lax.broadcasted_iota(jnp.int32, (DN_ROWS, DN_ROWS), 0)
    cc = lax.broadcasted_iota(jnp.int32, (DN_ROWS, DN_ROWS), 1)
    same = (rr // c64) == (cc // c64)
    fwd_rows = rr < DN_REP * c64
    strict = same & ((fwd_rows & (rr > cc)) | (jnp.logical_not(fwd_rows) & (rr < cc)))
    eye = (rr == cc)
    eye_f = eye.astype(F32)
    r64 = lax.broadcasted_iota(jnp.int32, (c64, DN_ROWS), 0)
    c64i = lax.broadcasted_iota(jnp.int32, (c64, DN_ROWS), 1)
    eye_row = (r64 == (c64i % c64)).astype(F32)
    li = lax.broadcasted_iota(jnp.int32, (c64, c64), 0)
    lj = lax.broadcasted_iota(jnp.int32, (c64, c64), 1)
    tril = (lj <= li).astype(BF16)
    triu = (lj >= li).astype(BF16)
    lane_all = lax.broadcasted_iota(jnp.int32, (c64, LANES), 1)
    hk = pl.program_id(1)
    beta_cols = [(i // DN_REP) * DN_V_HEADS + hk * DN_REP + i % DN_REP for i in range(DN_INST)]
    gate_cols = [2 * DN_V_HEADS + col for col in beta_cols]
    row_chain = lax.broadcasted_iota(jnp.int32, (1, DN_ROWS), 1) // c64
    tall_same = (lax.broadcasted_iota(jnp.int32, (DN_INST * dk, DN_ROWS), 0) // dk
                 == lax.broadcasted_iota(jnp.int32, (DN_INST * dk, DN_ROWS), 1) // c64)
    e_r = lax.broadcasted_iota(jnp.int32, (DN_INST * dk, dk), 0)
    e_c = lax.broadcasted_iota(jnp.int32, (DN_INST * dk, dk), 1)
    eye_tall = ((e_r % dk) == e_c).astype(F32)
    neg_a = -jnp.exp(alog_ref[...])
    dtb = dtb_ref[...]

    def stack(cols):
        return jnp.concatenate(cols, axis=0)

    def blockdiag(rows_):
        return jnp.where(same, jnp.concatenate([rows_] * DN_INST, axis=0), 0.0)

    def chunk_maps(c):
        rows = pl.ds(pl.multiple_of(c * c64, c64), c64)
        tl = tail_ref[rows, :]
        beta_all = 1.0 / (1.0 + jnp.exp(-tl))
        g_all = neg_a * _softplus(tl + dtb)
        parts = _split3_bf16(g_all)
        gcf = _dot(tril, parts[0]) + (_dot(tril, parts[1]) + _dot(tril, parts[2]))
        gcb = _dot(triu, parts[0]) + (_dot(triu, parts[1]) + _dot(triu, parts[2]))
        yield

        def pick(a, col):
            return jnp.sum(jnp.where(lane_all == col, a, 0.0), axis=1, keepdims=True)

        beta_b = jnp.broadcast_to(stack([pick(beta_all, beta_cols[i]) for i in range(DN_INST)]), (DN_ROWS, dk))
        gc_b = jnp.broadcast_to(stack([pick(gcf if i < DN_REP else gcb, gate_cols[i]) for i in range(DN_INST)]),
                                (DN_ROWS, dk))
        g_tot = jnp.sum(g_all, axis=0, keepdims=True)
        gtot = [jnp.sum(jnp.where(lane_all[0:1] == gate_cols[i], g_tot, 0.0), axis=1, keepdims=True)
                for i in range(DN_INST)]
        gtot_b = stack([jnp.broadcast_to(t, (c64, dk)) for t in gtot])
        egc_b = jnp.exp(gc_b)

        kb = kn_ref[rows, :]
        qb = qn_ref[rows, :]
        vb = vc_ref[rows, :]
        k4 = stack([kb.astype(F32)] * DN_INST)
        q4 = stack([qb.astype(F32)] * DN_INST)
        v4 = stack([vb[:, (i % DN_REP) * DN_DV:(i % DN_REP + 1) * DN_DV] for i in range(DN_INST)]).astype(F32)
        k4t = k4.T

        gmat = jnp.concatenate([gc_b] * (DN_ROWS // dk), axis=1)
        grow = jnp.sum(jnp.where(eye, gmat, 0.0), axis=0, keepdims=True)
        decay = jnp.exp(jnp.where(strict, gmat - grow, NEG_INF))
        k4t_b = k4t.astype(BF16)
        kk_row = _dot(kb, k4t_b)
        qk_row = _dot(qb, k4t_b)
        yield
        n_bd = jnp.concatenate([beta_b] * (DN_ROWS // dk), axis=1) * stack([kk_row] * DN_INST) * decay
        qk_bd = stack([qk_row] * DN_INST) * (decay + eye_f)

        r_pow = -(n_bd[0:c64] + n_bd[c64:2 * c64] + n_bd[2 * c64:3 * c64] + n_bd[3 * c64:4 * c64])
        t_row = eye_row + r_pow
        r_pow = _dot(r_pow.astype(BF16), (-n_bd).astype(BF16))
        yield
        for _ in range(4):
            p_bd = blockdiag(r_pow).astype(BF16)
            rt = _dot(jnp.concatenate([r_pow, t_row], axis=0).astype(BF16), p_bd)
            yield
            r_pow = rt[0:c64]
            t_row = t_row + rt[c64:]
        t_row = t_row + _dot(t_row.astype(BF16), blockdiag(r_pow).astype(BF16))
        yield

        rhs = jnp.concatenate([k4 * (beta_b * egc_b), v4 * beta_b], axis=1).astype(BF16)
        wu = _dot(blockdiag(t_row).astype(BF16), rhs).astype(BF16)
        yield
        qk_wu = _dot(qk_bd.astype(BF16), wu)
        gtot_row = jnp.zeros((1, DN_ROWS), F32)
        for i in range(DN_INST):
            gtot_row = jnp.where(row_chain == i, gtot[i], gtot_row)
        kdt = k4t * jnp.exp(gtot_row - grow)
        kdt_wide = jnp.where(tall_same, stack([kdt] * DN_INST), 0.0).astype(BF16)
        kw = _dot(kdt_wide, wu)
        yield
        qp = (q4 * egc_b - qk_wu[:, :dk]).astype(BF16)
        op = qk_wu[:, dk:].astype(BF16)
        dl = stack([jnp.broadcast_to(jnp.exp(t), (dk, dk)) for t in gtot])
        return qp, op, (dl * eye_tall - kw[:, :dk]).astype(BF16), kw[:, dk:].astype(BF16)

    def run_interleaved(gens):
        results = [None] * len(gens)
        while any(r is None for r in results):
            for u, gen in enumerate(gens):
                if results[u] is None:
                    try:
                        next(gen)
                    except StopIteration as done:
                        results[u] = done.value
        return results

    half_group = DN_PREP_GROUP // 2
    n_groups = nc // DN_PREP_GROUP

    def prep_group(g):
        low = [g * half_group + u for u in range(half_group)]
        cs = low + [nc - 1 - c for c in low]
        maps = run_interleaved([chunk_maps(c) for c in cs])
        for c, (qp, op, mt, bb) in zip(cs, maps):
            mq_ref[c] = stack([x for i in range(DN_INST)
                               for x in (mt[i * dk:(i + 1) * dk], qp[i * c64:(i + 1) * c64])])
            bo_ref[c] = stack([x for i in range(DN_INST)
                               for x in (bb[i * dk:(i + 1) * dk], op[i * c64:(i + 1) * c64])])

    def sweep_group(g):
        for u in range(half_group):
            j = g * half_group + u
            for i in range(DN_INST):
                c = j if i < DN_REP else nc - 1 - j
                span = slice(i * (dk + c64), (i + 1) * (dk + c64))
                res = _dot(mq_ref[c, span, :], state_ref[i].astype(BF16)) + bo_ref[c, span, :].astype(F32)
                state_ref[i] = res[:dk]
                rows = pl.ds(pl.multiple_of(CONV_HALO + c * c64, CONV_HALO), c64)
                vcols = slice((i % DN_REP) * DN_DV, (i % DN_REP + 1) * DN_DV)
                oacc_ref[rows, vcols] = oacc_ref[rows, vcols] + res[dk:]

    state_ref[...] = jnp.zeros_like(state_ref)
    oacc_ref[...] = jnp.zeros_like(oacc_ref)
    prep_group(0)

    def sweep_and_prep(g, carry):
        sweep_group(g - 1)
        prep_group(g)
        return carry

    lax.fori_loop(1, n_groups, sweep_and_prep, 0)

    def sweep_only(g, carry):
        sweep_group(g)
        return carry

    lax.fori_loop(n_groups - 1, 2 * n_groups, sweep_only, 0)

    def finish(r, carry):
        rows = pl.ds(pl.multiple_of(r * CONV_TILE, CONV_TILE), CONV_TILE)
        acc_rows = pl.ds(pl.multiple_of(CONV_HALO + r * CONV_TILE, CONV_HALO), CONV_TILE)
        for vl in range(DN_REP):
            vcols = slice(vl * DN_DV, (vl + 1) * DN_DV)
            o = _rms_rows(oacc_ref[acc_rows, vcols]) * nw_ref[...]
            o_ref[rows, vcols] = (o * _silu(z_ref[rows, vcols].astype(F32))).astype(o_ref.dtype)
        return carry

    lax.fori_loop(0, seq // CONV_TILE, finish, 0)


def _deltanet_core(proj, tail, conv_w, a_log, dt_bias, norm_w, batch, seq):
    gates = 2 * DN_V_HEADS
    alog = jnp.concatenate([jnp.zeros((gates,), F32), a_log.reshape(gates)]).reshape(1, LANES)
    dtb = jnp.concatenate([jnp.zeros((gates,), F32), dt_bias.reshape(gates)]).reshape(1, LANES)
    nc = seq // DN_CHUNK
    assert seq % (DN_CHUNK * DN_PREP_GROUP) == 0 and seq % CONV_TILE == 0
    kq = DN_K_HEADS
    vw = DN_REP * DN_DV
    kv = 2 * DN_K_HEADS * DN_DK // vw
    kz = kv + DN_K_HEADS
    once = pl.Buffered(1)
    return pl.pallas_call(
        _deltanet_kernel, name="deltanet_core", grid=(batch, DN_K_HEADS),
        in_specs=[pl.BlockSpec((seq, DN_DK), lambda b, h: (b, h), pipeline_mode=once),
                  pl.BlockSpec((seq, DN_DK), lambda b, h: (b, kq + h), pipeline_mode=once),
                  pl.BlockSpec((seq, vw), lambda b, h: (b, kv + h), pipeline_mode=once),
                  pl.BlockSpec((seq, vw), lambda b, h: (b, kz + h), pipeline_mode=once),
                  pl.BlockSpec((seq, LANES), lambda b, h: (b, 0), pipeline_mode=once),
                  pl.BlockSpec((DN_CONV, DN_DK), lambda b, h: (0, h)),
                  pl.BlockSpec((DN_CONV, DN_DK), lambda b, h: (0, kq + h)),
                  pl.BlockSpec((DN_CONV, vw), lambda b, h: (0, kv + h)),
                  pl.BlockSpec((1, LANES), lambda b, h: (0, 0)),
                  pl.BlockSpec((1, LANES), lambda b, h: (0, 0)),
                  pl.BlockSpec((1, DN_DV), lambda b, h: (0, 0))],
        out_specs=pl.BlockSpec((seq, vw), lambda b, h: (b, h)),
        out_shape=jax.ShapeDtypeStruct((batch * seq, DN_V_HEADS * DN_DV), BF16),
        scratch_shapes=[pltpu.VMEM((seq, DN_DK), BF16), pltpu.VMEM((seq, DN_DK), BF16), pltpu.VMEM((seq, vw), BF16),
                        pltpu.VMEM((nc, DN_INST * (DN_DK + DN_CHUNK), DN_DK), BF16),
                        pltpu.VMEM((nc, DN_INST * (DN_DK + DN_CHUNK), DN_DV), BF16),
                        pltpu.VMEM((DN_INST, DN_DK, DN_DV), F32),
                        pltpu.VMEM((seq + 2 * CONV_HALO, vw), F32)],
        compiler_params=_cparams("parallel", "parallel"),
    )(proj, proj, proj, proj, tail, conv_w, conv_w, conv_w, alog, dtb, norm_w.reshape(1, DN_DV))


def _deltanet_mixer(x2, nw, w_in, conv_w, a_log, dt_bias, norm_w, w_out, batch, seq):
    conv_dim = 2 * DN_K_HEADS * DN_DK + DN_V_HEADS * DN_DV
    main = conv_dim + DN_V_HEADS * DN_DV
    proj = _norm_proj(x2, nw, w_in, n_out=main, tn=PROJ_COL_TILE, out_dtype=BF16)
    tail = _norm_proj(x2, nw, w_in, n_out=LANES, tn=LANES, out_dtype=F32, col_off=main)
    o = _deltanet_core(proj, tail, conv_w, a_log, dt_bias, norm_w, batch, seq)
    return _out_proj(o, w_out, x2)


def kernel(x, norm_mix_w, norm_ffn_w, final_norm_w, rel_bias_table, ret_w_in, ret_w_out, dil_w_in, dil_w_out,
           dn_w_in, dn_conv_w, dn_a_log, dn_dt_bias, dn_norm_w, dn_w_out, moe_w_router, moe_w_gate, moe_w_up,
           moe_w_down):
    batch, seq, d = x.shape
    x2 = x.reshape(batch * seq, d)
    for i in range(norm_mix_w.shape[0]):
        j = i // N_MIXERS
        kind = i % N_MIXERS
        if kind == 0:
            x2 = _retention_mixer(x2, norm_mix_w[i], (ret_w_in, j), (ret_w_out, j), batch, seq)
        elif kind == 1:
            x2 = _dilated_mixer(x2, norm_mix_w[i], (dil_w_in, j), (dil_w_out, j), rel_bias_table, batch, seq)
        else:
            x2 = _deltanet_mixer(x2, norm_mix_w[i], (dn_w_in, j), dn_conv_w[j], dn_a_log[j], dn_dt_bias[j],
                                 dn_norm_w[j], (dn_w_out, j), batch, seq)
        x2 = _moe(x2, norm_ffn_w[i], moe_w_router[i], moe_w_gate, moe_w_up, moe_w_down, i, batch, seq)
    return _final_norm(x2, final_norm_w).reshape(batch, seq, d)
```

```python
import functools
import math

import jax
import jax.numpy as jnp
import numpy as np
from jax import lax
from jax.experimental import pallas as pl
from jax.experimental.pallas import tpu as pltpu

F32 = jnp.float32
BF16 = jnp.bfloat16

D_MODEL = 2048
EPS = 1e-6
NEG_INF = -1e30
RET_HEADS = 8
RET_DK = D_MODEL // RET_HEADS
RET_DV = 2 * RET_DK
ROPE_BASE = 10000.0
RET_BWD_DECAY_OFFSET = 0.5
DIL_PATTERNS = ((128, 1), (512, 4), (2048, 16))
DIL_HEADS = 16
DIL_DH = D_MODEL // DIL_HEADS
REL_BUCKETS = 32
REL_MAX_DIST = 1024
DN_K_HEADS = 16
DN_V_HEADS = 32
DN_DK = 128
DN_DV = 128
DN_CONV = 5
N_EXPERTS = 16
EXPERT_FF = D_MODEL // 2
CAPACITY_FACTOR = 2
N_MIXERS = 3

LANES = 128
VMEM_LIMIT_BYTES = 56 * 1024 * 1024
ROW_TILE = 1024
PROJ_COL_TILE = 1024
PROJ_PIECE = 256
RET_CHUNK = 256
RET_HEAD_GROUP = 2
MOE_FF_TILE = 256
MOE_DMA_UNROLL = 8


def _cparams(*sem):
    return pltpu.CompilerParams(dimension_semantics=sem, vmem_limit_bytes=VMEM_LIMIT_BYTES)


def _dot(a, b):
    return jnp.dot(a, b, preferred_element_type=F32)


def _dot_nt(a, b):
    return lax.dot_general(a, b, (((1,), (1,)), ((), ())), preferred_element_type=F32)


def _dot_tn(a, b):
    return lax.dot_general(a, b, (((0,), (0,)), ((), ())), preferred_element_type=F32)


def _silu(x):
    return x / (1.0 + jnp.exp(-x))


def _rms_rows(x):
    return x * lax.rsqrt(jnp.mean(x * x, axis=-1, keepdims=True) + EPS)


def _norm_proj_kernel(x_ref, nw_ref, w_ref, o_ref, hn_ref):
    @pl.when(pl.program_id(1) == 0)
    def _():
        hn_ref[...] = (_rms_rows(x_ref[...]) * nw_ref[...]).astype(BF16)

    o_ref[...] = _dot(hn_ref[...], w_ref[...].astype(BF16)).astype(o_ref.dtype)


def _norm_proj_residue_kernel(x_ref, nw_ref, w_ref, o_ref, hn_ref, lanes_ref, *, dil):
    @pl.when(pl.program_id(1) == 0)
    def _():
        h = _rms_rows(x_ref[...]) * nw_ref[...]
        per = h.shape[0] // dil
        for c in range(h.shape[1] // LANES):
            cols = slice(c * LANES, (c + 1) * LANES)
            lanes_ref[...] = h[:, cols]
            for r in range(dil):
                hn_ref[r * per:(r + 1) * per, cols] = lanes_ref[pl.ds(r, per, stride=dil), :].astype(BF16)

    o_ref[...] = _dot(hn_ref[...], w_ref[...].astype(BF16)).astype(o_ref.dtype)


def _norm_proj_rope_kernel(x_ref, nw_ref, w_ref, cos_ref, sin_ref, o_ref, hn_ref, *, n_q_tiles, n_rope_tiles, k_scale):
    j = pl.program_id(1)

    @pl.when(j == 0)
    def _():
        hn_ref[...] = (_rms_rows(x_ref[...]) * nw_ref[...]).astype(BF16)

    c = cos_ref[...]
    s = sin_ref[...]
    half = c.shape[1]
    hn = hn_ref[...]
    rotate = j < n_rope_tiles
    scale = jnp.where(j >= n_q_tiles, k_scale, 1.0).astype(F32)
    accs = [_dot(hn, w_ref[:, 2 * hh * half:(2 * hh + 2) * half].astype(BF16))
            for hh in range(o_ref.shape[1] // (2 * half))]
    for hh, acc in enumerate(accs):
        x1 = acc[:, :half]
        x2 = acc[:, half:]
        o_ref[:, 2 * hh * half:(2 * hh + 1) * half] = jnp.where(rotate, (x1 * c - x2 * s) * scale, x1).astype(o_ref.dtype)
        o_ref[:, (2 * hh + 1) * half:(2 * hh + 2) * half] = jnp.where(rotate, (x2 * c + x1 * s) * scale,
                                                                       x2).astype(o_ref.dtype)


def _norm_proj(x2, nw, w, *, n_out, tn, out_dtype, col_off=0, rope=None, residue_major=1):
    m, k = x2.shape
    w_stack, layer = w
    tm = min(ROW_TILE, m)
    grid = (m // tm, n_out // tn)
    off = col_off // tn
    in_specs = [pl.BlockSpec((tm, k), lambda i, j: (i, 0)),
                pl.BlockSpec((1, k), lambda i, j: (0, 0)),
                pl.BlockSpec((None, k, tn), lambda i, j: (layer, 0, j + off))]
    args = [x2, nw.reshape(1, k), w_stack]
    scratch = [pltpu.VMEM((tm, k), BF16)]
    if residue_major > 1:
        body = functools.partial(_norm_proj_residue_kernel, dil=residue_major)
        scratch.append(pltpu.VMEM((tm, LANES), F32))
    elif rope is None:
        body = _norm_proj_kernel
    else:
        cos, sin, seq, n_q_cols, n_rope_cols, k_scale = rope
        nsb = seq // tm
        in_specs += [pl.BlockSpec((tm, cos.shape[1]), lambda i, j: (i % nsb, 0)),
                     pl.BlockSpec((tm, cos.shape[1]), lambda i, j: (i % nsb, 0))]
        args += [cos, sin]
        body = functools.partial(_norm_proj_rope_kernel, n_q_tiles=n_q_cols // tn, n_rope_tiles=n_rope_cols // tn,
                                 k_scale=k_scale)
    return pl.pallas_call(
        body, name="norm_proj", grid=grid, in_specs=in_specs,
        out_specs=pl.BlockSpec((tm, tn), lambda i, j: (i, j)),
        out_shape=jax.ShapeDtypeStruct((m, n_out), out_dtype),
        scratch_shapes=scratch,
        compiler_params=_cparams("parallel", "arbitrary"),
    )(*args)


def _out_proj_kernel(a_ref, w_ref, res_ref, o_ref):
    o_ref[...] = res_ref[...] + _dot(a_ref[...], w_ref[...].astype(BF16))


def _out_proj(a, w, res, *, tn=512):
    m, k = a.shape
    w_stack, layer = w
    n = w_stack.shape[2]
    tm = min(ROW_TILE, m)
    return pl.pallas_call(
        _out_proj_kernel, name="out_proj", grid=(m // tm, n // tn),
        in_specs=[pl.BlockSpec((tm, k), lambda i, j: (i, 0)),
                  pl.BlockSpec((None, k, tn), lambda i, j: (layer, 0, j)),
                  pl.BlockSpec((tm, tn), lambda i, j: (i, j))],
        out_specs=pl.BlockSpec((tm, tn), lambda i, j: (i, j)),
        out_shape=jax.ShapeDtypeStruct((m, n), F32),
        compiler_params=_cparams("parallel", "arbitrary"),
    )(a, w_stack, res)


def _final_norm_kernel(x_ref, nw_ref, o_ref):
    o_ref[...] = _rms_rows(x_ref[...]) * nw_ref[...]


def _final_norm(x2, nw):
    m, k = x2.shape
    tm = min(ROW_TILE, m)
    return pl.pallas_call(
        _final_norm_kernel, name="final_norm", grid=(m // tm,),
        in_specs=[pl.BlockSpec((tm, k), lambda i: (i, 0)), pl.BlockSpec((1, k), lambda i: (0, 0))],
        out_specs=pl.BlockSpec((tm, k), lambda i: (i, 0)),
        out_shape=jax.ShapeDtypeStruct((m, k), F32),
        compiler_params=_cparams("parallel"),
    )(x2, nw.reshape(1, k))


def _retention_kernel(q_ref, k_ref, v_ref, g_ref, dmat_ref, vec_ref, o_ref, state_ref, oacc_ref):
    sweep = pl.program_id(2)
    c = pl.program_id(3)
    nc = pl.num_programs(3)
    hg = range(RET_HEAD_GROUP)
    qs = [slice(h * RET_DK, (h + 1) * RET_DK) for h in hg]
    vs = [slice(h * RET_DV, (h + 1) * RET_DV) for h in hg]

    @pl.when(c == 0)
    def _():
        state_ref[...] = jnp.zeros_like(state_ref)

    @pl.when(sweep == 0)
    def _():
        scores = [_dot_nt(q_ref[:, qs[h]], k_ref[:, qs[h]]) for h in hg]
        inters = [_dot(q_ref[:, qs[h]], state_ref[h].astype(BF16)) for h in hg]
        kds = [(k_ref[:, qs[h]].astype(F32) * vec_ref[h][:, 2:3]).astype(BF16) for h in hg]
        upds = [_dot_tn(kds[h], v_ref[:, vs[h]]) for h in hg]
        probs = [(scores[h] * dmat_ref[h]).astype(BF16) for h in hg]
        intras = [_dot(probs[h], v_ref[:, vs[h]]) for h in hg]
        for h in hg:
            vec = vec_ref[h]
            oacc_ref[c, :, vs[h]] = intras[h] + inters[h] * vec[:, 0:1]
            state_ref[h] = state_ref[h] * vec[0:1, 4:5] + upds[h]

    @pl.when(sweep == 1)
    def _():
        inters = [_dot(q_ref[:, qs[h]], state_ref[h].astype(BF16)) for h in hg]
        kds = [(k_ref[:, qs[h]].astype(F32) * vec_ref[h][:, 3:4]).astype(BF16) for h in hg]
        upds = [_dot_tn(kds[h], v_ref[:, vs[h]]) for h in hg]
        for h in hg:
            vec = vec_ref[h]
            o = _rms_rows(oacc_ref[nc - 1 - c, :, vs[h]] + inters[h] * vec[:, 1:2])
            o_ref[:, vs[h]] = (o * _silu(g_ref[:, vs[h]].astype(F32))).astype(o_ref.dtype)
            state_ref[h] = state_ref[h] * vec[0:1, 5:6] + upds[h]


def _retention_tables(chunk):
    hh = jnp.arange(RET_HEADS, dtype=F32)
    lg_f = jnp.log1p(-jnp.exp2(-5.0 - hh))
    lg_b = jnp.log1p(-jnp.exp2(-(5.0 + RET_BWD_DECAY_OFFSET) - hh))
    t = jnp.arange(chunk, dtype=F32)
    diff = t[:, None] - t[None, :]
    dmat = jnp.where(diff[None] >= 0,
                     jnp.exp(jnp.maximum(diff, 0.0)[None] * lg_f[:, None, None]),
                     jnp.exp(jnp.maximum(-diff, 0.0)[None] * lg_b[:, None, None]))
    cols = [jnp.exp((t[None, :] + 1.0) * lg_f[:, None]),
            jnp.exp((chunk - t)[None, :] * lg_b[:, None]),
            jnp.exp((chunk - 1.0 - t)[None, :] * lg_f[:, None]),
            jnp.exp(t[None, :] * lg_b[:, None]),
            jnp.broadcast_to(jnp.exp(chunk * lg_f)[:, None], (RET_HEADS, chunk)),
            jnp.broadcast_to(jnp.exp(chunk * lg_b)[:, None], (RET_HEADS, chunk))]
    cols += [jnp.zeros((RET_HEADS, chunk), F32)] * 2
    return dmat, jnp.stack(cols, axis=-1)


def _retention_core(proj, batch, seq):
    cq = min(RET_CHUNK, seq)
    nc = seq // cq
    dmat, vec = _retention_tables(cq)
    g = RET_HEAD_GROUP
    n_hg = RET_HEADS // g
    kq = n_hg
    kv = 2 * RET_HEADS * RET_DK // (g * RET_DV)
    kg = kv + n_hg

    def row(b, s, c):
        return b * nc + c + s * (nc - 1 - 2 * c)

    def row_out(b, s, c):
        return b * nc + nc - 1 - c * s

    return pl.pallas_call(
        _retention_kernel, name="retention_core", grid=(batch, n_hg, 2, nc),
        in_specs=[pl.BlockSpec((cq, g * RET_DK), lambda b, h, s, c: (row(b, s, c), h)),
                  pl.BlockSpec((cq, g * RET_DK), lambda b, h, s, c: (row(b, s, c), kq + h)),
                  pl.BlockSpec((cq, g * RET_DV), lambda b, h, s, c: (row(b, s, c), kv + h)),
                  pl.BlockSpec((cq, g * RET_DV), lambda b, h, s, c: (row_out(b, s, c), kg + h)),
                  pl.BlockSpec((g, cq, cq), lambda b, h, s, c: (h, 0, 0)),
                  pl.BlockSpec((g, cq, 8), lambda b, h, s, c: (h, 0, 0))],
        out_specs=pl.BlockSpec((cq, g * RET_DV), lambda b, h, s, c: (row_out(b, s, c), h)),
        out_shape=jax.ShapeDtypeStruct((batch * seq, RET_HEADS * RET_DV), BF16),
        scratch_shapes=[pltpu.VMEM((g, RET_DK, RET_DV), F32), pltpu.VMEM((nc, cq, g * RET_DV), F32)],
        compiler_params=_cparams("parallel", "parallel", "arbitrary", "arbitrary"),
    )(proj, proj, proj, proj, dmat, vec)


def _rope_tables(seq, half):
    inv = ROPE_BASE ** (-np.arange(half, dtype=np.float64) / half)
    ang = np.arange(seq, dtype=np.float64)[:, None] * inv[None, :]
    return jnp.asarray(np.cos(ang), F32), jnp.asarray(np.sin(ang), F32)


def _retention_mixer(x2, nw, w_in, w_out, batch, seq):
    cos, sin = _rope_tables(seq, RET_DK // 2)
    qk_cols = 2 * RET_HEADS * RET_DK
    proj = _norm_proj(x2, nw, w_in, n_out=w_in[0].shape[2], tn=PROJ_COL_TILE, out_dtype=BF16,
                      rope=(cos, sin, seq, qk_cols // 2, qk_cols, RET_DK ** -0.5))
    o = _retention_core(proj, batch, seq)
    return _out_proj(o, w_out, x2)


def _split_bf16(x):
    hi = x.astype(BF16)
    return hi, (x - hi.astype(F32)).astype(BF16)


def _router_kernel(x_ref, nw_ref, wr_ref, hx_ref, afft_ref):
    half = x_ref.shape[1] // 2
    h = _rms_rows(x_ref[...]) * nw_ref[...]
    h_hi, h_lo = _split_bf16(h)
    bits = lax.bitcast_convert_type(h_hi.astype(F32), jnp.uint32)
    hx_ref[:, :half] = (bits[:, half:] & jnp.uint32(0xFFFF0000)) | (bits[:, :half] >> 16)
    w_hi, w_lo = _split_bf16(wr_ref[...])
    lg = _dot(h_hi, w_hi) + (_dot(h_lo, w_hi) + _dot(h_hi, w_lo))
    lane = lax.broadcasted_iota(jnp.int32, lg.shape, 1)
    lg = jnp.where(lane < N_EXPERTS, lg, NEG_INF)
    e = jnp.exp(lg - jnp.max(lg, axis=1, keepdims=True))
    aff = e / jnp.sum(e, axis=1, keepdims=True)
    hx_ref[:, half:] = lax.bitcast_convert_type(aff, jnp.uint32)
    afft_ref[0] = aff.T[:N_EXPERTS, :]


def _router(x2, nw, w_router, batch, seq):
    m, d = x2.shape
    tm = min(512, seq)
    nsb = seq // tm
    hw = d // 2 + LANES
    wr = jnp.pad(w_router, ((0, 0), (0, LANES - N_EXPERTS)))
    return pl.pallas_call(
        _router_kernel, name="moe_router", grid=(m // tm,),
        in_specs=[pl.BlockSpec((tm, d), lambda i: (i, 0)),
                  pl.BlockSpec((1, d), lambda i: (0, 0)),
                  pl.BlockSpec((d, LANES), lambda i: (0, 0))],
        out_specs=[pl.BlockSpec((tm, hw), lambda i: (i, 0)),
                   pl.BlockSpec((1, N_EXPERTS, tm), lambda i: (i // nsb, 0, i % nsb))],
        out_shape=[jax.ShapeDtypeStruct((m, hw), jnp.uint32),
                   jax.ShapeDtypeStruct((batch, N_EXPERTS, seq), F32)],
        compiler_params=_cparams("parallel"),
    )(x2, nw.reshape(1, d), wr)


def _topk_kernel(aff_ref, idx_ref, loc_ref, off_ref, end_ref, *, cap, n_groups, n_blk):
    n = n_groups * n_blk
    bits = lax.bitcast_convert_type(aff_ref[...], jnp.int32)
    ri = lax.broadcasted_iota(jnp.int32, (n, n), 0)
    ci = lax.broadcasted_iota(jnp.int32, (n, n), 1)
    same = (ri // n_blk) == (ci // n_blk)
    grp_ones = same.astype(BF16)
    grp_before = (same & (ci < ri)).astype(BF16)
    li = lax.broadcasted_iota(jnp.int32, (LANES, LANES), 0)
    lj = lax.broadcasted_iota(jnp.int32, (LANES, LANES), 1)
    incl = (li <= lj).astype(BF16)
    ones = jnp.ones((LANES, LANES), BF16)

    def row_total(mask):
        return _dot(mask.astype(BF16), ones)

    def group_count(mask):
        return _dot(grp_ones, row_total(mask).astype(BF16))

    member = (lax.broadcasted_iota(jnp.int32, (n_groups, n), 1) // n_blk
              == lax.broadcasted_iota(jnp.int32, (n_groups, n), 0)).astype(BF16)
    spread = (lax.broadcasted_iota(jnp.int32, (n, n_groups), 0) // n_blk
              == lax.broadcasted_iota(jnp.int32, (n, n_groups), 1)).astype(BF16)

    def search(i, tau):
        cand = tau | jnp.left_shift(jnp.int32(1), 30 - i)
        per_lane = _dot(member, (bits >= cand).astype(BF16))
        take = (jnp.sum(per_lane, axis=1, keepdims=True) >= cap).astype(BF16)
        take_rows = _dot(spread, jnp.broadcast_to(take, (n_groups, LANES)))
        return jnp.where(take_rows > 0.5, cand, tau)

    tau = lax.fori_loop(0, 31, search, jnp.zeros((n, LANES), jnp.int32))

    def cumsum(mask):
        mb = mask.astype(BF16)
        tot = _dot(mb, ones)
        return _dot(mb, incl), _dot(grp_before, tot.astype(BF16)), tot

    gt = bits > tau
    eq = bits == tau
    need = cap - group_count(gt)
    eq_loc, eq_off, _ = cumsum(eq)
    sel = gt | (eq & (eq_loc + eq_off <= need))
    loc, off, tot = cumsum(sel)
    loc_ref[...] = loc
    off_ref[...] = off
    end_ref[...] = off + tot

    slot = lax.broadcasted_iota(jnp.int32, (cap, LANES), 0).astype(F32)
    lane = lax.broadcasted_iota(jnp.int32, (cap, LANES), 1)
    eye = lax.broadcasted_iota(jnp.int32, (n_blk, LANES), 0) == lax.broadcasted_iota(jnp.int32, (n_blk, LANES), 1)
    pad = jnp.zeros((LANES - n_blk, LANES), BF16)

    def compact(g, acc):
        rows = pl.ds(pl.multiple_of(g * n_blk, n_blk), n_blk)
        end_row = jnp.sum(jnp.where(eye, end_ref[rows, :], 0.0), axis=0, keepdims=True)
        off_row = jnp.sum(jnp.where(eye, off_ref[rows, :], 0.0), axis=0, keepdims=True)
        blk = jnp.sum(((end_row <= slot) & (lane < n_blk)).astype(F32), axis=1, keepdims=True)
        onehot = lane == blk.astype(jnp.int32)
        loc_pad = jnp.concatenate([loc_ref[rows, :].astype(BF16), pad], axis=0)
        in_blk = _dot(onehot.astype(BF16), loc_pad)
        rank = slot - jnp.sum(jnp.where(onehot, off_row, 0.0), axis=1, keepdims=True)
        pos = blk * LANES + jnp.sum((in_blk <= rank).astype(F32), axis=1, keepdims=True)
        return jnp.where(lane == g, pos.astype(jnp.int32), acc)

    idx_ref[...] = lax.fori_loop(0, n_groups, compact, jnp.zeros((cap, LANES), jnp.int32))


def _topk(afft, cap):
    batch, n_e, seq = afft.shape
    n_groups = batch * n_e
    n_blk = seq // LANES
    n = n_groups * n_blk
    assert n_groups <= LANES and n_blk <= LANES
    idx = pl.pallas_call(
        functools.partial(_topk_kernel, cap=cap, n_groups=n_groups, n_blk=n_blk), name="moe_topk",
        out_shape=jax.ShapeDtypeStruct((cap, LANES), jnp.int32),
        scratch_shapes=[pltpu.VMEM((n, LANES), F32)] * 3,
        compiler_params=pltpu.CompilerParams(vmem_limit_bytes=VMEM_LIMIT_BYTES),
    )(afft.reshape(n, LANES))
    return idx[:, :n_groups].T.reshape(batch, n_e, cap)


def _moe_ffn_kernel(idx_ref, hx_hbm, wg_ref, wu_ref, wd_ref, xres_hbm, out_hbm,
                    hbuf, xbf, acc, rbuf, gate, sem_h, sem_r, sem_s):
    del xres_hbm
    e = pl.program_id(0)
    f = pl.program_id(1)
    n_e = pl.num_programs(0)
    nf = pl.num_programs(1)
    rows = xbf.shape[0]
    d = xbf.shape[1]
    per_step = rows // nf
    slot = e % 2
    base = e * rows
    nxt = ((e + 1) % n_e) * rows

    def h_copy(row_id, fq, u, s):
        return pltpu.make_async_copy(hx_hbm.at[pl.ds(row_id, 1), :], hbuf.at[s, fq, pl.ds(u, 1), :], sem_h.at[s])

    def wait_all(buf, sem):
        pltpu.make_async_copy(buf, buf, sem).wait()

    @pl.when((e == 0) & (f == 0))
    def _():
        def start(r, carry):
            h_copy(idx_ref[r], r // per_step, r % per_step, 0).start()
            return carry

        lax.fori_loop(0, rows, start, 0, unroll=MOE_DMA_UNROLL)

    @pl.when(f == 0)
    def _():
        wait_all(hbuf.at[slot], sem_h.at[slot])
        half = d // 2
        packed = hbuf[slot].reshape(rows, half + LANES)
        words = packed[:, :half]
        xbf[:, :half] = lax.bitcast_convert_type(words << 16, F32).astype(BF16)
        xbf[:, half:] = lax.bitcast_convert_type(words & jnp.uint32(0xFFFF0000), F32).astype(BF16)
        lane = lax.broadcasted_iota(jnp.int32, (rows, LANES), 1)
        aff = lax.bitcast_convert_type(packed[:, half:], F32)
        gate[...] = jnp.sum(jnp.where(lane == e, aff, 0.0), axis=1, keepdims=True)

    for u in range(per_step):
        r = f * per_step + u
        h_copy(idx_ref[nxt + r], f, u, 1 - slot).start()
        pltpu.make_async_copy(out_hbm.at[pl.ds(idx_ref[base + r], 1), :], rbuf.at[f, pl.ds(u, 1), :],
                              sem_r.at[0]).start()

    x = xbf[...]
    hid = (_silu(_dot(x, wg_ref[...].astype(BF16))) * _dot(x, wu_ref[...].astype(BF16))).astype(BF16)
    part = _dot(hid, wd_ref[...].astype(BF16))

    @pl.when(f == 0)
    def _():
        acc[...] = part

    @pl.when(f > 0)
    def _():
        acc[...] += part

    @pl.when(f == nf - 1)
    def _():
        wait_all(rbuf, sem_r.at[0])
        rbuf[...] = rbuf[...] + (acc[...] * gate[...]).reshape(rbuf.shape)

        for fq in range(rbuf.shape[0]):
            for u in range(per_step):
                pltpu.make_async_copy(rbuf.at[fq, pl.ds(u, 1), :],
                                      out_hbm.at[pl.ds(idx_ref[base + fq * per_step + u], 1), :], sem_s.at[0]).start()
        wait_all(rbuf, sem_s.at[0])

    @pl.when((e == n_e - 1) & (f == nf - 1))
    def _():
        wait_all(hbuf.at[1 - slot], sem_h.at[1 - slot])


def _moe_ffn(rows_idx, hx, x2, w_gate, w_up, w_down, layer):
    m, d = x2.shape
    _, n_e, _, ff = w_gate.shape
    rows = rows_idx.shape[0] // n_e
    tf = min(MOE_FF_TILE, ff)
    nf = ff // tf
    grid_spec = pltpu.PrefetchScalarGridSpec(
        num_scalar_prefetch=1, grid=(n_e, nf),
        in_specs=[pl.BlockSpec(memory_space=pl.ANY),
                  pl.BlockSpec((None, None, d, tf), lambda e, f, idx: (layer, e, 0, f)),
                  pl.BlockSpec((None, None, d, tf), lambda e, f, idx: (layer, e, 0, f)),
                  pl.BlockSpec((None, None, tf, d), lambda e, f, idx: (layer, e, f, 0)),
                  pl.BlockSpec(memory_space=pl.ANY)],
        out_specs=pl.BlockSpec(memory_space=pl.ANY),
        scratch_shapes=[pltpu.VMEM((2, nf, rows // nf, hx.shape[1]), jnp.uint32), pltpu.VMEM((rows, d), BF16),
                        pltpu.VMEM((rows, d), F32), pltpu.VMEM((nf, rows // nf, d), F32),
                        pltpu.VMEM((rows, 1), F32),
                        pltpu.SemaphoreType.DMA((2,)), pltpu.SemaphoreType.DMA((1,)),
                        pltpu.SemaphoreType.DMA((1,))])
    return pl.pallas_call(
        _moe_ffn_kernel, name="moe_ffn", grid_spec=grid_spec,
        out_shape=jax.ShapeDtypeStruct((m, d), F32),
        input_output_aliases={5: 0},
        compiler_params=_cparams("arbitrary", "arbitrary"),
    )(rows_idx, hx, w_gate, w_up, w_down, x2)


def _moe(x2, nw, w_router, w_gate, w_up, w_down, layer, batch, seq):
    cap = CAPACITY_FACTOR * seq // N_EXPERTS
    hx, afft = _router(x2, nw, w_router, batch, seq)
    idx = _topk(afft, cap)
    rows_idx = idx + (jnp.arange(batch, dtype=jnp.int32) * seq)[:, None, None]
    rows_idx = jnp.transpose(rows_idx, (1, 0, 2)).reshape(-1)
    return _moe_ffn(rows_idx, hx, x2, w_gate, w_up, w_down, layer)


DIL_BLOCK = 64


def _t5_bucket(rel):
    nb = REL_BUCKETS // 2
    max_exact = nb // 2
    ret = jnp.where(rel > 0, nb, 0)
    n = jnp.abs(rel)
    nf = jnp.maximum(n, 1).astype(F32)
    large = max_exact + (jnp.log(nf / max_exact) / math.log(REL_MAX_DIST / max_exact)
                         * (nb - max_exact)).astype(jnp.int32)
    large = jnp.minimum(large, nb - 1)
    return ret + jnp.where(n < max_exact, n, large)


def _dil_bias_kernel(table_ref, bucket_ref, o_ref):
    col = pl.program_id(0)
    bkt = bucket_ref[0]
    acc = jnp.zeros(bkt.shape, F32)
    for b in range(REL_BUCKETS):
        acc = jnp.where(bkt == b, table_ref[b, col], acc)
    s = lax.broadcasted_iota(jnp.int32, bkt.shape, 0)
    t = lax.broadcasted_iota(jnp.int32, bkt.shape, 1)
    o_ref[0] = jnp.where((t >= s) & (t <= s + 2 * DIL_BLOCK), acc, NEG_INF)


def _dil_bias(rel_table):
    qb = DIL_BLOCK
    rel_steps = jnp.arange(3 * qb)[None, :] - jnp.arange(qb)[:, None] - qb
    buckets = jnp.stack([_t5_bucket(rel_steps * dil) for _, dil in DIL_PATTERNS]).astype(jnp.int32)
    n_col = rel_table.shape[1]
    return pl.pallas_call(
        _dil_bias_kernel, name="dil_bias", grid=(n_col,),
        in_specs=[pl.BlockSpec(memory_space=pltpu.SMEM),
                  pl.BlockSpec((1, qb, 3 * qb), lambda c: (c // DIL_HEADS, 0, 0))],
        out_specs=pl.BlockSpec((1, qb, 3 * qb), lambda c: (c, 0, 0)),
        out_shape=jax.ShapeDtypeStruct((n_col, qb, 3 * qb), F32),
        compiler_params=_cparams("parallel"),
    )(rel_table, buckets)


def _dil_attn_kernel(q_ref, kp_ref, kc_ref, kn_ref, vp_ref, vc_ref, vn_ref, bias_ref, o_ref, lse_ref):
    jb = pl.program_id(2)
    nb = pl.num_programs(2)
    qb = q_ref.shape[0]
    t = lax.broadcasted_iota(jnp.int32, (qb, 3 * qb), 1)
    in_seq = ((jb > 0) | (t >= qb)) & ((jb < nb - 1) | (t < 2 * qb))
    kw = jnp.concatenate([kp_ref[...], kc_ref[...], kn_ref[...]], axis=0)
    vw = jnp.concatenate([vp_ref[...], vc_ref[...], vn_ref[...]], axis=0)
    lane = lax.broadcasted_iota(jnp.int32, (qb, LANES), 1)
    lse = jnp.zeros((qb, LANES), F32)
    heads = [slice(h * DIL_DH, (h + 1) * DIL_DH) for h in range(DIL_HEADS)]
    scores = [_dot_nt(q_ref[:, hs], kw[:, hs]) for hs in heads]
    probs, dens = [], []
    for h, s in enumerate(scores):
        s = jnp.where(in_seq, s * (DIL_DH ** -0.5) + bias_ref[h], NEG_INF)
        m = jnp.max(s, axis=1, keepdims=True)
        p = jnp.exp(s - m)
        den = jnp.sum(p, axis=1, keepdims=True)
        probs.append(p.astype(BF16))
        dens.append(den)
        lse = jnp.where(lane == h, m + jnp.log(den), lse)
    outs = [_dot(p, vw[:, hs]) for p, hs in zip(probs, heads)]
    for hs, o, den in zip(heads, outs, dens):
        o_ref[:, hs] = (o / den).astype(o_ref.dtype)
    lse_ref[0] = lse[:, :DIL_HEADS]


def _dil_group(proj, bias, gi, dil, batch, seq):
    qb = DIL_BLOCK
    nb = seq // dil // qb
    hw = DIL_HEADS * DIL_DH
    m = batch * seq
    tm = min(ROW_TILE, m)
    bpt = tm // (dil * qb)
    tiles_per_seq = seq // tm

    def row_block(b, r, j):
        return (b * tiles_per_seq + j // bpt) * (tm // qb) + r * bpt + j % bpt

    def spec(which, shift):
        def imap(b, r, j):
            return (row_block(b, r, jnp.clip(j + shift, 0, nb - 1)), which)
        return pl.BlockSpec((qb, hw), imap)

    o, lse = pl.pallas_call(
        _dil_attn_kernel, name=f"dil_attn_d{dil}", grid=(batch, dil, nb),
        in_specs=[spec(0, 0), spec(1, -1), spec(1, 0), spec(1, 1), spec(2, -1), spec(2, 0), spec(2, 1),
                  pl.BlockSpec((DIL_HEADS, qb, 3 * qb), lambda b, r, j: (gi, 0, 0))],
        out_specs=[pl.BlockSpec((qb, hw), lambda b, r, j: (row_block(b, r, j), 0)),
                   pl.BlockSpec((1, qb, DIL_HEADS), lambda b, r, j: (row_block(b, r, j), 0, 0))],
        out_shape=[jax.ShapeDtypeStruct((m, hw), BF16),
                   jax.ShapeDtypeStruct((m // qb, qb, DIL_HEADS), F32)],
        compiler_params=_cparams("parallel", "parallel", "arbitrary"),
    )(proj, proj, proj, proj, proj, proj, proj, bias)
    lse = lse.reshape(m // tm, dil, tm // dil, DIL_HEADS)
    return o, jnp.transpose(lse, (0, 2, 1, 3)).reshape(m, DIL_HEADS)


def _dil_out_kernel(o0_ref, o1_ref, o2_ref, l0_ref, l1_ref, l2_ref, w_ref, res_ref, out_ref, comb_ref, tok_ref, *,
                    dils):
    @pl.when(pl.program_id(1) == 0)
    def _():
        l0, l1, l2 = l0_ref[...], l1_ref[...], l2_ref[...]
        m = jnp.maximum(jnp.maximum(l0, l1), l2)
        e0, e1, e2 = jnp.exp(l0 - m), jnp.exp(l1 - m), jnp.exp(l2 - m)
        den = e0 + e1 + e2
        wts = (e0 / den, e1 / den, e2 / den)
        rows = comb_ref.shape[0]
        for h in range(DIL_HEADS):
            hs = slice(h * DIL_DH, (h + 1) * DIL_DH)
            total = None
            for g, (o_ref, dil) in enumerate(zip((o0_ref, o1_ref, o2_ref), dils)):
                if dil == 1:
                    o = o_ref[:, hs].astype(F32)
                else:
                    per = rows // dil
                    for r in range(dil):
                        tok_ref[g, pl.ds(r, per, stride=dil), :] = o_ref[r * per:(r + 1) * per, hs].astype(F32)
                    o = tok_ref[g]
                term = wts[g][:, h:h + 1] * o
                total = term if total is None else total + term
            comb_ref[:, hs] = total.astype(BF16)

    out_ref[...] = res_ref[...] + _dot(comb_ref[...], w_ref[...].astype(BF16))


def _dil_out(os_, lses, w, res, *, tn=512):
    m, k = os_[0].shape
    w_stack, layer = w
    n = w_stack.shape[2]
    tm = min(ROW_TILE, m)
    ospec = pl.BlockSpec((tm, k), lambda i, j: (i, 0))
    lspec = pl.BlockSpec((tm, DIL_HEADS), lambda i, j: (i, 0))
    dils = tuple(dil for _, dil in DIL_PATTERNS)
    return pl.pallas_call(
        functools.partial(_dil_out_kernel, dils=dils), name="dil_out", grid=(m // tm, n // tn),
        in_specs=[ospec, ospec, ospec, lspec, lspec, lspec,
                  pl.BlockSpec((None, k, tn), lambda i, j: (layer, 0, j)),
                  pl.BlockSpec((tm, tn), lambda i, j: (i, j))],
        out_specs=pl.BlockSpec((tm, tn), lambda i, j: (i, j)),
        out_shape=jax.ShapeDtypeStruct((m, n), F32),
        scratch_shapes=[pltpu.VMEM((tm, k), BF16), pltpu.VMEM((len(dils), tm, DIL_DH), F32)],
        compiler_params=_cparams("parallel", "arbitrary"),
    )(*os_, *lses, w_stack, res)


def _dilated_mixer(x2, nw, w_in, w_out, rel_table, batch, seq):
    gw = w_in[0].shape[2] // len(DIL_PATTERNS)
    bias = _dil_bias(rel_table)
    outs = []
    for gi, (_, dil) in enumerate(DIL_PATTERNS):
        proj = _norm_proj(x2, nw, w_in, n_out=gw, tn=PROJ_COL_TILE, out_dtype=BF16, col_off=gi * gw,
                          residue_major=dil)
        outs.append(_dil_group(proj, bias, gi, dil, batch, seq))
    return _dil_out([o for o, _ in outs], [l for _, l in outs], w_out, x2)


DN_CHUNK = 64
DN_REP = DN_V_HEADS // DN_K_HEADS
DN_INST = 2 * DN_REP
DN_ROWS = DN_INST * DN_CHUNK
DN_PREP_GROUP = 8
CONV_TILE = 256
CONV_HALO = 8


def _split3_bf16(x):
    hi = x.astype(BF16)
    r = x - hi.astype(F32)
    mid = r.astype(BF16)
    return hi, mid, (r - mid.astype(F32)).astype(BF16)


def _softplus(x):
    return jnp.maximum(x, 0.0) + jnp.log1p(jnp.exp(-jnp.abs(x)))


def _deltanet_kernel(q_ref, k_ref, v_ref, z_ref, tail_ref, cwq_ref, cwk_ref, cwv_ref, alog_ref, dtb_ref, nw_ref,
                     o_ref, qn_ref, kn_ref, vc_ref, mq_ref, bo_ref, state_ref, oacc_ref):
    seq = q_ref.shape[0]
    nc = seq // DN_CHUNK
    c64 = DN_CHUNK
    dk = DN_DK

    pads = [slice(p * dk, (p + 1) * dk) for p in range(oacc_ref.shape[1] // dk)]
    oacc_ref[0:CONV_HALO, :] = jnp.zeros((CONV_HALO, oacc_ref.shape[1]), F32)
    oacc_ref[CONV_HALO + seq:, :] = jnp.zeros((CONV_HALO, oacc_ref.shape[1]), F32)

    def conv_into(jobs):
        ws = []
        for p, (src_ref, col, w_ref, _, _) in enumerate(jobs):
            oacc_ref[CONV_HALO:CONV_HALO + seq, pads[p]] = src_ref[:, col:col + dk].astype(F32)
            ws.append(w_ref[:, col:col + dk])

        def tile(r, carry):
            start = pl.multiple_of(r * CONV_TILE, CONV_TILE)
            ys = []
            for p in range(len(jobs)):
                win = oacc_ref[pl.ds(start, CONV_TILE + 2 * CONV_HALO), pads[p]]
                y = jnp.zeros((CONV_TILE, dk), F32)
                for j in range(DN_CONV):
                    lo = CONV_HALO + j - DN_CONV // 2
                    y = y + win[lo:lo + CONV_TILE, :] * ws[p][j:j + 1, :]
                ys.append(_silu(y))
            for y, (_, col, _, dst_ref, l2_scale) in zip(ys, jobs):
                if l2_scale is not None:
                    y = y * (lax.rsqrt(jnp.sum(y * y, axis=1, keepdims=True) + EPS) * l2_scale)
                dst_ref[pl.ds(start, CONV_TILE), col:col + dk] = y.astype(dst_ref.dtype)
            return carry

        lax.fori_loop(0, seq // CONV_TILE, tile, 0)

    conv_into([(q_ref, 0, cwq_ref, qn_ref, DN_DK ** -0.5), (k_ref, 0, cwk_ref, kn_ref, 1.0)])
    conv_into([(v_ref, vl * DN_DV, cwv_ref, vc_ref, None) for vl in range(DN_REP)])

    rr = lax.broadcasted_iota(jnp.int32, (DN_ROWS, DN_ROWS), 0)
    cc = lax.broadcasted_iota(jnp.int32, (DN_ROWS, DN_ROWS), 1)
    same = (rr // c64) == (cc // c64)
    fwd_rows = rr < DN_REP * c64
    strict = same & ((fwd_rows & (rr > cc)) | (jnp.logical_not(fwd_rows) & (rr < cc)))
    eye = (rr == cc)
    eye_f = eye.astype(F32)
    r64 = lax.broadcasted_iota(jnp.int32, (c64, DN_ROWS), 0)
    c64i = lax.broadcasted_iota(jnp.int32, (c64, DN_ROWS), 1)
    eye_row = (r64 == (c64i % c64)).astype(F32)
    li = lax.broadcasted_iota(jnp.int32, (c64, c64), 0)
    lj = lax.broadcasted_iota(jnp.int32, (c64, c64), 1)
    tril = (lj <= li).astype(BF16)
    triu = (lj >= li).astype(BF16)
    lane_all = lax.broadcasted_iota(jnp.int32, (c64, LANES), 1)
    hk = pl.program_id(1)
    beta_cols = [(i // DN_REP) * DN_V_HEADS + hk * DN_REP + i % DN_REP for i in range(DN_INST)]
    gate_cols = [2 * DN_V_HEADS + col for col in beta_cols]
    row_chain = lax.broadcasted_iota(jnp.int32, (1, DN_ROWS), 1) // c64
    tall_same = (lax.broadcasted_iota(jnp.int32, (DN_INST * dk, DN_ROWS), 0) // dk
                 == lax.broadcasted_iota(jnp.int32, (DN_INST * dk, DN_ROWS), 1) // c64)
    e_r = lax.broadcasted_iota(jnp.int32, (DN_INST * dk, dk), 0)
    e_c = lax.broadcasted_iota(jnp.int32, (DN_INST * dk, dk), 1)
    eye_tall = ((e_r % dk) == e_c).astype(F32)
    neg_a = -jnp.exp(alog_ref[...])
    dtb = dtb_ref[...]

    def stack(cols):
        return jnp.concatenate(cols, axis=0)

    def blockdiag(rows_):
        return jnp.where(same, jnp.concatenate([rows_] * DN_INST, axis=0), 0.0)

    def chunk_maps(c):
        rows = pl.ds(pl.multiple_of(c * c64, c64), c64)
        tl = tail_ref[rows, :]
        beta_all = 1.0 / (1.0 + jnp.exp(-tl))
        g_all = neg_a * _softplus(tl + dtb)
        parts = _split3_bf16(g_all)
        gcf = _dot(tril, parts[0]) + (_dot(tril, parts[1]) + _dot(tril, parts[2]))
        gcb = _dot(triu, parts[0]) + (_dot(triu, parts[1]) + _dot(triu, parts[2]))
        yield

        def pick(a, col):
            return jnp.sum(jnp.where(lane_all == col, a, 0.0), axis=1, keepdims=True)

        beta_b = jnp.broadcast_to(stack([pick(beta_all, beta_cols[i]) for i in range(DN_INST)]), (DN_ROWS, dk))
        gc_b = jnp.broadcast_to(stack([pick(gcf if i < DN_REP else gcb, gate_cols[i]) for i in range(DN_INST)]),
                                (DN_ROWS, dk))
        g_tot = jnp.sum(g_all, axis=0, keepdims=True)
        gtot = [jnp.sum(jnp.where(lane_all[0:1] == gate_cols[i], g_tot, 0.0), axis=1, keepdims=True)
                for i in range(DN_INST)]
        gtot_b = stack([jnp.broadcast_to(t, (c64, dk)) for t in gtot])
        egc_b = jnp.exp(gc_b)

        kb = kn_ref[rows, :]
        qb = qn_ref[rows, :]
        vb = vc_ref[rows, :]
        k4 = stack([kb.astype(F32)] * DN_INST)
        q4 = stack([qb.astype(F32)] * DN_INST)
        v4 = stack([vb[:, (i % DN_REP) * DN_DV:(i % DN_REP + 1) * DN_DV] for i in range(DN_INST)]).astype(F32)
        k4t = k4.T

        gmat = jnp.concatenate([gc_b] * (DN_ROWS // dk), axis=1)
        grow = jnp.sum(jnp.where(eye, gmat, 0.0), axis=0, keepdims=True)
        decay = jnp.exp(jnp.where(strict, gmat - grow, NEG_INF))
        k4t_b = k4t.astype(BF16)
        kk_row = _dot(kb, k4t_b)
        qk_row = _dot(qb, k4t_b)
        yield
        n_bd = jnp.concatenate([beta_b] * (DN_ROWS // dk), axis=1) * stack([kk_row] * DN_INST) * decay
        qk_bd = stack([qk_row] * DN_INST) * (decay + eye_f)

        r_pow = -(n_bd[0:c64] + n_bd[c64:2 * c64] + n_bd[2 * c64:3 * c64] + n_bd[3 * c64:4 * c64])
        t_row = eye_row + r_pow
        r_pow = _dot(r_pow.astype(BF16), (-n_bd).astype(BF16))
        yield
        for _ in range(4):
            p_bd = blockdiag(r_pow).astype(BF16)
            rt = _dot(jnp.concatenate([r_pow, t_row], axis=0).astype(BF16), p_bd)
            yield
            r_pow = rt[0:c64]
            t_row = t_row + rt[c64:]
        t_row = t_row + _dot(t_row.astype(BF16), blockdiag(r_pow).astype(BF16))
        yield

        rhs = jnp.concatenate([k4 * (beta_b * egc_b), v4 * beta_b], axis=1).astype(BF16)
        wu = _dot(blockdiag(t_row).astype(BF16), rhs).astype(BF16)
        yield
        qk_wu = _dot(qk_bd.astype(BF16), wu)
        gtot_row = jnp.zeros((1, DN_ROWS), F32)
        for i in range(DN_INST):
            gtot_row = jnp.where(row_chain == i, gtot[i], gtot_row)
        kdt = k4t * jnp.exp(gtot_row - grow)
        kdt_wide = jnp.where(tall_same, stack([kdt] * DN_INST), 0.0).astype(BF16)
        kw = _dot(kdt_wide, wu)
        yield
        qp = (q4 * egc_b - qk_wu[:, :dk]).astype(BF16)
        op = qk_wu[:, dk:].astype(BF16)
        dl = stack([jnp.broadcast_to(jnp.exp(t), (dk, dk)) for t in gtot])
        return qp, op, (dl * eye_tall - kw[:, :dk]).astype(BF16), kw[:, dk:].astype(BF16)

    def run_interleaved(gens):
        results = [None] * len(gens)
        while any(r is None for r in results):
            for u, gen in enumerate(gens):
                if results[u] is None:
                    try:
                        next(gen)
                    except StopIteration as done:
                        results[u] = done.value
        return results

    half_group = DN_PREP_GROUP // 2
    n_groups = nc // DN_PREP_GROUP

    def prep_group(g):
        low = [g * half_group + u for u in range(half_group)]
        cs = low + [nc - 1 - c for c in low]
        maps = run_interleaved([chunk_maps(c) for c in cs])
        for c, (qp, op, mt, bb) in zip(cs, maps):
            mq_ref[c] = stack([x for i in range(DN_INST)
                               for x in (mt[i * dk:(i + 1) * dk], qp[i * c64:(i + 1) * c64])])
            bo_ref[c] = stack([x for i in range(DN_INST)
                               for x in (bb[i * dk:(i + 1) * dk], op[i * c64:(i + 1) * c64])])

    def sweep_group(g):
        for u in range(half_group):
            j = g * half_group + u
            for i in range(DN_INST):
                c = j if i < DN_REP else nc - 1 - j
                span = slice(i * (dk + c64), (i + 1) * (dk + c64))
                res = _dot(mq_ref[c, span, :], state_ref[i].astype(BF16)) + bo_ref[c, span, :].astype(F32)
                state_ref[i] = res[:dk]
                rows = pl.ds(pl.multiple_of(CONV_HALO + c * c64, CONV_HALO), c64)
                vcols = slice((i % DN_REP) * DN_DV, (i % DN_REP + 1) * DN_DV)
                oacc_ref[rows, vcols] = oacc_ref[rows, vcols] + res[dk:]

    state_ref[...] = jnp.zeros_like(state_ref)
    oacc_ref[...] = jnp.zeros_like(oacc_ref)
    prep_group(0)

    def sweep_and_prep(g, carry):
        sweep_group(g - 1)
        prep_group(g)
        return carry

    lax.fori_loop(1, n_groups, sweep_and_prep, 0)

    def sweep_only(g, carry):
        sweep_group(g)
        return carry

    lax.fori_loop(n_groups - 1, 2 * n_groups, sweep_only, 0)

    def finish(r, carry):
        rows = pl.ds(pl.multiple_of(r * CONV_TILE, CONV_TILE), CONV_TILE)
        acc_rows = pl.ds(pl.multiple_of(CONV_HALO + r * CONV_TILE, CONV_HALO), CONV_TILE)
        for vl in range(DN_REP):
            vcols = slice(vl * DN_DV, (vl + 1) * DN_DV)
            o = _rms_rows(oacc_ref[acc_rows, vcols]) * nw_ref[...]
            o_ref[rows, vcols] = (o * _silu(z_ref[rows, vcols].astype(F32))).astype(o_ref.dtype)
        return carry

    lax.fori_loop(0, seq // CONV_TILE, finish, 0)


def _deltanet_core(proj, tail, conv_w, a_log, dt_bias, norm_w, batch, seq):
    gates = 2 * DN_V_HEADS
    alog = jnp.concatenate([jnp.zeros((gates,), F32), a_log.reshape(gates)]).reshape(1, LANES)
    dtb = jnp.concatenate([jnp.zeros((gates,), F32), dt_bias.reshape(gates)]).reshape(1, LANES)
    nc = seq // DN_CHUNK
    assert seq % (DN_CHUNK * DN_PREP_GROUP) == 0 and seq % CONV_TILE == 0
    kq = DN_K_HEADS
    vw = DN_REP * DN_DV
    kv = 2 * DN_K_HEADS * DN_DK // vw
    kz = kv + DN_K_HEADS
    once = pl.Buffered(1)
    return pl.pallas_call(
        _deltanet_kernel, name="deltanet_core", grid=(batch, DN_K_HEADS),
        in_specs=[pl.BlockSpec((seq, DN_DK), lambda b, h: (b, h), pipeline_mode=once),
                  pl.BlockSpec((seq, DN_DK), lambda b, h: (b, kq + h), pipeline_mode=once),
                  pl.BlockSpec((seq, vw), lambda b, h: (b, kv + h), pipeline_mode=once),
                  pl.BlockSpec((seq, vw), lambda b, h: (b, kz + h), pipeline_mode=once),
                  pl.BlockSpec((seq, LANES), lambda b, h: (b, 0), pipeline_mode=once),
                  pl.BlockSpec((DN_CONV, DN_DK), lambda b, h: (0, h)),
                  pl.BlockSpec((DN_CONV, DN_DK), lambda b, h: (0, kq + h)),
                  pl.BlockSpec((DN_CONV, vw), lambda b, h: (0, kv + h)),
                  pl.BlockSpec((1, LANES), lambda b, h: (0, 0)),
                  pl.BlockSpec((1, LANES), lambda b, h: (0, 0)),
                  pl.BlockSpec((1, DN_DV), lambda b, h: (0, 0))],
        out_specs=pl.BlockSpec((seq, vw), lambda b, h: (b, h)),
        out_shape=jax.ShapeDtypeStruct((batch * seq, DN_V_HEADS * DN_DV), BF16),
        scratch_shapes=[pltpu.VMEM((seq, DN_DK), BF16), pltpu.VMEM((seq, DN_DK), BF16), pltpu.VMEM((seq, vw), BF16),
                        pltpu.VMEM((nc, DN_INST * (DN_DK + DN_CHUNK), DN_DK), BF16),
                        pltpu.VMEM((nc, DN_INST * (DN_DK + DN_CHUNK), DN_DV), BF16),
                        pltpu.VMEM((DN_INST, DN_DK, DN_DV), F32),
                        pltpu.VMEM((seq + 2 * CONV_HALO, vw), F32)],
        compiler_params=_cparams("parallel", "parallel"),
    )(proj, proj, proj, proj, tail, conv_w, conv_w, conv_w, alog, dtb, norm_w.reshape(1, DN_DV))


def _deltanet_mixer(x2, nw, w_in, conv_w, a_log, dt_bias, norm_w, w_out, batch, seq):
    conv_dim = 2 * DN_K_HEADS * DN_DK + DN_V_HEADS * DN_DV
    main = conv_dim + DN_V_HEADS * DN_DV
    proj = _norm_proj(x2, nw, w_in, n_out=main, tn=PROJ_COL_TILE, out_dtype=BF16)
    tail = _norm_proj(x2, nw, w_in, n_out=LANES, tn=LANES, out_dtype=F32, col_off=main)
    o = _deltanet_core(proj, tail, conv_w, a_log, dt_bias, norm_w, batch, seq)
    return _out_proj(o, w_out, x2)


def kernel(x, norm_mix_w, norm_ffn_w, final_norm_w, rel_bias_table, ret_w_in, ret_w_out, dil_w_in, dil_w_out,
           dn_w_in, dn_conv_w, dn_a_log, dn_dt_bias, dn_norm_w, dn_w_out, moe_w_router, moe_w_gate, moe_w_up,
           moe_w_down):
    batch, seq, d = x.shape
    x2 = x.reshape(batch * seq, d)
    for i in range(norm_mix_w.shape[0]):
        j = i // N_MIXERS
        kind = i % N_MIXERS
        if kind == 0:
            x2 = _retention_mixer(x2, norm_mix_w[i], (ret_w_in, j), (ret_w_out, j), batch, seq)
        elif kind == 1:
            x2 = _dilated_mixer(x2, norm_mix_w[i], (dil_w_in, j), (dil_w_out, j), rel_bias_table, batch, seq)
        else:
            x2 = _deltanet_mixer(x2, norm_mix_w[i], (dn_w_in, j), dn_conv_w[j], dn_a_log[j], dn_dt_bias[j],
                                 dn_norm_w[j], (dn_w_out, j), batch, seq)
        x2 = _moe(x2, norm_ffn_w[i], moe_w_router[i], moe_w_gate, moe_w_up, moe_w_down, i, batch, seq)
    return _final_norm(x2, final_norm_w).reshape(batch, seq, d)
```

```python
import functools
import math

import jax
import jax.numpy as jnp
import numpy as np
from jax import lax
from jax.experimental import pallas as pl
from jax.experimental.pallas import tpu as pltpu

F32 = jnp.float32
BF16 = jnp.bfloat16

D_MODEL = 2048
EPS = 1e-6
NEG_INF = -1e30
RET_HEADS = 8
RET_DK = D_MODEL // RET_HEADS
RET_DV = 2 * RET_DK
ROPE_BASE = 10000.0
RET_BWD_DECAY_OFFSET = 0.5
DIL_PATTERNS = ((128, 1), (512, 4), (2048, 16))
DIL_HEADS = 16
DIL_DH = D_MODEL // DIL_HEADS
REL_BUCKETS = 32
REL_MAX_DIST = 1024
DN_K_HEADS = 16
DN_V_HEADS = 32
DN_DK = 128
DN_DV = 128
DN_CONV = 5
N_EXPERTS = 16
EXPERT_FF = D_MODEL // 2
CAPACITY_FACTOR = 2
N_MIXERS = 3

LANES = 128
VMEM_LIMIT_BYTES = 56 * 1024 * 1024
ROW_TILE = 1024
PROJ_COL_TILE = 1024
PROJ_PIECE = 256
RET_CHUNK = 256
RET_HEAD_GROUP = 2
MOE_FF_TILE = 256
MOE_DMA_UNROLL = 8


def _cparams(*sem):
    return pltpu.CompilerParams(dimension_semantics=sem, vmem_limit_bytes=VMEM_LIMIT_BYTES)


def _dot(a, b):
    return jnp.dot(a, b, preferred_element_type=F32)


def _dot_nt(a, b):
    return lax.dot_general(a, b, (((1,), (1,)), ((), ())), preferred_element_type=F32)


def _dot_tn(a, b):
    return lax.dot_general(a, b, (((0,), (0,)), ((), ())), preferred_element_type=F32)


def _silu(x):
    return x / (1.0 + jnp.exp(-x))


def _rms_rows(x):
    return x * lax.rsqrt(jnp.mean(x * x, axis=-1, keepdims=True) + EPS)


def _norm_proj_kernel(x_ref, nw_ref, w_ref, o_ref, hn_ref):
    @pl.when(pl.program_id(1) == 0)
    def _():
        hn_ref[...] = (_rms_rows(x_ref[...]) * nw_ref[...]).astype(BF16)

    o_ref[...] = _dot(hn_ref[...], w_ref[...].astype(BF16)).astype(o_ref.dtype)


def _norm_proj_residue_kernel(x_ref, nw_ref, w_ref, o_ref, hn_ref, lanes_ref, *, dil):
    @pl.when(pl.program_id(1) == 0)
    def _():
        h = _rms_rows(x_ref[...]) * nw_ref[...]
        per = h.shape[0] // dil
        for c in range(h.shape[1] // LANES):
            cols = slice(c * LANES, (c + 1) * LANES)
            lanes_ref[...] = h[:, cols]
            for r in range(dil):
                hn_ref[r * per:(r + 1) * per, cols] = lanes_ref[pl.ds(r, per, stride=dil), :].astype(BF16)

    o_ref[...] = _dot(hn_ref[...], w_ref[...].astype(BF16)).astype(o_ref.dtype)


def _norm_proj_rope_kernel(x_ref, nw_ref, w_ref, cos_ref, sin_ref, o_ref, hn_ref, *, n_q_tiles, n_rope_tiles, k_scale):
    j = pl.program_id(1)

    @pl.when(j == 0)
    def _():
        hn_ref[...] = (_rms_rows(x_ref[...]) * nw_ref[...]).astype(BF16)

    c = cos_ref[...]
    s = sin_ref[...]
    half = c.shape[1]
    hn = hn_ref[...]
    rotate = j < n_rope_tiles
    scale = jnp.where(j >= n_q_tiles, k_scale, 1.0).astype(F32)
    accs = [_dot(hn, w_ref[:, 2 * hh * half:(2 * hh + 2) * half].astype(BF16))
            for hh in range(o_ref.shape[1] // (2 * half))]
    for hh, acc in enumerate(accs):
        x1 = acc[:, :half]
        x2 = acc[:, half:]
        o_ref[:, 2 * hh * half:(2 * hh + 1) * half] = jnp.where(rotate, (x1 * c - x2 * s) * scale, x1).astype(o_ref.dtype)
        o_ref[:, (2 * hh + 1) * half:(2 * hh + 2) * half] = jnp.where(rotate, (x2 * c + x1 * s) * scale,
                                                                       x2).astype(o_ref.dtype)


def _norm_proj(x2, nw, w, *, n_out, tn, out_dtype, col_off=0, rope=None, residue_major=1):
    m, k = x2.shape
    w_stack, layer = w
    tm = min(ROW_TILE, m)
    grid = (m // tm, n_out // tn)
    off = col_off // tn
    in_specs = [pl.BlockSpec((tm, k), lambda i, j: (i, 0)),
                pl.BlockSpec((1, k), lambda i, j: (0, 0)),
                pl.BlockSpec((None, k, tn), lambda i, j: (layer, 0, j + off))]
    args = [x2, nw.reshape(1, k), w_stack]
    scratch = [pltpu.VMEM((tm, k), BF16)]
    if residue_major > 1:
        body = functools.partial(_norm_proj_residue_kernel, dil=residue_major)
        scratch.append(pltpu.VMEM((tm, LANES), F32))
    elif rope is None:
        body = _norm_proj_kernel
    else:
        cos, sin, seq, n_q_cols, n_rope_cols, k_scale = rope
        nsb = seq // tm
        in_specs += [pl.BlockSpec((tm, cos.shape[1]), lambda i, j: (i % nsb, 0)),
                     pl.BlockSpec((tm, cos.shape[1]), lambda i, j: (i % nsb, 0))]
        args += [cos, sin]
        body = functools.partial(_norm_proj_rope_kernel, n_q_tiles=n_q_cols // tn, n_rope_tiles=n_rope_cols // tn,
                                 k_scale=k_scale)
    return pl.pallas_call(
        body, name="norm_proj", grid=grid, in_specs=in_specs,
        out_specs=pl.BlockSpec((tm, tn), lambda i, j: (i, j)),
        out_shape=jax.ShapeDtypeStruct((m, n_out), out_dtype),
        scratch_shapes=scratch,
        compiler_params=_cparams("parallel", "arbitrary"),
    )(*args)


def _out_proj_kernel(a_ref, w_ref, res_ref, o_ref):
    o_ref[...] = res_ref[...] + _dot(a_ref[...], w_ref[...].astype(BF16))


def _out_proj(a, w, res, *, tn=512):
    m, k = a.shape
    w_stack, layer = w
    n = w_stack.shape[2]
    tm = min(ROW_TILE, m)
    return pl.pallas_call(
        _out_proj_kernel, name="out_proj", grid=(m // tm, n // tn),
        in_specs=[pl.BlockSpec((tm, k), lambda i, j: (i, 0)),
                  pl.BlockSpec((None, k, tn), lambda i, j: (layer, 0, j)),
                  pl.BlockSpec((tm, tn), lambda i, j: (i, j))],
        out_specs=pl.BlockSpec((tm, tn), lambda i, j: (i, j)),
        out_shape=jax.ShapeDtypeStruct((m, n), F32),
        compiler_params=_cparams("parallel", "arbitrary"),
    )(a, w_stack, res)


def _final_norm_kernel(x_ref, nw_ref, o_ref):
    o_ref[...] = _rms_rows(x_ref[...]) * nw_ref[...]


def _final_norm(x2, nw):
    m, k = x2.shape
    tm = min(ROW_TILE, m)
    return pl.pallas_call(
        _final_norm_kernel, name="final_norm", grid=(m // tm,),
        in_specs=[pl.BlockSpec((tm, k), lambda i: (i, 0)), pl.BlockSpec((1, k), lambda i: (0, 0))],
        out_specs=pl.BlockSpec((tm, k), lambda i: (i, 0)),
        out_shape=jax.ShapeDtypeStruct((m, k), F32),
        compiler_params=_cparams("parallel"),
    )(x2, nw.reshape(1, k))


def _retention_kernel(q_ref, k_ref, v_ref, g_ref, dmat_ref, vec_ref, o_ref, state_ref, oacc_ref):
    sweep = pl.program_id(2)
    c = pl.program_id(3)
    nc = pl.num_programs(3)
    hg = range(RET_HEAD_GROUP)
    qs = [slice(h * RET_DK, (h + 1) * RET_DK) for h in hg]
    vs = [slice(h * RET_DV, (h + 1) * RET_DV) for h in hg]

    @pl.when(c == 0)
    def _():
        state_ref[...] = jnp.zeros_like(state_ref)

    @pl.when(sweep == 0)
    def _():
        scores = [_dot_nt(q_ref[:, qs[h]], k_ref[:, qs[h]]) for h in hg]
        inters = [_dot(q_ref[:, qs[h]], state_ref[h].astype(BF16)) for h in hg]
        kds = [(k_ref[:, qs[h]].astype(F32) * vec_ref[h][:, 2:3]).astype(BF16) for h in hg]
        upds = [_dot_tn(kds[h], v_ref[:, vs[h]]) for h in hg]
        probs = [(scores[h] * dmat_ref[h]).astype(BF16) for h in hg]
        intras = [_dot(probs[h], v_ref[:, vs[h]]) for h in hg]
        for h in hg:
            vec = vec_ref[h]
            oacc_ref[c, :, vs[h]] = intras[h] + inters[h] * vec[:, 0:1]
            state_ref[h] = state_ref[h] * vec[0:1, 4:5] + upds[h]

    @pl.when(sweep == 1)
    def _():
        inters = [_dot(q_ref[:, qs[h]], state_ref[h].astype(BF16)) for h in hg]
        kds = [(k_ref[:, qs[h]].astype(F32) * vec_ref[h][:, 3:4]).astype(BF16) for h in hg]
        upds = [_dot_tn(kds[h], v_ref[:, vs[h]]) for h in hg]
        for h in hg:
            vec = vec_ref[h]
            o = _rms_rows(oacc_ref[nc - 1 - c, :, vs[h]] + inters[h] * vec[:, 1:2])
            o_ref[:, vs[h]] = (o * _silu(g_ref[:, vs[h]].astype(F32))).astype(o_ref.dtype)
            state_ref[h] = state_ref[h] * vec[0:1, 5:6] + upds[h]


def _retention_tables(chunk):
    hh = jnp.arange(RET_HEADS, dtype=F32)
    lg_f = jnp.log1p(-jnp.exp2(-5.0 - hh))
    lg_b = jnp.log1p(-jnp.exp2(-(5.0 + RET_BWD_DECAY_OFFSET) - hh))
    t = jnp.arange(chunk, dtype=F32)
    diff = t[:, None] - t[None, :]
    dmat = jnp.where(diff[None] >= 0,
                     jnp.exp(jnp.maximum(diff, 0.0)[None] * lg_f[:, None, None]),
                     jnp.exp(jnp.maximum(-diff, 0.0)[None] * lg_b[:, None, None]))
    cols = [jnp.exp((t[None, :] + 1.0) * lg_f[:, None]),
            jnp.exp((chunk - t)[None, :] * lg_b[:, None]),
            jnp.exp((chunk - 1.0 - t)[None, :] * lg_f[:, None]),
            jnp.exp(t[None, :] * lg_b[:, None]),
            jnp.broadcast_to(jnp.exp(chunk * lg_f)[:, None], (RET_HEADS, chunk)),
            jnp.broadcast_to(jnp.exp(chunk * lg_b)[:, None], (RET_HEADS, chunk))]
    cols += [jnp.zeros((RET_HEADS, chunk), F32)] * 2
    return dmat, jnp.stack(cols, axis=-1)


def _retention_core(proj, batch, seq):
    cq = min(RET_CHUNK, seq)
    nc = seq // cq
    dmat, vec = _retention_tables(cq)
    g = RET_HEAD_GROUP
    n_hg = RET_HEADS // g
    kq = n_hg
    kv = 2 * RET_HEADS * RET_DK // (g * RET_DV)
    kg = kv + n_hg

    def row(b, s, c):
        return b * nc + c + s * (nc - 1 - 2 * c)

    def row_out(b, s, c):
        return b * nc + nc - 1 - c * s

    return pl.pallas_call(
        _retention_kernel, name="retention_core", grid=(batch, n_hg, 2, nc),
        in_specs=[pl.BlockSpec((cq, g * RET_DK), lambda b, h, s, c: (row(b, s, c), h)),
                  pl.BlockSpec((cq, g * RET_DK), lambda b, h, s, c: (row(b, s, c), kq + h)),
                  pl.BlockSpec((cq, g * RET_DV), lambda b, h, s, c: (row(b, s, c), kv + h)),
                  pl.BlockSpec((cq, g * RET_DV), lambda b, h, s, c: (row_out(b, s, c), kg + h)),
                  pl.BlockSpec((g, cq, cq), lambda b, h, s, c: (h, 0, 0)),
                  pl.BlockSpec((g, cq, 8), lambda b, h, s, c: (h, 0, 0))],
        out_specs=pl.BlockSpec((cq, g * RET_DV), lambda b, h, s, c: (row_out(b, s, c), h)),
        out_shape=jax.ShapeDtypeStruct((batch * seq, RET_HEADS * RET_DV), BF16),
        scratch_shapes=[pltpu.VMEM((g, RET_DK, RET_DV), F32), pltpu.VMEM((nc, cq, g * RET_DV), F32)],
        compiler_params=_cparams("parallel", "parallel", "arbitrary", "arbitrary"),
    )(proj, proj, proj, proj, dmat, vec)


def _rope_tables(seq, half):
    inv = ROPE_BASE ** (-np.arange(half, dtype=np.float64) / half)
    ang = np.arange(seq, dtype=np.float64)[:, None] * inv[None, :]
    return jnp.asarray(np.cos(ang), F32), jnp.asarray(np.sin(ang), F32)


def _retention_mixer(x2, nw, w_in, w_out, batch, seq):
    cos, sin = _rope_tables(seq, RET_DK // 2)
    qk_cols = 2 * RET_HEADS * RET_DK
    proj = _norm_proj(x2, nw, w_in, n_out=w_in[0].shape[2], tn=PROJ_COL_TILE, out_dtype=BF16,
                      rope=(cos, sin, seq, qk_cols // 2, qk_cols, RET_DK ** -0.5))
    o = _retention_core(proj, batch, seq)
    return _out_proj(o, w_out, x2)


def _split_bf16(x):
    hi = x.astype(BF16)
    return hi, (x - hi.astype(F32)).astype(BF16)


def _router_kernel(x_ref, nw_ref, wr_ref, hx_ref, afft_ref):
    half = x_ref.shape[1] // 2
    h = _rms_rows(x_ref[...]) * nw_ref[...]
    h_hi, h_lo = _split_bf16(h)
    bits = lax.bitcast_convert_type(h_hi.astype(F32), jnp.uint32)
    hx_ref[:, :half] = (bits[:, half:] & jnp.uint32(0xFFFF0000)) | (bits[:, :half] >> 16)
    w_hi, w_lo = _split_bf16(wr_ref[...])
    lg = _dot(h_hi, w_hi) + (_dot(h_lo, w_hi) + _dot(h_hi, w_lo))
    lane = lax.broadcasted_iota(jnp.int32, lg.shape, 1)
    lg = jnp.where(lane < N_EXPERTS, lg, NEG_INF)
    e = jnp.exp(lg - jnp.max(lg, axis=1, keepdims=True))
    aff = e / jnp.sum(e, axis=1, keepdims=True)
    hx_ref[:, half:] = lax.bitcast_convert_type(aff, jnp.uint32)
    afft_ref[0] = aff.T[:N_EXPERTS, :]


def _router(x2, nw, w_router, batch, seq):
    m, d = x2.shape
    tm = min(512, seq)
    nsb = seq // tm
    hw = d // 2 + LANES
    wr = jnp.pad(w_router, ((0, 0), (0, LANES - N_EXPERTS)))
    return pl.pallas_call(
        _router_kernel, name="moe_router", grid=(m // tm,),
        in_specs=[pl.BlockSpec((tm, d), lambda i: (i, 0)),
                  pl.BlockSpec((1, d), lambda i: (0, 0)),
                  pl.BlockSpec((d, LANES), lambda i: (0, 0))],
        out_specs=[pl.BlockSpec((tm, hw), lambda i: (i, 0)),
                   pl.BlockSpec((1, N_EXPERTS, tm), lambda i: (i // nsb, 0, i % nsb))],
        out_shape=[jax.ShapeDtypeStruct((m, hw), jnp.uint32),
                   jax.ShapeDtypeStruct((batch, N_EXPERTS, seq), F32)],
        compiler_params=_cparams("parallel"),
    )(x2, nw.reshape(1, d), wr)


def _topk_kernel(aff_ref, idx_ref, loc_ref, off_ref, end_ref, *, cap, n_groups, n_blk):
    n = n_groups * n_blk
    bits = lax.bitcast_convert_type(aff_ref[...], jnp.int32)
    ri = lax.broadcasted_iota(jnp.int32, (n, n), 0)
    ci = lax.broadcasted_iota(jnp.int32, (n, n), 1)
    same = (ri // n_blk) == (ci // n_blk)
    grp_ones = same.astype(BF16)
    grp_before = (same & (ci < ri)).astype(BF16)
    li = lax.broadcasted_iota(jnp.int32, (LANES, LANES), 0)
    lj = lax.broadcasted_iota(jnp.int32, (LANES, LANES), 1)
    incl = (li <= lj).astype(BF16)
    ones = jnp.ones((LANES, LANES), BF16)

    def row_total(mask):
        return _dot(mask.astype(BF16), ones)

    def group_count(mask):
        return _dot(grp_ones, row_total(mask).astype(BF16))

    member = (lax.broadcasted_iota(jnp.int32, (n_groups, n), 1) // n_blk
              == lax.broadcasted_iota(jnp.int32, (n_groups, n), 0)).astype(BF16)
    spread = (lax.broadcasted_iota(jnp.int32, (n, n_groups), 0) // n_blk
              == lax.broadcasted_iota(jnp.int32, (n, n_groups), 1)).astype(BF16)

    def search(i, tau):
        cand = tau | jnp.left_shift(jnp.int32(1), 30 - i)
        per_lane = _dot(member, (bits >= cand).astype(BF16))
        take = (jnp.sum(per_lane, axis=1, keepdims=True) >= cap).astype(BF16)
        take_rows = _dot(spread, jnp.broadcast_to(take, (n_groups, LANES)))
        return jnp.where(take_rows > 0.5, cand, tau)

    tau = lax.fori_loop(0, 31, search, jnp.zeros((n, LANES), jnp.int32))

    def cumsum(mask):
        mb = mask.astype(BF16)
        tot = _dot(mb, ones)
        return _dot(mb, incl), _dot(grp_before, tot.astype(BF16)), tot

    gt = bits > tau
    eq = bits == tau
    need = cap - group_count(gt)
    eq_loc, eq_off, _ = cumsum(eq)
    sel = gt | (eq & (eq_loc + eq_off <= need))
    loc, off, tot = cumsum(sel)
    loc_ref[...] = loc
    off_ref[...] = off
    end_ref[...] = off + tot

    slot = lax.broadcasted_iota(jnp.int32, (cap, LANES), 0).astype(F32)
    lane = lax.broadcasted_iota(jnp.int32, (cap, LANES), 1)
    eye = lax.broadcasted_iota(jnp.int32, (n_blk, LANES), 0) == lax.broadcasted_iota(jnp.int32, (n_blk, LANES), 1)
    pad = jnp.zeros((LANES - n_blk, LANES), BF16)

    def compact(g, acc):
        rows = pl.ds(pl.multiple_of(g * n_blk, n_blk), n_blk)
        end_row = jnp.sum(jnp.where(eye, end_ref[rows, :], 0.0), axis=0, keepdims=True)
        off_row = jnp.sum(jnp.where(eye, off_ref[rows, :], 0.0), axis=0, keepdims=True)
        blk = jnp.sum(((end_row <= slot) & (lane < n_blk)).astype(F32), axis=1, keepdims=True)
        onehot = lane == blk.astype(jnp.int32)
        loc_pad = jnp.concatenate([loc_ref[rows, :].astype(BF16), pad], axis=0)
        in_blk = _dot(onehot.astype(BF16), loc_pad)
        rank = slot - jnp.sum(jnp.where(onehot, off_row, 0.0), axis=1, keepdims=True)
        pos = blk * LANES + jnp.sum((in_blk <= rank).astype(F32), axis=1, keepdims=True)
        return jnp.where(lane == g, pos.astype(jnp.int32), acc)

    idx_ref[...] = lax.fori_loop(0, n_groups, compact, jnp.zeros((cap, LANES), jnp.int32))


def _topk(afft, cap):
    batch, n_e, seq = afft.shape
    n_groups = batch * n_e
    n_blk = seq // LANES
    n = n_groups * n_blk
    assert n_groups <= LANES and n_blk <= LANES
    idx = pl.pallas_call(
        functools.partial(_topk_kernel, cap=cap, n_groups=n_groups, n_blk=n_blk), name="moe_topk",
        out_shape=jax.ShapeDtypeStruct((cap, LANES), jnp.int32),
        scratch_shapes=[pltpu.VMEM((n, LANES), F32)] * 3,
        compiler_params=pltpu.CompilerParams(vmem_limit_bytes=VMEM_LIMIT_BYTES),
    )(afft.reshape(n, LANES))
    return idx[:, :n_groups].T.reshape(batch, n_e, cap)


def _moe_ffn_kernel(idx_ref, hx_hbm, wg_ref, wu_ref, wd_ref, xres_hbm, out_hbm,
                    hbuf, xbf, acc, rbuf, gate, sem_h, sem_r, sem_s):
    del xres_hbm
    e = pl.program_id(0)
    f = pl.program_id(1)
    n_e = pl.num_programs(0)
    nf = pl.num_programs(1)
    rows = xbf.shape[0]
    d = xbf.shape[1]
    per_step = rows // nf
    slot = e % 2
    base = e * rows
    nxt = ((e + 1) % n_e) * rows

    def h_copy(row_id, fq, u, s):
        return pltpu.make_async_copy(hx_hbm.at[pl.ds(row_id, 1), :], hbuf.at[s, fq, pl.ds(u, 1), :], sem_h.at[s])

    def wait_all(buf, sem):
        pltpu.make_async_copy(buf, buf, sem).wait()

    @pl.when((e == 0) & (f == 0))
    def _():
        def start(r, carry):
            h_copy(idx_ref[r], r // per_step, r % per_step, 0).start()
            return carry

        lax.fori_loop(0, rows, start, 0, unroll=MOE_DMA_UNROLL)

    @pl.when(f == 0)
    def _():
        wait_all(hbuf.at[slot], sem_h.at[slot])
        half = d // 2
        packed = hbuf[slot].reshape(rows, half + LANES)
        words = packed[:, :half]
        xbf[:, :half] = lax.bitcast_convert_type(words << 16, F32).astype(BF16)
        xbf[:, half:] = lax.bitcast_convert_type(words & jnp.uint32(0xFFFF0000), F32).astype(BF16)
        lane = lax.broadcasted_iota(jnp.int32, (rows, LANES), 1)
        aff = lax.bitcast_convert_type(packed[:, half:], F32)
        gate[...] = jnp.sum(jnp.where(lane == e, aff, 0.0), axis=1, keepdims=True)

    for u in range(per_step):
        r = f * per_step + u
        h_copy(idx_ref[nxt + r], f, u, 1 - slot).start()
        pltpu.make_async_copy(out_hbm.at[pl.ds(idx_ref[base + r], 1), :], rbuf.at[f, pl.ds(u, 1), :],
                              sem_r.at[0]).start()

    x = xbf[...]
    hid = (_silu(_dot(x, wg_ref[...].astype(BF16))) * _dot(x, wu_ref[...].astype(BF16))).astype(BF16)
    part = _dot(hid, wd_ref[...].astype(BF16))

    @pl.when(f == 0)
    def _():
        acc[...] = part

    @pl.when(f > 0)
    def _():
        acc[...] += part

    @pl.when(f == nf - 1)
    def _():
        wait_all(rbuf, sem_r.at[0])
        rbuf[...] = rbuf[...] + (acc[...] * gate[...]).reshape(rbuf.shape)

        for fq in range(rbuf.shape[0]):
            for u in range(per_step):
                pltpu.make_async_copy(rbuf.at[fq, pl.ds(u, 1), :],
                                      out_hbm.at[pl.ds(idx_ref[base + fq * per_step + u], 1), :],
                                      sem_s.at[0]).start(priority=u % 2)
        wait_all(rbuf, sem_s.at[0])

    @pl.when((e == n_e - 1) & (f == nf - 1))
    def _():
        wait_all(hbuf.at[1 - slot], sem_h.at[1 - slot])


def _moe_ffn(rows_idx, hx, x2, w_gate, w_up, w_down, layer):
    m, d = x2.shape
    _, n_e, _, ff = w_gate.shape
    rows = rows_idx.shape[0] // n_e
    tf = min(MOE_FF_TILE, ff)
    nf = ff // tf
    grid_spec = pltpu.PrefetchScalarGridSpec(
        num_scalar_prefetch=1, grid=(n_e, nf),
        in_specs=[pl.BlockSpec(memory_space=pl.ANY),
                  pl.BlockSpec((None, None, d, tf), lambda e, f, idx: (layer, e, 0, f)),
                  pl.BlockSpec((None, None, d, tf), lambda e, f, idx: (layer, e, 0, f)),
                  pl.BlockSpec((None, None, tf, d), lambda e, f, idx: (layer, e, f, 0)),
                  pl.BlockSpec(memory_space=pl.ANY)],
        out_specs=pl.BlockSpec(memory_space=pl.ANY),
        scratch_shapes=[pltpu.VMEM((2, nf, rows // nf, hx.shape[1]), jnp.uint32), pltpu.VMEM((rows, d), BF16),
                        pltpu.VMEM((rows, d), F32), pltpu.VMEM((nf, rows // nf, d), F32),
                        pltpu.VMEM((rows, 1), F32),
                        pltpu.SemaphoreType.DMA((2,)), pltpu.SemaphoreType.DMA((1,)),
                        pltpu.SemaphoreType.DMA((1,))])
    return pl.pallas_call(
        _moe_ffn_kernel, name="moe_ffn", grid_spec=grid_spec,
        out_shape=jax.ShapeDtypeStruct((m, d), F32),
        input_output_aliases={5: 0},
        compiler_params=_cparams("arbitrary", "arbitrary"),
    )(rows_idx, hx, w_gate, w_up, w_down, x2)


def _moe(x2, nw, w_router, w_gate, w_up, w_down, layer, batch, seq):
    cap = CAPACITY_FACTOR * seq // N_EXPERTS
    hx, afft = _router(x2, nw, w_router, batch, seq)
    idx = _topk(afft, cap)
    rows_idx = idx + (jnp.arange(batch, dtype=jnp.int32) * seq)[:, None, None]
    rows_idx = jnp.transpose(rows_idx, (1, 0, 2)).reshape(-1)
    return _moe_ffn(rows_idx, hx, x2, w_gate, w_up, w_down, layer)


DIL_BLOCK = 64


def _t5_bucket(rel):
    nb = REL_BUCKETS // 2
    max_exact = nb // 2
    ret = jnp.where(rel > 0, nb, 0)
    n = jnp.abs(rel)
    nf = jnp.maximum(n, 1).astype(F32)
    large = max_exact + (jnp.log(nf / max_exact) / math.log(REL_MAX_DIST / max_exact)
                         * (nb - max_exact)).astype(jnp.int32)
    large = jnp.minimum(large, nb - 1)
    return ret + jnp.where(n < max_exact, n, large)


def _dil_bias_kernel(table_ref, bucket_ref, o_ref):
    col = pl.program_id(0)
    bkt = bucket_ref[0]
    acc = jnp.zeros(bkt.shape, F32)
    for b in range(REL_BUCKETS):
        acc = jnp.where(bkt == b, table_ref[b, col], acc)
    s = lax.broadcasted_iota(jnp.int32, bkt.shape, 0)
    t = lax.broadcasted_iota(jnp.int32, bkt.shape, 1)
    o_ref[0] = jnp.where((t >= s) & (t <= s + 2 * DIL_BLOCK), acc, NEG_INF)


def _dil_bias(rel_table):
    qb = DIL_BLOCK
    rel_steps = jnp.arange(3 * qb)[None, :] - jnp.arange(qb)[:, None] - qb
    buckets = jnp.stack([_t5_bucket(rel_steps * dil) for _, dil in DIL_PATTERNS]).astype(jnp.int32)
    n_col = rel_table.shape[1]
    return pl.pallas_call(
        _dil_bias_kernel, name="dil_bias", grid=(n_col,),
        in_specs=[pl.BlockSpec(memory_space=pltpu.SMEM),
                  pl.BlockSpec((1, qb, 3 * qb), lambda c: (c // DIL_HEADS, 0, 0))],
        out_specs=pl.BlockSpec((1, qb, 3 * qb), lambda c: (c, 0, 0)),
        out_shape=jax.ShapeDtypeStruct((n_col, qb, 3 * qb), F32),
        compiler_params=_cparams("parallel"),
    )(rel_table, buckets)


def _dil_attn_kernel(q_ref, kp_ref, kc_ref, kn_ref, vp_ref, vc_ref, vn_ref, bias_ref, o_ref, lse_ref):
    jb = pl.program_id(2)
    nb = pl.num_programs(2)
    qb = q_ref.shape[0]
    t = lax.broadcasted_iota(jnp.int32, (qb, 3 * qb), 1)
    in_seq = ((jb > 0) | (t >= qb)) & ((jb < nb - 1) | (t < 2 * qb))
    kw = jnp.concatenate([kp_ref[...], kc_ref[...], kn_ref[...]], axis=0)
    vw = jnp.concatenate([vp_ref[...], vc_ref[...], vn_ref[...]], axis=0)
    lane = lax.broadcasted_iota(jnp.int32, (qb, LANES), 1)
    lse = jnp.zeros((qb, LANES), F32)
    heads = [slice(h * DIL_DH, (h + 1) * DIL_DH) for h in range(DIL_HEADS)]
    scores = [_dot_nt(q_ref[:, hs], kw[:, hs]) for hs in heads]
    probs, dens = [], []
    for h, s in enumerate(scores):
        s = jnp.where(in_seq, s * (DIL_DH ** -0.5) + bias_ref[h], NEG_INF)
        m = jnp.max(s, axis=1, keepdims=True)
        p = jnp.exp(s - m)
        den = jnp.sum(p, axis=1, keepdims=True)
        probs.append(p.astype(BF16))
        dens.append(den)
        lse = jnp.where(lane == h, m + jnp.log(den), lse)
    outs = [_dot(p, vw[:, hs]) for p, hs in zip(probs, heads)]
    for hs, o, den in zip(heads, outs, dens):
        o_ref[:, hs] = (o / den).astype(o_ref.dtype)
    lse_ref[0] = lse[:, :DIL_HEADS]


def _dil_group(proj, bias, gi, dil, batch, seq):
    qb = DIL_BLOCK
    nb = seq // dil // qb
    hw = DIL_HEADS * DIL_DH
    m = batch * seq
    tm = min(ROW_TILE, m)
    bpt = tm // (dil * qb)
    tiles_per_seq = seq // tm

    def row_block(b, r, j):
        return (b * tiles_per_seq + j // bpt) * (tm // qb) + r * bpt + j % bpt

    def spec(which, shift):
        def imap(b, r, j):
            return (row_block(b, r, jnp.clip(j + shift, 0, nb - 1)), which)
        return pl.BlockSpec((qb, hw), imap)

    o, lse = pl.pallas_call(
        _dil_attn_kernel, name=f"dil_attn_d{dil}", grid=(batch, dil, nb),
        in_specs=[spec(0, 0), spec(1, -1), spec(1, 0), spec(1, 1), spec(2, -1), spec(2, 0), spec(2, 1),
                  pl.BlockSpec((DIL_HEADS, qb, 3 * qb), lambda b, r, j: (gi, 0, 0))],
        out_specs=[pl.BlockSpec((qb, hw), lambda b, r, j: (row_block(b, r, j), 0)),
                   pl.BlockSpec((1, qb, DIL_HEADS), lambda b, r, j: (row_block(b, r, j), 0, 0))],
        out_shape=[jax.ShapeDtypeStruct((m, hw), BF16),
                   jax.ShapeDtypeStruct((m // qb, qb, DIL_HEADS), F32)],
        compiler_params=_cparams("parallel", "parallel", "arbitrary"),
    )(proj, proj, proj, proj, proj, proj, proj, bias)
    lse = lse.reshape(m // tm, dil, tm // dil, DIL_HEADS)
    return o, jnp.transpose(lse, (0, 2, 1, 3)).reshape(m, DIL_HEADS)


def _dil_out_kernel(o0_ref, o1_ref, o2_ref, l0_ref, l1_ref, l2_ref, w_ref, res_ref, out_ref, comb_ref, tok_ref, *,
                    dils):
    @pl.when(pl.program_id(1) == 0)
    def _():
        l0, l1, l2 = l0_ref[...], l1_ref[...], l2_ref[...]
        m = jnp.maximum(jnp.maximum(l0, l1), l2)
        e0, e1, e2 = jnp.exp(l0 - m), jnp.exp(l1 - m), jnp.exp(l2 - m)
        den = e0 + e1 + e2
        wts = (e0 / den, e1 / den, e2 / den)
        rows = comb_ref.shape[0]
        for h in range(DIL_HEADS):
            hs = slice(h * DIL_DH, (h + 1) * DIL_DH)
            total = None
            for g, (o_ref, dil) in enumerate(zip((o0_ref, o1_ref, o2_ref), dils)):
                if dil == 1:
                    o = o_ref[:, hs].astype(F32)
                else:
                    per = rows // dil
                    for r in range(dil):
                        tok_ref[g, pl.ds(r, per, stride=dil), :] = o_ref[r * per:(r + 1) * per, hs].astype(F32)
                    o = tok_ref[g]
                term = wts[g][:, h:h + 1] * o
                total = term if total is None else total + term
            comb_ref[:, hs] = total.astype(BF16)

    out_ref[...] = res_ref[...] + _dot(comb_ref[...], w_ref[...].astype(BF16))


def _dil_out(os_, lses, w, res, *, tn=512):
    m, k = os_[0].shape
    w_stack, layer = w
    n = w_stack.shape[2]
    tm = min(ROW_TILE, m)
    ospec = pl.BlockSpec((tm, k), lambda i, j: (i, 0))
    lspec = pl.BlockSpec((tm, DIL_HEADS), lambda i, j: (i, 0))
    dils = tuple(dil for _, dil in DIL_PATTERNS)
    return pl.pallas_call(
        functools.partial(_dil_out_kernel, dils=dils), name="dil_out", grid=(m // tm, n // tn),
        in_specs=[ospec, ospec, ospec, lspec, lspec, lspec,
                  pl.BlockSpec((None, k, tn), lambda i, j: (layer, 0, j)),
                  pl.BlockSpec((tm, tn), lambda i, j: (i, j))],
        out_specs=pl.BlockSpec((tm, tn), lambda i, j: (i, j)),
        out_shape=jax.ShapeDtypeStruct((m, n), F32),
        scratch_shapes=[pltpu.VMEM((tm, k), BF16), pltpu.VMEM((len(dils), tm, DIL_DH), F32)],
        compiler_params=_cparams("parallel", "arbitrary"),
    )(*os_, *lses, w_stack, res)


def _dilated_mixer(x2, nw, w_in, w_out, rel_table, batch, seq):
    gw = w_in[0].shape[2] // len(DIL_PATTERNS)
    bias = _dil_bias(rel_table)
    outs = []
    for gi, (_, dil) in enumerate(DIL_PATTERNS):
        proj = _norm_proj(x2, nw, w_in, n_out=gw, tn=PROJ_COL_TILE, out_dtype=BF16, col_off=gi * gw,
                          residue_major=dil)
        outs.append(_dil_group(proj, bias, gi, dil, batch, seq))
    return _dil_out([o for o, _ in outs], [l for _, l in outs], w_out, x2)


DN_CHUNK = 64
DN_REP = DN_V_HEADS // DN_K_HEADS
DN_INST = 2 * DN_REP
DN_ROWS = DN_INST * DN_CHUNK
DN_PREP_GROUP = 8
CONV_TILE = 256
CONV_HALO = 8


def _split3_bf16(x):
    hi = x.astype(BF16)
    r = x - hi.astype(F32)
    mid = r.astype(BF16)
    return hi, mid, (r - mid.astype(F32)).astype(BF16)


def _softplus(x):
    return jnp.maximum(x, 0.0) + jnp.log1p(jnp.exp(-jnp.abs(x)))


def _deltanet_kernel(q_ref, k_ref, v_ref, z_ref, tail_ref, cwq_ref, cwk_ref, cwv_ref, alog_ref, dtb_ref, nw_ref,
                     o_ref, qn_ref, kn_ref, vc_ref, mq_ref, bo_ref, state_ref, oacc_ref):
    seq = q_ref.shape[0]
    nc = seq // DN_CHUNK
    c64 = DN_CHUNK
    dk = DN_DK

    pads = [slice(p * dk, (p + 1) * dk) for p in range(oacc_ref.shape[1] // dk)]
    oacc_ref[0:CONV_HALO, :] = jnp.zeros((CONV_HALO, oacc_ref.shape[1]), F32)
    oacc_ref[CONV_HALO + seq:, :] = jnp.zeros((CONV_HALO, oacc_ref.shape[1]), F32)

    def conv_into(jobs):
        ws = []
        for p, (src_ref, col, w_ref, _, _) in enumerate(jobs):
            oacc_ref[CONV_HALO:CONV_HALO + seq, pads[p]] = src_ref[:, col:col + dk].astype(F32)
            ws.append(w_ref[:, col:col + dk])

        def tile(r, carry):
            start = pl.multiple_of(r * CONV_TILE, CONV_TILE)
            ys = []
            for p in range(len(jobs)):
                win = oacc_ref[pl.ds(start, CONV_TILE + 2 * CONV_HALO), pads[p]]
                y = jnp.zeros((CONV_TILE, dk), F32)
                for j in range(DN_CONV):
                    lo = CONV_HALO + j - DN_CONV // 2
                    y = y + win[lo:lo + CONV_TILE, :] * ws[p][j:j + 1, :]
                ys.append(_silu(y))
            for y, (_, col, _, dst_ref, l2_scale) in zip(ys, jobs):
                if l2_scale is not None:
                    y = y * (lax.rsqrt(jnp.sum(y * y, axis=1, keepdims=True) + EPS) * l2_scale)
                dst_ref[pl.ds(start, CONV_TILE), col:col + dk] = y.astype(dst_ref.dtype)
            return carry

        lax.fori_loop(0, seq // CONV_TILE, tile, 0)

    conv_into([(q_ref, 0, cwq_ref, qn_ref, DN_DK ** -0.5), (k_ref, 0, cwk_ref, kn_ref, 1.0)])
    conv_into([(v_ref, vl * DN_DV, cwv_ref, vc_ref, None) for vl in range(DN_REP)])

    rr = lax.broadcasted_iota(jnp.int32, (DN_ROWS, DN_ROWS), 0)
    cc = lax.broadcasted_iota(jnp.int32, (DN_ROWS, DN_ROWS), 1)
    same = (rr // c64) == (cc // c64)
    fwd_rows = rr < DN_REP * c64
    strict = same & ((fwd_rows & (rr > cc)) | (jnp.logical_not(fwd_rows) & (rr < cc)))
    eye = (rr == cc)
    eye_f = eye.astype(F32)
    r64 = lax.broadcasted_iota(jnp.int32, (c64, DN_ROWS), 0)
    c64i = lax.broadcasted_iota(jnp.int32, (c64, DN_ROWS), 1)
    eye_row = (r64 == (c64i % c64)).astype(F32)
    li = lax.broadcasted_iota(jnp.int32, (c64, c64), 0)
    lj = lax.broadcasted_iota(jnp.int32, (c64, c64), 1)
    tril = (lj <= li).astype(BF16)
    triu = (lj >= li).astype(BF16)
    lane_all = lax.broadcasted_iota(jnp.int32, (c64, LANES), 1)
    hk = pl.program_id(1)
    beta_cols = [(i // DN_REP) * DN_V_HEADS + hk * DN_REP + i % DN_REP for i in range(DN_INST)]
    gate_cols = [2 * DN_V_HEADS + col for col in beta_cols]
    row_chain = lax.broadcasted_iota(jnp.int32, (1, DN_ROWS), 1) // c64
    tall_same = (lax.broadcasted_iota(jnp.int32, (DN_INST * dk, DN_ROWS), 0) // dk
                 == lax.broadcasted_iota(jnp.int32, (DN_INST * dk, DN_ROWS), 1) // c64)
    e_r = lax.broadcasted_iota(jnp.int32, (DN_INST * dk, dk), 0)
    e_c = lax.broadcasted_iota(jnp.int32, (DN_INST * dk, dk), 1)
    eye_tall = ((e_r % dk) == e_c).astype(F32)
    neg_a = -jnp.exp(alog_ref[...])
    dtb = dtb_ref[...]

    def stack(cols):
        return jnp.concatenate(cols, axis=0)

    def blockdiag(rows_):
        return jnp.where(same, jnp.concatenate([rows_] * DN_INST, axis=0), 0.0)

    def chunk_maps(c):
        rows = pl.ds(pl.multiple_of(c * c64, c64), c64)
        tl = tail_ref[rows, :]
        beta_all = 1.0 / (1.0 + jnp.exp(-tl))
        g_all = neg_a * _softplus(tl + dtb)
        parts = _split3_bf16(g_all)
        gcf = _dot(tril, parts[0]) + (_dot(tril, parts[1]) + _dot(tril, parts[2]))
        gcb = _dot(triu, parts[0]) + (_dot(triu, parts[1]) + _dot(triu, parts[2]))
        yield

        def pick(a, col):
            return jnp.sum(jnp.where(lane_all == col, a, 0.0), axis=1, keepdims=True)

        beta_b = jnp.broadcast_to(stack([pick(beta_all, beta_cols[i]) for i in range(DN_INST)]), (DN_ROWS, dk))
        gc_b = jnp.broadcast_to(stack([pick(gcf if i < DN_REP else gcb, gate_cols[i]) for i in range(DN_INST)]),
                                (DN_ROWS, dk))
        g_tot = jnp.sum(g_all, axis=0, keepdims=True)
        gtot = [jnp.sum(jnp.where(lane_all[0:1] == gate_cols[i], g_tot, 0.0), axis=1, keepdims=True)
                for i in range(DN_INST)]
        gtot_b = stack([jnp.broadcast_to(t, (c64, dk)) for t in gtot])
        egc_b = jnp.exp(gc_b)

        kb = kn_ref[rows, :]
        qb = qn_ref[rows, :]
        vb = vc_ref[rows, :]
        k4 = stack([kb.astype(F32)] * DN_INST)
        q4 = stack([qb.astype(F32)] * DN_INST)
        v4 = stack([vb[:, (i % DN_REP) * DN_DV:(i % DN_REP + 1) * DN_DV] for i in range(DN_INST)]).astype(F32)
        k4t = k4.T

        gmat = jnp.concatenate([gc_b] * (DN_ROWS // dk), axis=1)
        grow = jnp.sum(jnp.where(eye, gmat, 0.0), axis=0, keepdims=True)
        decay = jnp.exp(jnp.where(strict, gmat - grow, NEG_INF))
        k4t_b = k4t.astype(BF16)
        kk_row = _dot(kb, k4t_b)
        qk_row = _dot(qb, k4t_b)
        yield
        n_bd = jnp.concatenate([beta_b] * (DN_ROWS // dk), axis=1) * stack([kk_row] * DN_INST) * decay
        qk_bd = stack([qk_row] * DN_INST) * (decay + eye_f)

        r_pow = -(n_bd[0:c64] + n_bd[c64:2 * c64] + n_bd[2 * c64:3 * c64] + n_bd[3 * c64:4 * c64])
        t_row = eye_row + r_pow
        r_pow = _dot(r_pow.astype(BF16), (-n_bd).astype(BF16))
        yield
        for _ in range(4):
            p_bd = blockdiag(r_pow).astype(BF16)
            rt = _dot(jnp.concatenate([r_pow, t_row], axis=0).astype(BF16), p_bd)
            yield
            r_pow = rt[0:c64]
            t_row = t_row + rt[c64:]
        t_row = t_row + _dot(t_row.astype(BF16), blockdiag(r_pow).astype(BF16))
        yield

        rhs = jnp.concatenate([k4 * (beta_b * egc_b), v4 * beta_b], axis=1).astype(BF16)
        wu = _dot(blockdiag(t_row).astype(BF16), rhs).astype(BF16)
        yield
        qk_wu = _dot(qk_bd.astype(BF16), wu)
        gtot_row = jnp.zeros((1, DN_ROWS), F32)
        for i in range(DN_INST):
            gtot_row = jnp.where(row_chain == i, gtot[i], gtot_row)
        kdt = k4t * jnp.exp(gtot_row - grow)
        kdt_wide = jnp.where(tall_same, stack([kdt] * DN_INST), 0.0).astype(BF16)
        kw = _dot(kdt_wide, wu)
        yield
        qp = (q4 * egc_b - qk_wu[:, :dk]).astype(BF16)
        op = qk_wu[:, dk:].astype(BF16)
        dl = stack([jnp.broadcast_to(jnp.exp(t), (dk, dk)) for t in gtot])
        return qp, op, (dl * eye_tall - kw[:, :dk]).astype(BF16), kw[:, dk:].astype(BF16)

    def run_interleaved(gens):
        results = [None] * len(gens)
        while any(r is None for r in results):
            for u, gen in enumerate(gens):
                if results[u] is None:
                    try:
                        next(gen)
                    except StopIteration as done:
                        results[u] = done.value
        return results

    half_group = DN_PREP_GROUP // 2
    n_groups = nc // DN_PREP_GROUP

    def prep_group(g):
        low = [g * half_group + u for u in range(half_group)]
        cs = low + [nc - 1 - c for c in low]
        maps = run_interleaved([chunk_maps(c) for c in cs])
        for c, (qp, op, mt, bb) in zip(cs, maps):
            mq_ref[c] = stack([x for i in range(DN_INST)
                               for x in (mt[i * dk:(i + 1) * dk], qp[i * c64:(i + 1) * c64])])
            bo_ref[c] = stack([x for i in range(DN_INST)
                               for x in (bb[i * dk:(i + 1) * dk], op[i * c64:(i + 1) * c64])])

    def sweep_group(g):
        for u in range(half_group):
            j = g * half_group + u
            for i in range(DN_INST):
                c = j if i < DN_REP else nc - 1 - j
                span = slice(i * (dk + c64), (i + 1) * (dk + c64))
                res = _dot(mq_ref[c, span, :], state_ref[i].astype(BF16)) + bo_ref[c, span, :].astype(F32)
                state_ref[i] = res[:dk]
                rows = pl.ds(pl.multiple_of(CONV_HALO + c * c64, CONV_HALO), c64)
                vcols = slice((i % DN_REP) * DN_DV, (i % DN_REP + 1) * DN_DV)
                oacc_ref[rows, vcols] = oacc_ref[rows, vcols] + res[dk:]

    state_ref[...] = jnp.zeros_like(state_ref)
    oacc_ref[...] = jnp.zeros_like(oacc_ref)
    prep_group(0)

    def sweep_and_prep(g, carry):
        sweep_group(g - 1)
        prep_group(g)
        return carry

    lax.fori_loop(1, n_groups, sweep_and_prep, 0)

    def sweep_only(g, carry):
        sweep_group(g)
        return carry

    lax.fori_loop(n_groups - 1, 2 * n_groups, sweep_only, 0)

    def finish(r, carry):
        rows = pl.ds(pl.multiple_of(r * CONV_TILE, CONV_TILE), CONV_TILE)
        acc_rows = pl.ds(pl.multiple_of(CONV_HALO + r * CONV_TILE, CONV_HALO), CONV_TILE)
        for vl in range(DN_REP):
            vcols = slice(vl * DN_DV, (vl + 1) * DN_DV)
            o = _rms_rows(oacc_ref[acc_rows, vcols]) * nw_ref[...]
            o_ref[rows, vcols] = (o * _silu(z_ref[rows, vcols].astype(F32))).astype(o_ref.dtype)
        return carry

    lax.fori_loop(0, seq // CONV_TILE, finish, 0)


def _deltanet_core(proj, tail, conv_w, a_log, dt_bias, norm_w, batch, seq):
    gates = 2 * DN_V_HEADS
    alog = jnp.concatenate([jnp.zeros((gates,), F32), a_log.reshape(gates)]).reshape(1, LANES)
    dtb = jnp.concatenate([jnp.zeros((gates,), F32), dt_bias.reshape(gates)]).reshape(1, LANES)
    nc = seq // DN_CHUNK
    assert seq % (DN_CHUNK * DN_PREP_GROUP) == 0 and seq % CONV_TILE == 0
    kq = DN_K_HEADS
    vw = DN_REP * DN_DV
    kv = 2 * DN_K_HEADS * DN_DK // vw
    kz = kv + DN_K_HEADS
    once = pl.Buffered(1)
    return pl.pallas_call(
        _deltanet_kernel, name="deltanet_core", grid=(batch, DN_K_HEADS),
        in_specs=[pl.BlockSpec((seq, DN_DK), lambda b, h: (b, h), pipeline_mode=once),
                  pl.BlockSpec((seq, DN_DK), lambda b, h: (b, kq + h), pipeline_mode=once),
                  pl.BlockSpec((seq, vw), lambda b, h: (b, kv + h), pipeline_mode=once),
                  pl.BlockSpec((seq, vw), lambda b, h: (b, kz + h), pipeline_mode=once),
                  pl.BlockSpec((seq, LANES), lambda b, h: (b, 0), pipeline_mode=once),
                  pl.BlockSpec((DN_CONV, DN_DK), lambda b, h: (0, h)),
                  pl.BlockSpec((DN_CONV, DN_DK), lambda b, h: (0, kq + h)),
                  pl.BlockSpec((DN_CONV, vw), lambda b, h: (0, kv + h)),
                  pl.BlockSpec((1, LANES), lambda b, h: (0, 0)),
                  pl.BlockSpec((1, LANES), lambda b, h: (0, 0)),
                  pl.BlockSpec((1, DN_DV), lambda b, h: (0, 0))],
        out_specs=pl.BlockSpec((seq, vw), lambda b, h: (b, h)),
        out_shape=jax.ShapeDtypeStruct((batch * seq, DN_V_HEADS * DN_DV), BF16),
        scratch_shapes=[pltpu.VMEM((seq, DN_DK), BF16), pltpu.VMEM((seq, DN_DK), BF16), pltpu.VMEM((seq, vw), BF16),
                        pltpu.VMEM((nc, DN_INST * (DN_DK + DN_CHUNK), DN_DK), BF16),
                        pltpu.VMEM((nc, DN_INST * (DN_DK + DN_CHUNK), DN_DV), BF16),
                        pltpu.VMEM((DN_INST, DN_DK, DN_DV), F32),
                        pltpu.VMEM((seq + 2 * CONV_HALO, vw), F32)],
        compiler_params=_cparams("parallel", "parallel"),
    )(proj, proj, proj, proj, tail, conv_w, conv_w, conv_w, alog, dtb, norm_w.reshape(1, DN_DV))


def _deltanet_mixer(x2, nw, w_in, conv_w, a_log, dt_bias, norm_w, w_out, batch, seq):
    conv_dim = 2 * DN_K_HEADS * DN_DK + DN_V_HEADS * DN_DV
    main = conv_dim + DN_V_HEADS * DN_DV
    proj = _norm_proj(x2, nw, w_in, n_out=main, tn=PROJ_COL_TILE, out_dtype=BF16)
    tail = _norm_proj(x2, nw, w_in, n_out=LANES, tn=LANES, out_dtype=F32, col_off=main)
    o = _deltanet_core(proj, tail, conv_w, a_log, dt_bias, norm_w, batch, seq)
    return _out_proj(o, w_out, x2)


def kernel(x, norm_mix_w, norm_ffn_w, final_norm_w, rel_bias_table, ret_w_in, ret_w_out, dil_w_in, dil_w_out,
           dn_w_in, dn_conv_w, dn_a_log, dn_dt_bias, dn_norm_w, dn_w_out, moe_w_router, moe_w_gate, moe_w_up,
           moe_w_down):
    batch, seq, d = x.shape
    x2 = x.reshape(batch * seq, d)
    for i in range(norm_mix_w.shape[0]):
        j = i // N_MIXERS
        kind = i % N_MIXERS
        if kind == 0:
            x2 = _retention_mixer(x2, norm_mix_w[i], (ret_w_in, j), (ret_w_out, j), batch, seq)
        elif kind == 1:
            x2 = _dilated_mixer(x2, norm_mix_w[i], (dil_w_in, j), (dil_w_out, j), rel_bias_table, batch, seq)
        else:
            x2 = _deltanet_mixer(x2, norm_mix_w[i], (dn_w_in, j), dn_conv_w[j], dn_a_log[j], dn_dt_bias[j],
                                 dn_norm_w[j], (dn_w_out, j), batch, seq)
        x2 = _moe(x2, norm_ffn_w[i], moe_w_router[i], moe_w_gate, moe_w_up, moe_w_down, i, batch, seq)
    return _final_norm(x2, final_norm_w).reshape(batch, seq, d)
```
